```python
import math, functools
import jax, jax.numpy as jnp
from jax import lax
import numpy as np

D_MODEL = 1024
BATCH = 32
SEQ = 2048
DEPTH = 4

CTX_LEN = 256
GRID_W = 64
N_MOD = 6
RMS_EPS = 1e-6
N_EVEN = (DEPTH + 1) // 2
N_ODD = DEPTH // 2

SSD_W = D_MODEL
SSD_HEAD_DIM = 64
SSD_HEADS = SSD_W // SSD_HEAD_DIM
SSD_GROUPS = 2
SSD_HPG = SSD_HEADS // SSD_GROUPS
SSD_STATE = 128
SSD_BC = SSD_GROUPS * SSD_STATE
SSD_XBC = SSD_W + 2 * SSD_BC
SSD_CONV = 4
SSD_CHUNK = 128
LRU_W = D_MODEL
LRU_BLOCKS = 8
LRU_BLOCK_W = LRU_W // LRU_BLOCKS
LRU_CONV = 4
LRU_C = 8.0
EV_SPLITS = (SSD_W, SSD_W + SSD_XBC, SSD_W + SSD_XBC + 2 * SSD_HEADS, SSD_W + SSD_XBC + 2 * SSD_HEADS + LRU_W)
EV_IN = SSD_W + SSD_XBC + 2 * SSD_HEADS + 2 * LRU_W
EV_MIX = SSD_W + LRU_W
HG_W = 3 * D_MODEL // 4
HG_DK = 128
HG_DV = 128
HG_HEADS = HG_W // HG_DK
HG_CHUNK = 32
S5_W = D_MODEL // 4
S5_GROUP_CH = 16
S5_GROUPS = S5_W // S5_GROUP_CH
S5_STATE = 64
OD_SPLITS = (HG_W, 2 * HG_W, 3 * HG_W, 4 * HG_W, 5 * HG_W)
OD_IN = 5 * HG_W + S5_W
OD_MIX = HG_W + S5_W
D_FF = 2816
FFN_CONV = 3

kernel_name = "hybrid_ssd_rglru_hgrn2_s5_diffusion_trunk"


def rms_norm(t, g):
    tf = t.astype(jnp.float32)
    tf = tf * lax.rsqrt(jnp.mean(jnp.square(tf), axis=-1, keepdims=True) + RMS_EPS)
    return (tf * g.astype(jnp.float32)).astype(t.dtype)


def modulate(t, shift, scale):
    return t * (1 + scale) + shift


def depthwise_conv1d(t, w, b):
    k = w.shape[0]
    y = lax.conv_general_dilated(t, w[:, None, :], (1,), [((k - 1) // 2, k // 2)],
                                 dimension_numbers=("NWC", "WIO", "NWC"),
                                 feature_group_count=t.shape[-1])
    return y + b


def depthwise_conv2d(t, w, b):
    y = lax.conv_general_dilated(t, w[:, :, None, :], (1, 1), [(1, 1), (1, 1)],
                                 dimension_numbers=("NHWC", "HWIO", "NHWC"),
                                 feature_group_count=t.shape[-1])
    return y + b


def to_chunks(t, size):
    b, n = t.shape[:2]
    return jnp.swapaxes(t.reshape((b, n // size, size) + t.shape[2:]), 0, 1)


def from_chunks(t):
    t = jnp.swapaxes(t, 0, 1)
    return t.reshape((t.shape[0], t.shape[1] * t.shape[2]) + t.shape[3:])


def run_direction(scan_fn, ctx_seqs, lat_seqs, h0, reverse):
    flip = (lambda t: jnp.flip(t, axis=1)) if reverse else (lambda t: t)
    y_c, h_c = scan_fn(*map(flip, ctx_seqs), h0)
    y_x, _ = scan_fn(*map(flip, lat_seqs), h_c)
    return flip(y_c), flip(y_x)


def ssd_scan(x, dt, bm, cm, h0, a):
    log_a = dt * a
    xdt = x * dt[..., None]
    mask = jnp.tril(jnp.ones((SSD_CHUNK, SSD_CHUNK), dtype=bool))

    def step(h, inp):
        xc, lac, bc, cc = inp
        cum = jnp.cumsum(lac, axis=1)
        seg = cum[:, :, None] - cum[:, None]
        decay = jnp.exp(jnp.where(mask[None, :, :, None, None], seg, -jnp.inf))
        cb = jnp.einsum("blgn,bsgn->blsg", cc, bc)
        y = jnp.einsum("blsgh,bsghp->blghp", cb[..., None] * decay, xc)
        y = y + jnp.einsum("blgn,bghpn->blghp", cc, h) * jnp.exp(cum)[..., None]
        to_end = jnp.exp(cum[:, -1:] - cum)
        h = h * jnp.exp(cum[:, -1])[..., None, None] + jnp.einsum(
            "bsgn,bsghp->bghpn", bc, xc * to_end[..., None])
        return h, y

    h, ys = lax.scan(step, h0, tuple(to_chunks(t, SSD_CHUNK) for t in (xdt, log_a, bm, cm)))
    return from_chunks(ys), h


def linear_recurrence(a, b, h0):
    b = b.at[:, 0].add(a[:, 0] * h0)

    def combine(l, r):
        return l[0] * r[0], r[0] * l[1] + r[1]

    return lax.associative_scan(combine, (a, b), axis=1)[1]


def rglru_scan(u, h0, w_a, b_a, w_i, b_i, lam):
    r = jax.nn.sigmoid(jnp.einsum("btnk,nkj->btnj", u, w_a) + b_a)
    i = jax.nn.sigmoid(jnp.einsum("btnk,nkj->btnj", u, w_i) + b_i)
    log_a = -LRU_C * jax.nn.softplus(-lam) * r
    a = jnp.exp(log_a)
    bx = jnp.sqrt(-jnp.expm1(2 * log_a)) * (i * u)
    h = linear_recurrence(a, bx, h0)
    return h, h[:, -1]


def hgrn_scan(q, k, log_f, v, s0):
    mask = jnp.tril(jnp.ones((HG_CHUNK, HG_CHUNK), dtype=bool))

    def step(s, inp):
        qc, kc, gc, vc = inp
        cum = jnp.cumsum(gc, axis=1)
        seg = cum[:, :, None] - cum[:, None]
        decay = jnp.exp(jnp.where(mask[None, :, :, None, None], seg, -jnp.inf))
        att = jnp.einsum("blhk,blshk->blsh", qc, decay * kc[:, None])
        y = jnp.einsum("blsh,bshv->blhv", att, vc)
        y = y + jnp.einsum("blhk,bhkv->blhv", qc * jnp.exp(cum), s)
        s = s * jnp.exp(cum[:, -1])[..., None] + jnp.einsum(
            "bshk,bshv->bhkv", kc * jnp.exp(cum[:, -1:] - cum), vc)
        return s, y

    s, ys = lax.scan(step, s0, tuple(to_chunks(t, HG_CHUNK) for t in (q, k, log_f, v)))
    return from_chunks(ys), s


def s5_scan(u, h0, lam_re, lam_im, log_step, b_re, b_im, c_re, c_im):
    step = jnp.exp(log_step)[:, None]
    mag = jnp.exp(lam_re * step)
    ar, ai = mag * jnp.cos(lam_im * step), mag * jnp.sin(lam_im * step)
    den = lam_re * lam_re + lam_im * lam_im
    zr = ((ar - 1) * lam_re + ai * lam_im) / den
    zi = (ai * lam_re - (ar - 1) * lam_im) / den
    bbr = zr[..., None] * b_re - zi[..., None] * b_im
    bbi = zr[..., None] * b_im + zi[..., None] * b_re
    xr = jnp.einsum("btgk,gpk->btgp", u, bbr)
    xi = jnp.einsum("btgk,gpk->btgp", u, bbi)
    h0r, h0i = h0
    xr = xr.at[:, 0].add(ar * h0r - ai * h0i)
    xi = xi.at[:, 0].add(ar * h0i + ai * h0r)
    a_r = jnp.broadcast_to(ar, xr.shape)
    a_i = jnp.broadcast_to(ai, xi.shape)

    def combine(l, r):
        lar, lai, lbr, lbi = l
        rar, rai, rbr, rbi = r
        return (lar * rar - lai * rai, lar * rai + lai * rar,
                rar * lbr - rai * lbi + rbr, rar * lbi + rai * lbr + rbi)

    _, _, hr, hi = lax.associative_scan(combine, (a_r, a_i, xr, xi), axis=1)
    y = jnp.einsum("btgp,gkp->btgk", hr, c_re) - jnp.einsum("btgp,gkp->btgk", hi, c_im)
    return y, (hr[:, -1], hi[:, -1])


def even_mixer(hx, hc, w_in, w_out, ssd_conv_w, ssd_conv_b, ssd_dt_bias, ssd_a_log, ssd_d, ssd_norm_g,
               lru_conv_w, lru_conv_b, lru_w_a, lru_b_a, lru_w_i, lru_b_i, lru_lam, need_ctx):
    ssd_a = -jnp.exp(ssd_a_log).reshape(2, SSD_GROUPS, SSD_HPG)

    def prepare(h):
        bsz, t = h.shape[:2]
        z, xbc, dt, gy, u = jnp.split(h @ w_in, EV_SPLITS, axis=-1)
        xbc = jax.nn.silu(depthwise_conv1d(xbc, ssd_conv_w, ssd_conv_b))
        xs, bm, cm = jnp.split(xbc, [SSD_W, SSD_W + SSD_BC], axis=-1)
        dt = jax.nn.softplus(dt.reshape(bsz, t, 2, SSD_HEADS) + ssd_dt_bias)
        u = depthwise_conv1d(u, lru_conv_w, lru_conv_b)
        return {"z": z, "gy": gy,
                "x": xs.reshape(bsz, t, SSD_GROUPS, SSD_HPG, SSD_HEAD_DIM),
                "b": bm.reshape(bsz, t, SSD_GROUPS, SSD_STATE),
                "c": cm.reshape(bsz, t, SSD_GROUPS, SSD_STATE),
                "dt": dt.reshape(bsz, t, 2, SSD_GROUPS, SSD_HPG),
                "u": u.reshape(bsz, t, LRU_BLOCKS, LRU_BLOCK_W)}

    pc, px = prepare(hc), prepare(hx)
    bsz = hx.shape[0]
    ssd_h0 = jnp.zeros((bsz, SSD_GROUPS, SSD_HPG, SSD_HEAD_DIM, SSD_STATE), hx.dtype)
    lru_h0 = jnp.zeros((bsz, LRU_BLOCKS, LRU_BLOCK_W), hx.dtype)
    ssd_dirs, lru_dirs = [], []
    for d, reverse in enumerate((False, True)):
        ssd_dirs.append(run_direction(
            functools.partial(ssd_scan, a=ssd_a[d]),
            (pc["x"], pc["dt"][:, :, d], pc["b"], pc["c"]),
            (px["x"], px["dt"][:, :, d], px["b"], px["c"]), ssd_h0, reverse))
        lru_dirs.append(run_direction(
            functools.partial(rglru_scan, w_a=lru_w_a[d], b_a=lru_b_a[d].reshape(LRU_BLOCKS, LRU_BLOCK_W),
                              w_i=lru_w_i[d], b_i=lru_b_i[d].reshape(LRU_BLOCKS, LRU_BLOCK_W),
                              lam=lru_lam[d].reshape(LRU_BLOCKS, LRU_BLOCK_W)),
            (pc["u"],), (px["u"],), lru_h0, reverse))

    def finish(p, ssd_y, lru_h):
        bsz_, t = p["z"].shape[:2]
        y = (ssd_y + ssd_d.reshape(SSD_GROUPS, SSD_HPG, 1) * p["x"]).reshape(bsz_, t, SSD_W)
        y = rms_norm(y * jax.nn.silu(p["z"]), ssd_norm_g)
        r = lru_h.reshape(bsz_, t, LRU_W) * jax.nn.gelu(p["gy"])
        return jnp.concatenate([y, r], axis=-1) @ w_out

    out_x = finish(px, ssd_dirs[0][1] + ssd_dirs[1][1], lru_dirs[0][1] + lru_dirs[1][1])
    out_c = finish(pc, ssd_dirs[0][0] + ssd_dirs[1][0], lru_dirs[0][0] + lru_dirs[1][0]) if need_ctx else None
    return out_x, out_c


def odd_mixer(hx, hc, lower_bound, w_in, w_out, hg_norm_g, s5_lam_re, s5_lam_im, s5_log_step,
              s5_b_re, s5_b_im, s5_c_re, s5_c_im, s5_d, s5_glu_w, s5_glu_b, need_ctx):
    lb = lower_bound.reshape(HG_HEADS, HG_DK)

    def prepare(h):
        bsz, t = h.shape[:2]
        q, f_fwd, f_bwd, v, g, u = jnp.split(h @ w_in, OD_SPLITS, axis=-1)
        gates = []
        for f in (f_fwd, f_bwd):
            f = f.reshape(bsz, t, HG_HEADS, HG_DK)
            log_f = jnp.log(lb + (1 - lb) * jax.nn.sigmoid(f))
            k = (1 - lb) * jax.nn.sigmoid(-f)
            gates.append((k, log_f))
        return {"q": jax.nn.silu(q.reshape(bsz, t, HG_HEADS, HG_DK)),
                "v": v.reshape(bsz, t, HG_HEADS, HG_DV), "g": g, "gates": gates,
                "u": u.reshape(bsz, t, S5_GROUPS, S5_GROUP_CH)}

    pc, px = prepare(hc), prepare(hx)
    bsz = hx.shape[0]
    hg_h0 = jnp.zeros((bsz, HG_HEADS, HG_DK, HG_DV), hx.dtype)
    s5_zero = jnp.zeros((bsz, S5_GROUPS, S5_STATE), hx.dtype)
    hg_dirs, s5_dirs = [], []
    for d, reverse in enumerate((False, True)):
        hg_dirs.append(run_direction(
            hgrn_scan, (pc["q"], pc["gates"][d][0], pc["gates"][d][1], pc["v"]),
            (px["q"], px["gates"][d][0], px["gates"][d][1], px["v"]), hg_h0, reverse))
        s5_dirs.append(run_direction(
            functools.partial(s5_scan, lam_re=s5_lam_re[d], lam_im=s5_lam_im[d], log_step=s5_log_step[d],
                              b_re=s5_b_re, b_im=s5_b_im, c_re=s5_c_re[d], c_im=s5_c_im[d]),
            (pc["u"],), (px["u"],), (s5_zero, s5_zero), reverse))

    def finish(p, o, y):
        bsz_, t = p["g"].shape[:2]
        o = rms_norm(o, hg_norm_g) * jax.nn.silu(p["g"].reshape(bsz_, t, HG_HEADS, HG_DV))
        y = y + s5_d.reshape(S5_GROUPS, S5_GROUP_CH) * p["u"]
        y = jax.nn.gelu(y.reshape(bsz_, t, S5_W))
        y = y * jax.nn.sigmoid(y @ s5_glu_w + s5_glu_b)
        return jnp.concatenate([o.reshape(bsz_, t, HG_W), y], axis=-1) @ w_out

    out_x = finish(px, hg_dirs[0][1] + hg_dirs[1][1], s5_dirs[0][1] + s5_dirs[1][1])
    out_c = finish(pc, hg_dirs[0][0] + hg_dirs[1][0], s5_dirs[0][0] + s5_dirs[1][0]) if need_ctx else None
    return out_x, out_c


def conv_ffn(h, rows, w_gate, w_up, conv_w, conv_b, w_down):
    a = h @ w_gate
    if rows is None:
        a = depthwise_conv1d(a, conv_w[1], conv_b)
    else:
        bsz, t, f = a.shape
        a = depthwise_conv2d(a.reshape(bsz, rows, GRID_W, f), conv_w, conv_b).reshape(bsz, t, f)
    return (jax.nn.silu(a) * (h @ w_up)) @ w_down


def setup_inputs(seed: int = 0) -> dict:
    key = jax.random.key(seed)
    ks = iter(jax.random.split(key, 64))
    D = D_MODEL

    def nrm(shape, scale):
        return scale * jax.random.normal(next(ks), shape, jnp.float32)

    def uni(shape, lo, hi):
        return jax.random.uniform(next(ks), shape, jnp.float32, lo, hi)

    ssd_dt = jnp.exp(uni((N_EVEN, 2, SSD_HEADS), math.log(1e-3), math.log(1e-1)))
    lru_a = uni((N_EVEN, 2, LRU_W), 0.9, 0.999) ** (1.0 / LRU_C)
    n_idx = jnp.arange(S5_STATE, dtype=jnp.float32)
    return {
        "x": nrm((BATCH, SEQ, D), 1.0),
        "c": nrm((BATCH, D), 1.0),
        "ctx": nrm((BATCH, CTX_LEN, D), 1.0),
        "c_ctx": nrm((D,), 1.0),
        "w_mod": nrm((DEPTH, D, N_MOD * D), 0.5 * D ** -0.5),
        "b_mod": nrm((DEPTH, N_MOD * D), 0.02),
        "norm_mix_g": 1.0 + nrm((DEPTH, D), 0.02),
        "norm_ffn_g": 1.0 + nrm((DEPTH, D), 0.02),
        "final_norm_g": 1.0 + nrm((D,), 0.02),
        "ev_w_in": nrm((N_EVEN, D, EV_IN), D ** -0.5),
        "ev_w_out": nrm((N_EVEN, EV_MIX, D), EV_MIX ** -0.5),
        "ssd_conv_w": nrm((N_EVEN, SSD_CONV, SSD_XBC), SSD_CONV ** -0.5),
        "ssd_conv_b": nrm((N_EVEN, SSD_XBC), 0.02),
        "ssd_dt_bias": ssd_dt + jnp.log(-jnp.expm1(-ssd_dt)),
        "ssd_a_log": jnp.log(uni((N_EVEN, 2, SSD_HEADS), 1.0, 16.0)),
        "ssd_d": 1.0 + nrm((N_EVEN, SSD_HEADS), 0.1),
        "ssd_norm_g": 1.0 + nrm((N_EVEN, SSD_W), 0.02),
        "lru_conv_w": nrm((N_EVEN, LRU_CONV, LRU_W), LRU_CONV ** -0.5),
        "lru_conv_b": nrm((N_EVEN, LRU_W), 0.02),
        "lru_w_a": nrm((N_EVEN, 2, LRU_BLOCKS, LRU_BLOCK_W, LRU_BLOCK_W), LRU_BLOCK_W ** -0.5),
        "lru_b_a": nrm((N_EVEN, 2, LRU_W), 0.02),
        "lru_w_i": nrm((N_EVEN, 2, LRU_BLOCKS, LRU_BLOCK_W, LRU_BLOCK_W), LRU_BLOCK_W ** -0.5),
        "lru_b_i": nrm((N_EVEN, 2, LRU_W), 0.02),
        "lru_lam": jnp.log(lru_a) - jnp.log1p(-lru_a),
        "od_w_in": nrm((N_ODD, D, OD_IN), D ** -0.5),
        "od_w_out": nrm((N_ODD, OD_MIX, D), OD_MIX ** -0.5),
        "hg_lb_logits": nrm((DEPTH, HG_W), 0.1),
        "hg_norm_g": 1.0 + nrm((N_ODD, HG_HEADS, HG_DV), 0.02),
        "s5_lam_re": -0.5 + nrm((N_ODD, 2, S5_GROUPS, S5_STATE), 0.01),
        "s5_lam_im": math.pi * n_idx + nrm((N_ODD, 2, S5_GROUPS, S5_STATE), 0.01),
        "s5_log_step": uni((N_ODD, 2, S5_GROUPS), math.log(1e-3), math.log(1e-1)),
        "s5_b_re": nrm((N_ODD, S5_GROUPS, S5_STATE, S5_GROUP_CH), (2 * S5_GROUP_CH) ** -0.5),
        "s5_b_im": nrm((N_ODD, S5_GROUPS, S5_STATE, S5_GROUP_CH), (2 * S5_GROUP_CH) ** -0.5),
        "s5_c_re": nrm((N_ODD, 2, S5_GROUPS, S5_GROUP_CH, S5_STATE), 0.5),
        "s5_c_im": nrm((N_ODD, 2, S5_GROUPS, S5_GROUP_CH, S5_STATE), 0.5),
        "s5_d": nrm((N_ODD, S5_W), 1.0),
        "s5_glu_w": nrm((N_ODD, S5_W, S5_W), S5_W ** -0.5),
        "s5_glu_b": nrm((N_ODD, S5_W), 0.02),
        "ffn_w_gate": nrm((DEPTH, D, D_FF), D ** -0.5),
        "ffn_w_up": nrm((DEPTH, D, D_FF), D ** -0.5),
        "ffn_conv_w": nrm((DEPTH, FFN_CONV, FFN_CONV, D_FF), 1.0 / FFN_CONV),
        "ffn_conv_b": nrm((DEPTH, D_FF), 0.02),
        "ffn_w_down": nrm((DEPTH, D_FF, D), D_FF ** -0.5),
    }


def reference(x, c, ctx, c_ctx, w_mod, b_mod, norm_mix_g, norm_ffn_g, final_norm_g,
              ev_w_in, ev_w_out, ssd_conv_w, ssd_conv_b, ssd_dt_bias, ssd_a_log, ssd_d, ssd_norm_g,
              lru_conv_w, lru_conv_b, lru_w_a, lru_b_a, lru_w_i, lru_b_i, lru_lam,
              od_w_in, od_w_out, hg_lb_logits, hg_norm_g,
              s5_lam_re, s5_lam_im, s5_log_step, s5_b_re, s5_b_im, s5_c_re, s5_c_im, s5_d,
              s5_glu_w, s5_glu_b,
              ffn_w_gate, ffn_w_up, ffn_conv_w, ffn_conv_b, ffn_w_down):
    rows = x.shape[1] // GRID_W
    p = jax.nn.softmax(hg_lb_logits.astype(jnp.float32), axis=0)
    lower_bounds = (jnp.cumsum(p, axis=0) - p[0]).astype(hg_lb_logits.dtype)
    s_c = jax.nn.silu(c)
    s_cc = jax.nn.silu(c_ctx)
    for layer in range(DEPTH):
        last = layer == DEPTH - 1
        j = layer // 2
        mod_x = jnp.split((s_c @ w_mod[layer] + b_mod[layer])[:, None, :], N_MOD, axis=-1)
        mod_c = jnp.split(s_cc @ w_mod[layer] + b_mod[layer], N_MOD, axis=-1)
        hx = modulate(rms_norm(x, norm_mix_g[layer]), mod_x[0], mod_x[1])
        hc = modulate(rms_norm(ctx, norm_mix_g[layer]), mod_c[0], mod_c[1])
        if layer % 2 == 0:
            ox, oc = even_mixer(hx, hc, ev_w_in[j], ev_w_out[j], ssd_conv_w[j], ssd_conv_b[j],
                                ssd_dt_bias[j], ssd_a_log[j], ssd_d[j], ssd_norm_g[j],
                                lru_conv_w[j], lru_conv_b[j], lru_w_a[j], lru_b_a[j],
                                lru_w_i[j], lru_b_i[j], lru_lam[j], not last)
        else:
            ox, oc = odd_mixer(hx, hc, lower_bounds[layer], od_w_in[j], od_w_out[j], hg_norm_g[j],
                               s5_lam_re[j], s5_lam_im[j], s5_log_step[j], s5_b_re[j], s5_b_im[j],
                               s5_c_re[j], s5_c_im[j], s5_d[j], s5_glu_w[j], s5_glu_b[j], not last)
        x = x + mod_x[2] * ox
        fx = modulate(rms_norm(x, norm_ffn_g[layer]), mod_x[3], mod_x[4])
        x = x + mod_x[5] * conv_ffn(fx, rows, ffn_w_gate[layer], ffn_w_up[layer],
                                    ffn_conv_w[layer], ffn_conv_b[layer], ffn_w_down[layer])
        if not last:
            ctx = ctx + mod_c[2] * oc
            fc = modulate(rms_norm(ctx, norm_ffn_g[layer]), mod_c[3], mod_c[4])
            ctx = ctx + mod_c[5] * conv_ffn(fc, None, ffn_w_gate[layer], ffn_w_up[layer],
                                            ffn_conv_w[layer], ffn_conv_b[layer], ffn_w_down[layer])
    return rms_norm(x, final_norm_g)
```

```python
import functools
import math

import jax
import jax.numpy as jnp
from jax import lax
from jax.experimental import pallas as pl
from jax.experimental.pallas import tpu as pltpu

LANES = 128
RMS_EPS = 1e-6
N_MOD = 6
GRID_W = 64
SSD_HEAD_DIM = 64
SSD_HEADS = 16
SSD_GROUPS = 2
SSD_HPG = SSD_HEADS // SSD_GROUPS
SSD_STATE = 128
SSD_CHUNK = 128
LRU_BLOCKS = 8
LRU_C = 8.0
HG_HEADS = 6
S5_GROUPS = 16
S5_GROUP_CH = 16
S5_STATE = 64
VMEM_LIMIT = 56 * 1024 * 1024

BF16 = jnp.bfloat16
F32 = jnp.float32
HIGHEST = lax.Precision.HIGHEST


def _cparams(sem):
    return pltpu.CompilerParams(dimension_semantics=sem, vmem_limit_bytes=VMEM_LIMIT)


def _dot(a, b):
    return jnp.dot(a.astype(BF16), b.astype(BF16), preferred_element_type=F32)


def _dot32(a, b):
    return jnp.dot(a, b, preferred_element_type=F32, precision=HIGHEST)


def _sigmoid(v):
    return 1.0 / (1.0 + jnp.exp(-v))


def _silu(v):
    return v * _sigmoid(v)


def _gelu_tanh(v):
    return 0.5 * v * (1.0 + jnp.tanh(math.sqrt(2.0 / math.pi) * (v + 0.044715 * (v * v * v))))


def _softplus(v):
    return jnp.maximum(v, 0.0) + jnp.log(1.0 + jnp.exp(-jnp.abs(v)))


def _rms(v, g):
    return v * lax.rsqrt(jnp.mean(v * v, axis=-1, keepdims=True) + RMS_EPS) * g


def _norm_mod(xv, g, shift, scale):
    return _rms(xv, g) * (1.0 + scale) + shift


def _mod_kernel(s_ref, w_ref, b_ref, o_ref):
    o_ref[0, 0] = _dot(_silu(s_ref[...]), w_ref[0]) + b_ref[0, 0]


def _modulation(s, w_mod, b_mod):
    depth, d, _ = w_mod.shape
    rows = s.shape[0]
    return pl.pallas_call(
        _mod_kernel,
        grid=(depth, N_MOD),
        in_specs=[pl.BlockSpec((rows, d), lambda l, j: (0, 0)),
                  pl.BlockSpec((1, d, d), lambda l, j: (l, 0, j)),
                  pl.BlockSpec((1, 1, 1, d), lambda l, j: (l, j, 0, 0))],
        out_specs=pl.BlockSpec((1, 1, rows, d), lambda l, j: (l, j, 0, 0)),
        out_shape=jax.ShapeDtypeStruct((depth, N_MOD, rows, d), F32),
        compiler_params=_cparams(("arbitrary", "arbitrary")),
        name="modulation",
    )(s, w_mod, b_mod.reshape(depth, N_MOD, 1, d))


def _proj_kernel(x_ref, mod_ref, g_ref, w_ref, o_ref, *, nblk):
    m = mod_ref[0]
    h = _norm_mod(x_ref[0], g_ref[...], m[0:1], m[1:2]).astype(BF16)
    group = 4
    for b0 in range(0, nblk, group):
        nb = min(group, nblk - b0)
        r = jnp.dot(h, w_ref[:, b0 * LANES:(b0 + nb) * LANES], preferred_element_type=F32)
        for k in range(nb):
            o_ref[0, b0 + k] = r[:, k * LANES:(k + 1) * LANES]


def _project(x, mod, gain, w, tile):
    bsz, t, d = x.shape
    nblk = w.shape[1] // LANES
    tile = min(tile, t)
    mod_map = (lambda b, i: (b, 0, 0)) if mod.shape[0] == bsz else (lambda b, i: (0, 0, 0))
    return pl.pallas_call(
        functools.partial(_proj_kernel, nblk=nblk),
        grid=(bsz, t // tile),
        in_specs=[pl.BlockSpec((1, tile, d), lambda b, i: (b, i, 0)),
                  pl.BlockSpec((1, N_MOD, d), mod_map),
                  pl.BlockSpec((1, d), lambda b, i: (0, 0)),
                  pl.BlockSpec((d, nblk * LANES), lambda b, i: (0, 0), pipeline_mode=pl.Buffered(1))],
        out_specs=pl.BlockSpec((1, nblk, tile, LANES), lambda b, i: (b, 0, i, 0)),
        out_shape=jax.ShapeDtypeStruct((bsz, nblk, t, LANES), F32),
        compiler_params=_cparams(("parallel", "parallel")),
        name="project",
    )(x, mod, gain.reshape(1, d), w)


def _mix_out_kernel(x_ref, ma_ref, mb_ref, mod_ref, ng_ref, w_ref, o_ref, *, norm_first):
    m = mod_ref[0]
    pa = [ma_ref[0, k] for k in range(ma_ref.shape[1])]
    pb = [mb_ref[0, k].astype(BF16) for k in range(mb_ref.shape[1])]
    if norm_first:
        pa = [_rms(jnp.concatenate(pa, axis=-1), ng_ref[...]).astype(BF16)]
    else:
        pa = [p.astype(BF16) for p in pa]
    v = jnp.concatenate(pa + pb, axis=-1)
    o_ref[0] = x_ref[0] + m[2:3] * jnp.dot(v, w_ref[...], preferred_element_type=F32)


def _mix_out(x, mix_a, mix_b, mod, norm_gain, w, norm_first, tile):
    bsz, t, d = x.shape
    tile = min(tile, t)
    mod_map = (lambda b, i: (b, 0, 0)) if mod.shape[0] == bsz else (lambda b, i: (0, 0, 0))
    ng = norm_gain.reshape(1, -1)
    mix_spec = lambda a: pl.BlockSpec((1, a.shape[1], tile, LANES), lambda b, i: (b, 0, i, 0))
    return pl.pallas_call(
        functools.partial(_mix_out_kernel, norm_first=norm_first),
        grid=(bsz, t // tile),
        in_specs=[pl.BlockSpec((1, tile, d), lambda b, i: (b, i, 0)),
                  mix_spec(mix_a), mix_spec(mix_b),
                  pl.BlockSpec((1, N_MOD, d), mod_map),
                  pl.BlockSpec(ng.shape, lambda b, i: (0, 0)),
                  pl.BlockSpec(w.shape, lambda b, i: (0, 0), pipeline_mode=pl.Buffered(1))],
        out_specs=pl.BlockSpec((1, tile, d), lambda b, i: (b, i, 0)),
        out_shape=jax.ShapeDtypeStruct((bsz, t, d), F32),
        compiler_params=_cparams(("parallel", "parallel")),
        name="mix_out",
    )(x, mix_a, mix_b, mod, ng, w)


def _ffn_kernel(x_ref, xp_ref, xn_ref, mod_ref, g_ref, fg_ref, wg_ref, wu_ref, cw_ref, cb_ref, wd_ref,
                o_ref, fx_ref, acc_ref, *, tile, halo, grid_conv, final_norm):
    i = pl.program_id(1)
    f = pl.program_id(2)
    nt = pl.num_programs(1)
    nf = pl.num_programs(2)
    m = mod_ref[0]

    @pl.when(f == 0)
    def _():
        fx_ref[halo:halo + tile] = _norm_mod(x_ref[0], g_ref[...], m[3:4], m[4:5]).astype(BF16)
        if halo:
            keep_p = jnp.where(i > 0, 1.0, 0.0)
            keep_n = jnp.where(i < nt - 1, 1.0, 0.0)
            fx_ref[0:halo] = (keep_p * _norm_mod(xp_ref[0], g_ref[...], m[3:4], m[4:5])).astype(BF16)
            fx_ref[halo + tile:] = (keep_n * _norm_mod(xn_ref[0], g_ref[...], m[3:4], m[4:5])).astype(BF16)
        acc_ref[...] = jnp.zeros_like(acc_ref)

    fx = fx_ref[...]
    a = jnp.dot(fx, wg_ref[...], preferred_element_type=F32)
    rows = a.shape[0]
    pos = lax.broadcasted_iota(jnp.int32, a.shape, 0)
    if grid_conv:
        col = pos % GRID_W
        a_m1 = jnp.where(col == 0, 0.0, pltpu.roll(a, 1, axis=0))
        a_p1 = jnp.where(col == GRID_W - 1, 0.0, pltpu.roll(a, rows - 1, axis=0))
        conv = cb_ref[...]
        for dr in range(3):
            lo = dr * GRID_W
            conv = conv + (cw_ref[3 * dr + 0:3 * dr + 1] * a_m1[lo:lo + tile]
                           + cw_ref[3 * dr + 1:3 * dr + 2] * a[lo:lo + tile]
                           + cw_ref[3 * dr + 2:3 * dr + 3] * a_p1[lo:lo + tile])
    else:
        a_m1 = jnp.where(pos == 0, 0.0, pltpu.roll(a, 1, axis=0))
        a_p1 = jnp.where(pos == rows - 1, 0.0, pltpu.roll(a, rows - 1, axis=0))
        conv = cb_ref[...] + cw_ref[3:4] * a_m1 + cw_ref[4:5] * a + cw_ref[5:6] * a_p1
    up = jnp.dot(fx_ref[halo:halo + tile], wu_ref[...], preferred_element_type=F32)
    gated = (_silu(conv) * up).astype(BF16)
    acc_ref[...] += jnp.dot(gated, wd_ref[...], preferred_element_type=F32)

    @pl.when(f == nf - 1)
    def _():
        y = x_ref[0] + m[5:6] * acc_ref[...]
        if final_norm:
            y = _rms(y, fg_ref[...])
        o_ref[0] = y


def _conv_ffn(x, mod, gain, final_gain, w_gate, w_up, conv_w, conv_b, w_down, grid_conv, final_norm,
              tile, tf):
    bsz, t, d = x.shape
    dff = w_gate.shape[1]
    tile = min(tile, t)
    halo = GRID_W if grid_conv else 0
    nh = t // GRID_W
    per = tile // GRID_W
    mod_map = (lambda b, i, f: (b, 0, 0)) if mod.shape[0] == bsz else (lambda b, i, f: (0, 0, 0))
    return pl.pallas_call(
        functools.partial(_ffn_kernel, tile=tile, halo=halo, grid_conv=grid_conv, final_norm=final_norm),
        grid=(bsz, t // tile, dff // tf),
        in_specs=[pl.BlockSpec((1, tile, d), lambda b, i, f: (b, i, 0)),
                  pl.BlockSpec((1, GRID_W, d), lambda b, i, f: (b, jnp.maximum(i * per - 1, 0), 0)),
                  pl.BlockSpec((1, GRID_W, d), lambda b, i, f: (b, jnp.minimum((i + 1) * per, nh - 1), 0)),
                  pl.BlockSpec((1, N_MOD, d), mod_map),
                  pl.BlockSpec((1, d), lambda b, i, f: (0, 0)),
                  pl.BlockSpec((1, d), lambda b, i, f: (0, 0)),
                  pl.BlockSpec((d, tf), lambda b, i, f: (0, f)),
                  pl.BlockSpec((d, tf), lambda b, i, f: (0, f)),
                  pl.BlockSpec((9, tf), lambda b, i, f: (0, f)),
                  pl.BlockSpec((1, tf), lambda b, i, f: (0, f)),
                  pl.BlockSpec((tf, d), lambda b, i, f: (f, 0))],
        out_specs=pl.BlockSpec((1, tile, d), lambda b, i, f: (b, i, 0)),
        out_shape=jax.ShapeDtypeStruct((bsz, t, d), F32),
        scratch_shapes=[pltpu.VMEM((tile + 2 * halo, d), BF16), pltpu.VMEM((tile, d), F32)],
        compiler_params=_cparams(("parallel", "parallel", "arbitrary")),
        name="conv_ffn",
    )(x, x, x, mod, gain.reshape(1, d), final_gain.reshape(1, d), w_gate, w_up,
      conv_w.reshape(9, dff), conv_b.reshape(1, dff), w_down)


def _conv4_silu(v, w, b):
    t = v.shape[0]
    pos = lax.broadcasted_iota(jnp.int32, v.shape, 0)
    acc = b + w[1:2] * v
    acc = acc + w[0:1] * jnp.where(pos < 1, 0.0, pltpu.roll(v, 1, axis=0))
    acc = acc + w[2:3] * jnp.where(pos >= t - 1, 0.0, pltpu.roll(v, t - 1, axis=0))
    acc = acc + w[3:4] * jnp.where(pos >= t - 2, 0.0, pltpu.roll(v, t - 2, axis=0))
    return acc


def _ssd_kernel(xr_ref, br_ref, cr_ref, dt_ref, z_ref, cwx_ref, cwb_ref, cwc_ref, cbx_ref, cbb_ref, cbc_ref,
                dtb_ref, alog_ref, dvec_ref, h0_ref, y_ref, ht_ref, xs_ref, bs_ref, cs_ref, s_ref, *, t):
    nblk = 4
    nchunk = t // SSD_CHUNK
    for k in range(nblk):
        xs_ref[k] = _silu(_conv4_silu(xr_ref[0, k], cwx_ref[k], cbx_ref[k]))
    bs_ref[...] = _silu(_conv4_silu(br_ref[0, 0], cwb_ref[0], cbb_ref[0]))
    cs_ref[...] = _silu(_conv4_silu(cr_ref[0, 0], cwc_ref[0], cbc_ref[0]))

    dtb = dtb_ref[0]
    a_neg = -jnp.exp(alog_ref[0])
    li = lax.broadcasted_iota(jnp.int32, (SSD_CHUNK, SSD_CHUNK), 0)
    si = lax.broadcasted_iota(jnp.int32, (SSD_CHUNK, SSD_CHUNK), 1)
    lane = lax.broadcasted_iota(jnp.int32, (1, LANES), 1)
    lo_half = lane < SSD_HEAD_DIM

    for d in (1, 0):
        valid = (li >= si) if d == 0 else (li <= si)
        tri = valid.astype(F32)
        end_row = SSD_CHUNK - 1 if d == 0 else 0
        for k in range(nblk):
            s_ref[k] = h0_ref[0, d, 0, k]

        def chunk(i, carry, d=d, valid=valid, tri=tri, end_row=end_row):
            c = i if d == 0 else nchunk - 1 - i
            r0 = pl.multiple_of(c * SSD_CHUNK, SSD_CHUNK)
            rows = pl.ds(r0, SSD_CHUNK)
            bm = bs_ref[rows, :]
            cm = cs_ref[rows, :]
            dt = _softplus(dt_ref[0, 0, rows, :] + dtb)
            cum = _dot32(tri, dt * a_neg)
            cum_t = cum.T
            dt_t = dt.T
            cb = lax.dot_general(cm.astype(BF16), bm.astype(BF16), (((1,), (1,)), ((), ())),
                                 preferred_element_type=F32)
            bm_t = bm.T.astype(BF16)
            cm16 = cm.astype(BF16)
            for k in range(nblk):
                xk = xs_ref[k, rows, :]
                ms = []
                for e in range(2):
                    idx = d * SSD_HPG + 2 * k + e
                    seg = jnp.where(valid, cum[:, idx:idx + 1] - cum_t[idx:idx + 1, :], -jnp.inf)
                    ms.append((cb * jnp.exp(seg) * dt_t[idx:idx + 1, :]).astype(BF16))
                x_lo = jnp.where(lo_half, xk, 0.0).astype(BF16)
                x_hi = jnp.where(lo_half, 0.0, xk).astype(BF16)
                yk = (jnp.dot(ms[0], x_lo, preferred_element_type=F32)
                      + jnp.dot(ms[1], x_hi, preferred_element_type=F32))
                i0 = d * SSD_HPG + 2 * k
                ecol = jnp.where(lo_half, cum[:, i0:i0 + 1], cum[:, i0 + 1:i0 + 2])
                dcol = jnp.where(lo_half, dt[:, i0:i0 + 1], dt[:, i0 + 1:i0 + 2])
                tot = jnp.where(lo_half, cum[end_row:end_row + 1, i0:i0 + 1],
                                cum[end_row:end_row + 1, i0 + 1:i0 + 2])
                sk = s_ref[k]
                yk = yk + jnp.dot(cm16, sk.astype(BF16), preferred_element_type=F32) * jnp.exp(ecol)
                xw = (xk * dcol * jnp.exp(tot - ecol)).astype(BF16)
                s_ref[k] = sk * jnp.exp(tot) + jnp.dot(bm_t, xw, preferred_element_type=F32)
                if d == 1:
                    y_ref[0, k, rows, :] = yk
                else:
                    ytot = y_ref[0, k, rows, :] + yk + dvec_ref[k] * xk
                    y_ref[0, k, rows, :] = ytot * _silu(z_ref[0, k, rows, :])
            return carry

        lax.fori_loop(0, nchunk, chunk, 0)
        for k in range(nblk):
            ht_ref[0, d, 0, k] = s_ref[k]


def _ssd(p, h0, conv_w, conv_b, dtb, alog, dvec, *, z_blk0, xbc_blk0, dt_blk0):
    bsz, _, t, _ = p.shape
    g4 = lambda off: (lambda b, g: (b, off // 4 + g, 0, 0))
    g1 = lambda off: (lambda b, g: (b, off + g, 0, 0))
    big = pl.BlockSpec((1, 4, t, LANES), g4(xbc_blk0))
    one = lambda off: pl.BlockSpec((1, 1, t, LANES), g1(off))
    st_spec = pl.BlockSpec((1, 2, 1, 4, SSD_STATE, LANES), lambda b, g: (b, 0, g, 0, 0, 0))
    return pl.pallas_call(
        functools.partial(_ssd_kernel, t=t),
        grid=(bsz, SSD_GROUPS),
        in_specs=[big, one(xbc_blk0 + 8), one(xbc_blk0 + 10), one(dt_blk0),
                  pl.BlockSpec((1, 4, t, LANES), g4(z_blk0)),
                  pl.BlockSpec((4, 4, LANES), lambda b, g: (g, 0, 0)),
                  pl.BlockSpec((1, 4, LANES), lambda b, g: (8 + g, 0, 0)),
                  pl.BlockSpec((1, 4, LANES), lambda b, g: (10 + g, 0, 0)),
                  pl.BlockSpec((4, 1, LANES), lambda b, g: (g, 0, 0)),
                  pl.BlockSpec((1, 1, LANES), lambda b, g: (8 + g, 0, 0)),
                  pl.BlockSpec((1, 1, LANES), lambda b, g: (10 + g, 0, 0)),
                  pl.BlockSpec((1, 1, LANES), lambda b, g: (g, 0, 0)),
                  pl.BlockSpec((1, 1, LANES), lambda b, g: (g, 0, 0)),
                  pl.BlockSpec((4, 1, LANES), lambda b, g: (g, 0, 0)),
                  st_spec],
        out_specs=[pl.BlockSpec((1, 4, t, LANES), lambda b, g: (b, g, 0, 0)), st_spec],
        out_shape=[jax.ShapeDtypeStruct((bsz, 8, t, LANES), F32),
                   jax.ShapeDtypeStruct(h0.shape, F32)],
        scratch_shapes=[pltpu.VMEM((4, t, LANES), F32), pltpu.VMEM((t, LANES), F32),
                        pltpu.VMEM((t, LANES), F32), pltpu.VMEM((4, SSD_STATE, LANES), F32)],
        compiler_params=_cparams(("parallel", "parallel")),
        name="ssd",
    )(p, p, p, p, p, conv_w, conv_w, conv_w, conv_b, conv_b, conv_b, dtb, alog, dvec, h0)


LRU_NB = 2
N_SEG = 8


def _seg_rows(t):
    seg = t // N_SEG
    return seg, seg + 8


def _conv4(v, w, b):
    return _conv4_silu(v, w, b)


def _seg_scan(a_ref, b_ref, lead, seg, pitch, reverse):
    def step(i, carry):
        h, acc = carry
        j = (seg - 1 - i) if reverse else i
        idx = lead + (pl.ds(j, N_SEG, stride=pitch), slice(None))
        a = a_ref[idx]
        h = a * h + b_ref[idx]
        acc = acc * a
        a_ref[idx] = acc
        b_ref[idx] = h
        return h, acc

    return lax.fori_loop(0, seg, step, (jnp.zeros((N_SEG, LANES), F32), jnp.ones((N_SEG, LANES), F32)))


def _seg_inputs(h0, end, tot, reverse):
    rows = [None] * N_SEG
    hin = h0
    for s in (range(N_SEG - 1, -1, -1) if reverse else range(N_SEG)):
        rows[s] = hin
        hin = tot[s:s + 1] * hin + end[s:s + 1]
    return rows, hin


def _lru_kernel(u_ref, gy_ref, cw_ref, cb_ref, wa_ref, wi_ref, ba_ref, bi_ref, lam_ref, h0_ref,
                r_ref, ht_ref, uc_ref, a_ref, b_ref, *, t):
    seg, pitch = _seg_rows(t)
    for k in range(LRU_NB):
        uc_ref[k] = _conv4(u_ref[0, k], cw_ref[k], cb_ref[k])
    for d in (1, 0):
        for k in range(LRU_NB):
            decay_rate = -LRU_C * _softplus(-lam_ref[d, k])
            for s in range(N_SEG):
                u = uc_ref[k, s * seg:(s + 1) * seg, :]
                u16 = u.astype(BF16)
                r = _sigmoid(jnp.dot(u16, wa_ref[d, k], preferred_element_type=F32) + ba_ref[d, k])
                i = _sigmoid(jnp.dot(u16, wi_ref[d, k], preferred_element_type=F32) + bi_ref[d, k])
                a = jnp.exp(decay_rate * r)
                a_ref[d, k, s * pitch:s * pitch + seg, :] = a
                b_ref[d, k, s * pitch:s * pitch + seg, :] = jnp.sqrt((1.0 - a) * (1.0 + a)) * (i * u)
    for d in (1, 0):
        for k in range(LRU_NB):
            end, tot = _seg_scan(a_ref, b_ref, (d, k), seg, pitch, reverse=(d == 1))
            rows, hfin = _seg_inputs(h0_ref[0, d, k], end, tot, reverse=(d == 1))
            ht_ref[0, d, k] = hfin
            for s in range(N_SEG):
                src = slice(s * pitch, s * pitch + seg)
                dst = slice(s * seg, (s + 1) * seg)
                h = b_ref[d, k, src, :] + a_ref[d, k, src, :] * rows[s]
                if d == 1:
                    r_ref[0, k, dst, :] = h
                else:
                    r_ref[0, k, dst, :] = (r_ref[0, k, dst, :] + h) * _gelu_tanh(gy_ref[0, k, dst, :])


def _lru(p, h0, conv_w, conv_b, wa, wi, ba, bi, lam, *, gy_blk0, u_blk0):
    bsz, _, t, _ = p.shape
    nb = LRU_NB
    seg, pitch = _seg_rows(t)
    blk = lambda off: pl.BlockSpec((1, nb, t, LANES), lambda b, g: (b, off // nb + g, 0, 0))
    par = lambda shape: pl.BlockSpec(shape, lambda b, g: (0, g) + (0,) * (len(shape) - 2))
    st_spec = pl.BlockSpec((1, 2, nb, 1, LANES), lambda b, g: (b, 0, g, 0, 0))
    return pl.pallas_call(
        functools.partial(_lru_kernel, t=t),
        grid=(bsz, LRU_BLOCKS // nb),
        in_specs=[blk(u_blk0), blk(gy_blk0),
                  pl.BlockSpec((nb, 4, LANES), lambda b, g: (g, 0, 0)),
                  pl.BlockSpec((nb, 1, LANES), lambda b, g: (g, 0, 0)),
                  par((2, nb, LANES, LANES)), par((2, nb, LANES, LANES)),
                  par((2, nb, 1, LANES)), par((2, nb, 1, LANES)), par((2, nb, 1, LANES)),
                  st_spec],
        out_specs=[pl.BlockSpec((1, nb, t, LANES), lambda b, g: (b, g, 0, 0)), st_spec],
        out_shape=[jax.ShapeDtypeStruct((bsz, LRU_BLOCKS, t, LANES), F32),
                   jax.ShapeDtypeStruct(h0.shape, F32)],
        scratch_shapes=[pltpu.VMEM((nb, t, LANES), F32),
                        pltpu.VMEM((2, nb, N_SEG * pitch, LANES), F32),
                        pltpu.VMEM((2, nb, N_SEG * pitch, LANES), F32)],
        compiler_params=_cparams(("parallel", "parallel")),
        name="rglru",
    )(p, p, conv_w, conv_b, wa, wi, ba, bi, lam, h0)


EV_Z0, EV_XBC0, EV_DT0, EV_GY0, EV_U0, EV_NBLK = 0, 8, 20, 22, 30, 38


def _blocks(v, n):
    return v.reshape(n, 1, LANES)


def _per_group_heads(v):
    v = v.reshape(2, SSD_GROUPS, SSD_HPG).transpose(1, 0, 2).reshape(SSD_GROUPS, 2 * SSD_HPG)
    return jnp.pad(v, ((0, 0), (0, LANES - 2 * SSD_HPG))).reshape(SSD_GROUPS, 1, LANES)


def _even_weights(w_in, conv_w, conv_b, dt_bias, a_log, ssd_d, lru_conv_w, lru_conv_b, w_a, b_a, w_i, b_i, lam):
    d = w_in.shape[0]
    z, xbc, dt, gy, u = jnp.split(w_in, (1024, 2560, 2592, 3616), axis=1)
    dt = dt.reshape(d, 2, SSD_GROUPS, SSD_HPG).transpose(0, 2, 1, 3).reshape(d, SSD_GROUPS, 2 * SSD_HPG)
    dt = jnp.pad(dt, ((0, 0), (0, 0), (0, LANES - 2 * SSD_HPG))).reshape(d, SSD_GROUPS * LANES)
    return dict(
        w_in=jnp.concatenate([z, xbc, dt, gy, u], axis=1).astype(BF16),
        conv_w=conv_w.reshape(4, 12, LANES).transpose(1, 0, 2), conv_b=_blocks(conv_b, 12),
        dtb=_per_group_heads(dt_bias), alog=_per_group_heads(a_log),
        dvec=_blocks(jnp.repeat(ssd_d, SSD_HEAD_DIM), 8),
        lru_conv_w=lru_conv_w.reshape(4, LRU_BLOCKS, LANES).transpose(1, 0, 2),
        lru_conv_b=_blocks(lru_conv_b, LRU_BLOCKS),
        w_a=w_a.astype(BF16), w_i=w_i.astype(BF16),
        b_a=b_a.reshape(2, LRU_BLOCKS, 1, LANES), b_i=b_i.reshape(2, LRU_BLOCKS, 1, LANES),
        lam=lam.reshape(2, LRU_BLOCKS, 1, LANES))


def _even_scans(p, states, w):
    ssd_h, lru_h = states
    y, ssd_h = _ssd(p, ssd_h, w["conv_w"], w["conv_b"], w["dtb"], w["alog"], w["dvec"],
                    z_blk0=EV_Z0, xbc_blk0=EV_XBC0, dt_blk0=EV_DT0)
    r, lru_h = _lru(p, lru_h, w["lru_conv_w"], w["lru_conv_b"], w["w_a"], w["w_i"], w["b_a"], w["b_i"],
                    w["lam"], gy_blk0=EV_GY0, u_blk0=EV_U0)
    return y, r, (ssd_h, lru_h)


def _even_zero_states(bsz):
    return (jnp.zeros((bsz, 2, SSD_GROUPS, 4, SSD_STATE, LANES), F32),
            jnp.zeros((bsz, 2, LRU_BLOCKS, 1, LANES), F32))


HG_CHUNK = 128
SUBLANES = 8


def _group_boundary(cum, c, reverse):
    m = c // 2
    off = m if reverse else m - 1
    n = cum.shape[0]
    if c >= 2 * SUBLANES:
        r = cum.reshape(n // c, c, LANES)
        return jnp.broadcast_to(r[:, off:off + 1, :], r.shape).reshape(n, LANES)
    r = cum.reshape(n // SUBLANES, SUBLANES, LANES)
    sub = lax.broadcasted_iota(jnp.int32, r.shape, 1)
    p = None
    for g0 in range(0, SUBLANES, c):
        cand = jnp.broadcast_to(r[:, g0 + off:g0 + off + 1, :], r.shape)
        p = cand if p is None else jnp.where(sub >= g0, cand, p)
    return p.reshape(n, LANES)


def _hgrn_kernel(q_ref, ff_ref, fb_ref, v_ref, g_ref, lb_ref, ng_ref, h0_ref, o_ref, ht_ref, s_ref, *, t):
    nchunk = t // HG_CHUNK
    lb = lb_ref[0]
    li = lax.broadcasted_iota(jnp.int32, (HG_CHUNK, HG_CHUNK), 0)
    si = lax.broadcasted_iota(jnp.int32, (HG_CHUNK, HG_CHUNK), 1)
    for d in (1, 0):
        reverse = d == 1
        f_ref = fb_ref if reverse else ff_ref
        tri = ((li <= si) if reverse else (li >= si)).astype(F32)
        end_row = 0 if reverse else HG_CHUNK - 1
        s_ref[...] = h0_ref[0, d, 0]

        def chunk(i, carry, reverse=reverse, f_ref=f_ref, tri=tri, end_row=end_row):
            c = (nchunk - 1 - i) if reverse else i
            rows = pl.ds(pl.multiple_of(c * HG_CHUNK, HG_CHUNK), HG_CHUNK)
            fx = f_ref[0, 0, rows, :]
            log_f = jnp.log(lb + (1.0 - lb) * _sigmoid(fx))
            kk = (1.0 - lb) * _sigmoid(-fx)
            qq = _silu(q_ref[0, 0, rows, :])
            vv = v_ref[0, 0, rows, :]
            cum = _dot32(tri, log_f)
            att = None
            size = HG_CHUNK
            while size >= 2:
                half = size // 2
                fac = jnp.exp(-jnp.abs(cum - _group_boundary(cum, size, reverse)))
                later = ((li % size) < half) if reverse else ((li % size) >= half)
                q_l = jnp.where(later, qq * fac, 0.0).astype(BF16)
                k_l = jnp.where(later, 0.0, kk * fac).astype(BF16)
                a_l = lax.dot_general(q_l, k_l, (((1,), (1,)), ((), ())), preferred_element_type=F32)
                if size < HG_CHUNK:
                    a_l = jnp.where((li // size) == (si // size), a_l, 0.0)
                att = a_l if att is None else att + a_l
                size = half
            diag = jnp.sum(qq * kk, axis=-1, keepdims=True)
            o = jnp.dot(att.astype(BF16), vv.astype(BF16), preferred_element_type=F32) + diag * vv
            sv = s_ref[...]
            o = o + jnp.dot((qq * jnp.exp(cum)).astype(BF16), sv.astype(BF16), preferred_element_type=F32)
            cum_end = cum[end_row:end_row + 1, :]
            kw = (kk * jnp.exp(cum_end - cum)).T.astype(BF16)
            keep = jnp.broadcast_to(jnp.exp(cum_end), (HG_CHUNK, LANES)).T
            s_ref[...] = sv * keep + jnp.dot(kw, vv.astype(BF16), preferred_element_type=F32)
            if reverse:
                o_ref[0, 0, rows, :] = o
            else:
                o = o_ref[0, 0, rows, :] + o
                o_ref[0, 0, rows, :] = _rms(o, ng_ref[0]) * _silu(g_ref[0, 0, rows, :])
            return carry

        lax.fori_loop(0, nchunk, chunk, 0)
        ht_ref[0, d, 0] = s_ref[...]


def _hgrn(p, h0, lb, norm_g):
    bsz, _, t, _ = p.shape
    blk = lambda off: pl.BlockSpec((1, 1, t, LANES), lambda b, h: (b, off + h, 0, 0))
    par = pl.BlockSpec((1, 1, LANES), lambda b, h: (h, 0, 0))
    st_spec = pl.BlockSpec((1, 2, 1, LANES, LANES), lambda b, h: (b, 0, h, 0, 0))
    return pl.pallas_call(
        functools.partial(_hgrn_kernel, t=t),
        grid=(bsz, HG_HEADS),
        in_specs=[blk(0), blk(HG_HEADS), blk(2 * HG_HEADS), blk(3 * HG_HEADS), blk(4 * HG_HEADS),
                  par, par, st_spec],
        out_specs=[pl.BlockSpec((1, 1, t, LANES), lambda b, h: (b, h, 0, 0)), st_spec],
        out_shape=[jax.ShapeDtypeStruct((bsz, HG_HEADS, t, LANES), F32),
                   jax.ShapeDtypeStruct(h0.shape, F32)],
        scratch_shapes=[pltpu.VMEM((LANES, LANES), F32)],
        compiler_params=_cparams(("parallel", "parallel")),
        name="hgrn2",
    )(p, p, p, p, p, lb.reshape(HG_HEADS, 1, LANES), norm_g.reshape(HG_HEADS, 1, LANES), h0)


S5_W = S5_GROUPS * S5_GROUP_CH
S5_NSTATE = S5_GROUPS * S5_STATE
S5_NB = S5_NSTATE // LANES
S5_PAR = 4


def _s5_kernel(u_ref, are_ref, aim_ref, bd_ref, cd_ref, dvec_ref, gw_ref, gb_ref, h0_ref,
               y_ref, ht_ref, xr_ref, xi_ref, pw_ref, acc_ref, *, t):
    seg, pitch = _seg_rows(t)
    u = jnp.concatenate([u_ref[0, 0], u_ref[0, 1]], axis=-1)
    u16 = u.astype(BF16)
    acc_ref[...] = dvec_ref[...] * u
    for d in (1, 0):
        reverse = d == 1
        for j0 in range(0, S5_NB, S5_PAR):
            cols = slice(j0 * LANES, (j0 + S5_PAR) * LANES)
            for s in range(N_SEG):
                us = u16[s * seg:(s + 1) * seg]
                xr = jnp.dot(us, bd_ref[d, 0, :, cols], preferred_element_type=F32)
                xi = jnp.dot(us, bd_ref[d, 1, :, cols], preferred_element_type=F32)
                for k in range(S5_PAR):
                    xr_ref[k, s * pitch:s * pitch + seg, :] = xr[:, k * LANES:(k + 1) * LANES]
                    xi_ref[k, s * pitch:s * pitch + seg, :] = xi[:, k * LANES:(k + 1) * LANES]
            ar = [jnp.broadcast_to(are_ref[d, j0 + k], (N_SEG, LANES)) for k in range(S5_PAR)]
            ai = [jnp.broadcast_to(aim_ref[d, j0 + k], (N_SEG, LANES)) for k in range(S5_PAR)]

            def step(i, carry, reverse=reverse, ar=ar, ai=ai):
                j = (seg - 1 - i) if reverse else i
                out = []
                for k in range(S5_PAR):
                    hr, hi, pr, pi = carry[k]
                    idx = (k, pl.ds(j, N_SEG, stride=pitch), slice(None))
                    nhr = ar[k] * hr - ai[k] * hi + xr_ref[idx]
                    nhi = ar[k] * hi + ai[k] * hr + xi_ref[idx]
                    xr_ref[idx] = nhr
                    xi_ref[idx] = nhi
                    npr = ar[k] * pr - ai[k] * pi
                    npi = ar[k] * pi + ai[k] * pr
                    pw_ref[0, k, pl.ds(j, 1), :] = npr[0:1]
                    pw_ref[1, k, pl.ds(j, 1), :] = npi[0:1]
                    out.append((nhr, nhi, npr, npi))
                return tuple(out)

            zero = jnp.zeros((N_SEG, LANES), F32)
            one = jnp.ones((N_SEG, LANES), F32)
            fin = lax.fori_loop(0, seg, step, tuple((zero, zero, one, zero) for _ in range(S5_PAR)))
            ins = []
            for k in range(S5_PAR):
                end_r, end_i, tot_r, tot_i = fin[k]
                hin_r, hin_i = h0_ref[0, d, 0, j0 + k], h0_ref[0, d, 1, j0 + k]
                rows = [None] * N_SEG
                for s in (range(N_SEG - 1, -1, -1) if reverse else range(N_SEG)):
                    rows[s] = (hin_r, hin_i)
                    hin_r, hin_i = (tot_r[0:1] * hin_r - tot_i[0:1] * hin_i + end_r[s:s + 1],
                                    tot_r[0:1] * hin_i + tot_i[0:1] * hin_r + end_i[s:s + 1])
                ht_ref[0, d, 0, j0 + k] = hin_r
                ht_ref[0, d, 1, j0 + k] = hin_i
                ins.append(rows)
            for s in range(N_SEG):
                src = slice(s * pitch, s * pitch + seg)
                hr_parts, hi_parts = [], []
                for k in range(S5_PAR):
                    in_r, in_i = ins[k][s]
                    pr, pi = pw_ref[0, k], pw_ref[1, k]
                    hr_parts.append((xr_ref[k, src, :] + pr * in_r - pi * in_i).astype(BF16))
                    hi_parts.append((xi_ref[k, src, :] + pr * in_i + pi * in_r).astype(BF16))
                hr = jnp.concatenate(hr_parts, axis=-1)
                hi = jnp.concatenate(hi_parts, axis=-1)
                acc_ref[s * seg:(s + 1) * seg, :] += (
                    jnp.dot(hr, cd_ref[d, 0, cols, :], preferred_element_type=F32)
                    + jnp.dot(hi, cd_ref[d, 1, cols, :], preferred_element_type=F32))
    y = _gelu_tanh(acc_ref[...])
    y = y * _sigmoid(jnp.dot(y.astype(BF16), gw_ref[...], preferred_element_type=F32) + gb_ref[...])
    y_ref[0, 0] = y[:, :LANES]
    y_ref[0, 1] = y[:, LANES:]


def _s5(p, h0, a_re, a_im, bd, cd, dvec, glu_w, glu_b, *, u_blk0):
    bsz, _, t, _ = p.shape
    seg, pitch = _seg_rows(t)
    full = lambda a: pl.BlockSpec(a.shape, lambda b: (0,) * a.ndim)
    st_spec = pl.BlockSpec((1, 2, 2, S5_NB, 1, LANES), lambda b: (b, 0, 0, 0, 0, 0))
    return pl.pallas_call(
        functools.partial(_s5_kernel, t=t),
        grid=(bsz,),
        in_specs=[pl.BlockSpec((1, 2, t, LANES), lambda b: (b, u_blk0 // 2, 0, 0)),
                  full(a_re), full(a_im), full(bd), full(cd), full(dvec), full(glu_w), full(glu_b), st_spec],
        out_specs=[pl.BlockSpec((1, 2, t, LANES), lambda b: (b, 0, 0, 0)), st_spec],
        out_shape=[jax.ShapeDtypeStruct((bsz, 2, t, LANES), F32), jax.ShapeDtypeStruct(h0.shape, F32)],
        scratch_shapes=[pltpu.VMEM((S5_PAR, N_SEG * pitch, LANES), F32),
                        pltpu.VMEM((S5_PAR, N_SEG * pitch, LANES), F32),
                        pltpu.VMEM((2, S5_PAR, seg, LANES), F32),
                        pltpu.VMEM((t, S5_W), F32)],
        compiler_params=_cparams(("parallel",)),
        name="s5",
    )(p, a_re, a_im, bd, cd, dvec, glu_w, glu_b, h0)


def _s5_params(lam_re, lam_im, log_step, b_re, b_im, c_re, c_im):
    step = jnp.exp(log_step)[..., None]
    mag = jnp.exp(lam_re * step)
    ar, ai = mag * jnp.cos(lam_im * step), mag * jnp.sin(lam_im * step)
    den = lam_re * lam_re + lam_im * lam_im
    zr = ((ar - 1) * lam_re + ai * lam_im) / den
    zi = (ai * lam_re - (ar - 1) * lam_im) / den
    bbr = zr[..., None] * b_re - zi[..., None] * b_im
    bbi = zr[..., None] * b_im + zi[..., None] * b_re
    eye = jnp.eye(S5_GROUPS, dtype=F32)

    def in_mat(m):
        return jnp.einsum("dgpk,gh->dgkhp", m, eye).reshape(2, S5_W, S5_NSTATE)

    def out_mat(m):
        return jnp.einsum("dgkp,gh->dgphk", m, eye).reshape(2, S5_NSTATE, S5_W)

    bd = jnp.stack([in_mat(bbr), in_mat(bbi)], axis=1).astype(BF16)
    cd = jnp.stack([out_mat(c_re), -out_mat(c_im)], axis=1).astype(BF16)
    shape = (2, S5_NB, 1, LANES)
    return ar.reshape(shape), ai.reshape(shape), bd, cd


OD_U0 = 5 * HG_HEADS


def _odd_weights(w_in, lower_bound, hg_norm_g, lam_re, lam_im, log_step, b_re, b_im, c_re, c_im, s5_d,
                 glu_w, glu_b):
    a_re, a_im, bd, cd = _s5_params(lam_re, lam_im, log_step, b_re, b_im, c_re, c_im)
    return dict(w_in=w_in.astype(BF16), lb=lower_bound, norm_g=hg_norm_g, a_re=a_re, a_im=a_im, bd=bd, cd=cd,
                dvec=s5_d.reshape(1, S5_W), glu_w=glu_w.astype(BF16), glu_b=glu_b.reshape(1, S5_W))


def _odd_scans(p, states, w):
    hg_h, s5_h = states
    o, hg_h = _hgrn(p, hg_h, w["lb"], w["norm_g"])
    y, s5_h = _s5(p, s5_h, w["a_re"], w["a_im"], w["bd"], w["cd"], w["dvec"], w["glu_w"], w["glu_b"],
                  u_blk0=OD_U0)
    return o, y, (hg_h, s5_h)


def _odd_zero_states(bsz):
    return (jnp.zeros((bsz, 2, HG_HEADS, LANES, LANES), F32),
            jnp.zeros((bsz, 2, 2, S5_NB, 1, LANES), F32))


PROJ_TILE = 256
MIX_TILE = 512
FFN_TILE = 1024
FFN_COLS = 256


def kernel(x, c, ctx, c_ctx, w_mod, b_mod, norm_mix_g, norm_ffn_g, final_norm_g,
           ev_w_in, ev_w_out, ssd_conv_w, ssd_conv_b, ssd_dt_bias, ssd_a_log, ssd_d, ssd_norm_g,
           lru_conv_w, lru_conv_b, lru_w_a, lru_b_a, lru_w_i, lru_b_i, lru_lam,
           od_w_in, od_w_out, hg_lb_logits, hg_norm_g,
           s5_lam_re, s5_lam_im, s5_log_step, s5_b_re, s5_b_im, s5_c_re, s5_c_im, s5_d,
           s5_glu_w, s5_glu_b,
           ffn_w_gate, ffn_w_up, ffn_conv_w, ffn_conv_b, ffn_w_down):
    bsz, _, d = x.shape
    depth = w_mod.shape[0]
    prob = jax.nn.softmax(hg_lb_logits.astype(F32), axis=0)
    lower_bounds = (jnp.cumsum(prob, axis=0) - prob[0]).astype(hg_lb_logits.dtype)

    pad = (-(bsz + 1)) % SUBLANES
    cond = jnp.concatenate([c, c_ctx[None], jnp.zeros((pad, d), c.dtype)], axis=0)
    mods = _modulation(cond, w_mod, b_mod).transpose(0, 2, 1, 3)

    for layer in range(depth):
        last = layer == depth - 1
        j = layer // 2
        mod_x = mods[layer, :bsz]
        mod_c = mods[layer, bsz:bsz + 1]
        if layer % 2 == 0:
            w = _even_weights(ev_w_in[j], ssd_conv_w[j], ssd_conv_b[j], ssd_dt_bias[j], ssd_a_log[j], ssd_d[j],
                              lru_conv_w[j], lru_conv_b[j], lru_w_a[j], lru_b_a[j], lru_w_i[j], lru_b_i[j],
                              lru_lam[j])
            scans, zero_states = _even_scans, _even_zero_states
            w_out, norm_gain, norm_first = ev_w_out[j].astype(BF16), ssd_norm_g[j], True
        else:
            w = _odd_weights(od_w_in[j], lower_bounds[layer], hg_norm_g[j], s5_lam_re[j], s5_lam_im[j],
                             s5_log_step[j], s5_b_re[j], s5_b_im[j], s5_c_re[j], s5_c_im[j], s5_d[j],
                             s5_glu_w[j], s5_glu_b[j])
            scans, zero_states = _odd_scans, _odd_zero_states
            w_out, norm_gain, norm_first = od_w_out[j].astype(BF16), hg_norm_g[j], False
        ffn_w = (ffn_w_gate[layer].astype(BF16), ffn_w_up[layer].astype(BF16), ffn_conv_w[layer],
                 ffn_conv_b[layer], ffn_w_down[layer].astype(BF16))

        p_c = _project(ctx, mod_c, norm_mix_g[layer], w["w_in"], PROJ_TILE)
        a_c, b_c, states = scans(p_c, zero_states(bsz), w)
        p_x = _project(x, mod_x, norm_mix_g[layer], w["w_in"], PROJ_TILE)
        a_x, b_x, _ = scans(p_x, states, w)
        x = _mix_out(x, a_x, b_x, mod_x, norm_gain, w_out, norm_first, MIX_TILE)
        x = _conv_ffn(x, mod_x, norm_ffn_g[layer], final_norm_g, *ffn_w, grid_conv=True, final_norm=last,
                      tile=FFN_TILE, tf=FFN_COLS)
        if not last:
            ctx = _mix_out(ctx, a_c, b_c, mod_c, norm_gain, w_out, norm_first, MIX_TILE)
            ctx = _conv_ffn(ctx, mod_c, norm_ffn_g[layer], final_norm_g, *ffn_w, grid_conv=False,
                            final_norm=False, tile=FFN_TILE, tf=FFN_COLS)
    return x
```

```python
import functools
import math

import jax
import jax.numpy as jnp
import numpy as np
from jax import lax
from jax.experimental import pallas as pl
from jax.experimental.pallas import tpu as pltpu

LANES = 128
RMS_EPS = 1e-6
N_MOD = 6
GRID_W = 64
SSD_HEAD_DIM = 64
SSD_HEADS = 16
SSD_GROUPS = 2
SSD_HPG = SSD_HEADS // SSD_GROUPS
SSD_STATE = 128
SSD_CHUNK = 128
LRU_BLOCKS = 8
LRU_C = 8.0
HG_HEADS = 6
S5_GROUPS = 16
S5_GROUP_CH = 16
S5_STATE = 64
VMEM_LIMIT = 56 * 1024 * 1024

BF16 = jnp.bfloat16
F32 = jnp.float32
HIGHEST = lax.Precision.HIGHEST


def _cparams(sem):
    return pltpu.CompilerParams(dimension_semantics=sem, vmem_limit_bytes=VMEM_LIMIT)


def _dot(a, b):
    return jnp.dot(a.astype(BF16), b.astype(BF16), preferred_element_type=F32)


def _dot32(a, b):
    return jnp.dot(a, b, preferred_element_type=F32, precision=HIGHEST)


def _sigmoid(v):
    return 0.5 * jnp.tanh(0.5 * v) + 0.5


def _silu(v):
    return v * _sigmoid(v)


def _gelu_tanh(v):
    return 0.5 * v * (1.0 + jnp.tanh(math.sqrt(2.0 / math.pi) * (v + 0.044715 * (v * v * v))))


def _softplus(v):
    return jnp.maximum(v, 0.0) + jnp.log(1.0 + jnp.exp(-jnp.abs(v)))


def _rms(v, g):
    return v * lax.rsqrt(jnp.mean(v * v, axis=-1, keepdims=True) + RMS_EPS) * g


def _norm_mod(xv, g, shift, scale):
    return _rms(xv, g) * (1.0 + scale) + shift


def _mod_kernel(s_ref, w_ref, b_ref, o_ref):
    o_ref[0, 0] = _dot(_silu(s_ref[...]), w_ref[0]) + b_ref[0, 0]


def _modulation(s, w_mod, b_mod):
    depth, d, _ = w_mod.shape
    rows = s.shape[0]
    return pl.pallas_call(
        _mod_kernel,
        grid=(depth, N_MOD),
        in_specs=[pl.BlockSpec((rows, d), lambda l, j: (0, 0)),
                  pl.BlockSpec((1, d, d), lambda l, j: (l, 0, j)),
                  pl.BlockSpec((1, 1, 1, d), lambda l, j: (l, j, 0, 0))],
        out_specs=pl.BlockSpec((1, 1, rows, d), lambda l, j: (l, j, 0, 0)),
        out_shape=jax.ShapeDtypeStruct((depth, N_MOD, rows, d), F32),
        compiler_params=_cparams(("arbitrary", "arbitrary")),
        name="modulation",
    )(s, w_mod, b_mod.reshape(depth, N_MOD, 1, d))


def _proj_kernel(x_ref, mod_ref, g_ref, w_ref, o_ref, *, nblk):
    m = mod_ref[0]
    h = _norm_mod(x_ref[0], g_ref[...], m[0:1], m[1:2]).astype(BF16)
    group = 4
    for b0 in range(0, nblk, group):
        nb = min(group, nblk - b0)
        r = jnp.dot(h, w_ref[:, b0 * LANES:(b0 + nb) * LANES], preferred_element_type=F32)
        for k in range(nb):
            o_ref[0, b0 + k] = r[:, k * LANES:(k + 1) * LANES]


def _project(x, mod, gain, w, tile):
    bsz, t, d = x.shape
    nblk = w.shape[1] // LANES
    tile = min(tile, t)
    mod_map = (lambda b, i: (b, 0, 0)) if mod.shape[0] == bsz else (lambda b, i: (0, 0, 0))
    return pl.pallas_call(
        functools.partial(_proj_kernel, nblk=nblk),
        grid=(bsz, t // tile),
        in_specs=[pl.BlockSpec((1, tile, d), lambda b, i: (b, i, 0)),
                  pl.BlockSpec((1, N_MOD, d), mod_map),
                  pl.BlockSpec((1, d), lambda b, i: (0, 0)),
                  pl.BlockSpec((d, nblk * LANES), lambda b, i: (0, 0), pipeline_mode=pl.Buffered(1))],
        out_specs=pl.BlockSpec((1, nblk, tile, LANES), lambda b, i: (b, 0, i, 0)),
        out_shape=jax.ShapeDtypeStruct((bsz, nblk, t, LANES), F32),
        compiler_params=_cparams(("parallel", "parallel")),
        name="project",
    )(x, mod, gain.reshape(1, d), w)


def _mix_out_kernel(x_ref, ma_ref, mb_ref, mod_ref, ng_ref, w_ref, o_ref, *, norm_first):
    m = mod_ref[0]
    pa = [ma_ref[0, k] for k in range(ma_ref.shape[1])]
    pb = [mb_ref[0, k].astype(BF16) for k in range(mb_ref.shape[1])]
    if norm_first:
        pa = [_rms(jnp.concatenate(pa, axis=-1), ng_ref[...]).astype(BF16)]
    else:
        pa = [p.astype(BF16) for p in pa]
    v = jnp.concatenate(pa + pb, axis=-1)
    o_ref[0] = x_ref[0] + m[2:3] * jnp.dot(v, w_ref[...], preferred_element_type=F32)


def _mix_out(x, mix_a, mix_b, mod, norm_gain, w, norm_first, tile):
    bsz, t, d = x.shape
    tile = min(tile, t)
    mod_map = (lambda b, i: (b, 0, 0)) if mod.shape[0] == bsz else (lambda b, i: (0, 0, 0))
    ng = norm_gain.reshape(1, -1)
    mix_spec = lambda a: pl.BlockSpec((1, a.shape[1], tile, LANES), lambda b, i: (b, 0, i, 0))
    return pl.pallas_call(
        functools.partial(_mix_out_kernel, norm_first=norm_first),
        grid=(bsz, t // tile),
        in_specs=[pl.BlockSpec((1, tile, d), lambda b, i: (b, i, 0)),
                  mix_spec(mix_a), mix_spec(mix_b),
                  pl.BlockSpec((1, N_MOD, d), mod_map),
                  pl.BlockSpec(ng.shape, lambda b, i: (0, 0)),
                  pl.BlockSpec(w.shape, lambda b, i: (0, 0), pipeline_mode=pl.Buffered(1))],
        out_specs=pl.BlockSpec((1, tile, d), lambda b, i: (b, i, 0)),
        out_shape=jax.ShapeDtypeStruct((bsz, t, d), F32),
        compiler_params=_cparams(("parallel", "parallel")),
        name="mix_out",
    )(x, mix_a, mix_b, mod, ng, w)


def _ffn_kernel(x_ref, xp_ref, xn_ref, mod_ref, g_ref, fg_ref, wg_ref, wu_ref, cw_ref, cb_ref, wd_ref,
                o_ref, fx_ref, gt_ref, *, tile, halo, tf, grid_conv, final_norm):
    i = pl.program_id(1)
    nt = pl.num_programs(1)
    nf = wg_ref.shape[0]
    m = mod_ref[0]
    fx_ref[halo:halo + tile] = _norm_mod(x_ref[0], g_ref[...], m[3:4], m[4:5]).astype(BF16)
    if halo:
        keep_p = jnp.where(i > 0, 1.0, 0.0)
        keep_n = jnp.where(i < nt - 1, 1.0, 0.0)
        fx_ref[0:halo] = (keep_p * _norm_mod(xp_ref[0], g_ref[...], m[3:4], m[4:5])).astype(BF16)
        fx_ref[halo + tile:] = (keep_n * _norm_mod(xn_ref[0], g_ref[...], m[3:4], m[4:5])).astype(BF16)

    def columns(f, carry):
        a = jnp.dot(fx_ref[...], wg_ref[f], preferred_element_type=F32)
        up = jnp.dot(fx_ref[halo:halo + tile], wu_ref[f], preferred_element_type=F32)
        cw = cw_ref[f]
        rows = a.shape[0]
        pos = lax.broadcasted_iota(jnp.int32, a.shape, 0)
        if grid_conv:
            col = pos % GRID_W
            a_m1 = jnp.where(col == 0, 0.0, pltpu.roll(a, 1, axis=0))
            a_p1 = jnp.where(col == GRID_W - 1, 0.0, pltpu.roll(a, rows - 1, axis=0))
            conv = cb_ref[f]
            for dr in range(3):
                lo = dr * GRID_W
                conv = conv + (cw[3 * dr + 0:3 * dr + 1] * a_m1[lo:lo + tile]
                               + cw[3 * dr + 1:3 * dr + 2] * a[lo:lo + tile]
                               + cw[3 * dr + 2:3 * dr + 3] * a_p1[lo:lo + tile])
        else:
            a_m1 = jnp.where(pos == 0, 0.0, pltpu.roll(a, 1, axis=0))
            a_p1 = jnp.where(pos == rows - 1, 0.0, pltpu.roll(a, rows - 1, axis=0))
            conv = cb_ref[f] + cw[3:4] * a_m1 + cw[4:5] * a + cw[5:6] * a_p1
        gt_ref[:, pl.ds(pl.multiple_of(f * tf, tf), tf)] = (_silu(conv) * up).astype(BF16)
        return carry

    lax.fori_loop(0, nf, columns, 0)
    y = x_ref[0] + m[5:6] * jnp.dot(gt_ref[...], wd_ref[...], preferred_element_type=F32)
    if final_norm:
        y = _rms(y, fg_ref[...])
    o_ref[0] = y


def _conv_ffn(x, mod, gain, final_gain, w_gate, w_up, conv_w, conv_b, w_down, grid_conv, final_norm,
              tile, tf):
    bsz, t, d = x.shape
    nf = w_gate.shape[0]
    dff = nf * tf
    tile = min(tile, t)
    halo = GRID_W if grid_conv else 0
    nh = t // GRID_W
    per = tile // GRID_W
    mod_map = (lambda b, i: (b, 0, 0)) if mod.shape[0] == bsz else (lambda b, i: (0, 0, 0))
    resident = lambda a: pl.BlockSpec(a.shape, lambda b, i: (0,) * a.ndim, pipeline_mode=pl.Buffered(1))
    return pl.pallas_call(
        functools.partial(_ffn_kernel, tile=tile, halo=halo, tf=tf, grid_conv=grid_conv, final_norm=final_norm),
        grid=(bsz, t // tile),
        in_specs=[pl.BlockSpec((1, tile, d), lambda b, i: (b, i, 0)),
                  pl.BlockSpec((1, GRID_W, d), lambda b, i: (b, jnp.maximum(i * per - 1, 0), 0)),
                  pl.BlockSpec((1, GRID_W, d), lambda b, i: (b, jnp.minimum((i + 1) * per, nh - 1), 0)),
                  pl.BlockSpec((1, N_MOD, d), mod_map),
                  pl.BlockSpec((1, d), lambda b, i: (0, 0)),
                  pl.BlockSpec((1, d), lambda b, i: (0, 0)),
                  resident(w_gate), resident(w_up), resident(conv_w), resident(conv_b), resident(w_down)],
        out_specs=pl.BlockSpec((1, tile, d), lambda b, i: (b, i, 0)),
        out_shape=jax.ShapeDtypeStruct((bsz, t, d), F32),
        scratch_shapes=[pltpu.VMEM((tile + 2 * halo, d), BF16), pltpu.VMEM((tile, dff), BF16)],
        compiler_params=_cparams(("parallel", "parallel")),
        name="conv_ffn",
    )(x, x, x, mod, gain.reshape(1, d), final_gain.reshape(1, d), w_gate, w_up, conv_w, conv_b, w_down)


def _ffn_weights(w_gate, w_up, conv_w, conv_b, w_down, tf):
    d, dff = w_gate.shape
    nf = dff // tf
    tiles = lambda w: w.reshape(w.shape[0], nf, tf).transpose(1, 0, 2)
    return (tiles(w_gate).astype(BF16), tiles(w_up).astype(BF16), tiles(conv_w.reshape(9, dff)),
            tiles(conv_b.reshape(1, dff)), w_down.astype(BF16))


def _conv4_silu(v, w, b):
    t = v.shape[0]
    pos = lax.broadcasted_iota(jnp.int32, v.shape, 0)
    acc = b + w[1:2] * v
    acc = acc + w[0:1] * jnp.where(pos < 1, 0.0, pltpu.roll(v, 1, axis=0))
    acc = acc + w[2:3] * jnp.where(pos >= t - 1, 0.0, pltpu.roll(v, t - 1, axis=0))
    acc = acc + w[3:4] * jnp.where(pos >= t - 2, 0.0, pltpu.roll(v, t - 2, axis=0))
    return acc


def _ssd_kernel(xr_ref, br_ref, cr_ref, dt_ref, z_ref, cwx_ref, cwb_ref, cwc_ref, cbx_ref, cbb_ref, cbc_ref,
                dtb_ref, alog_ref, dvec_ref, h0_ref, y_ref, ht_ref, xs_ref, bs_ref, cs_ref, s_ref, *, t):
    nblk = 4
    nchunk = t // SSD_CHUNK
    for k in range(nblk):
        xs_ref[k] = _silu(_conv4_silu(xr_ref[0, k], cwx_ref[k], cbx_ref[k]))
    bs_ref[...] = _silu(_conv4_silu(br_ref[0, 0], cwb_ref[0], cbb_ref[0]))
    cs_ref[...] = _silu(_conv4_silu(cr_ref[0, 0], cwc_ref[0], cbc_ref[0]))

    dtb = dtb_ref[0]
    a_neg = -jnp.exp(alog_ref[0])
    li = lax.broadcasted_iota(jnp.int32, (SSD_CHUNK, SSD_CHUNK), 0)
    si = lax.broadcasted_iota(jnp.int32, (SSD_CHUNK, SSD_CHUNK), 1)
    lane = lax.broadcasted_iota(jnp.int32, (1, LANES), 1)
    lo_half = lane < SSD_HEAD_DIM

    for d in (1, 0):
        valid = (li >= si) if d == 0 else (li <= si)
        tri = valid.astype(F32)
        end_row = SSD_CHUNK - 1 if d == 0 else 0
        for k in range(nblk):
            s_ref[k] = h0_ref[0, d, 0, k]

        def chunk(i, carry, d=d, valid=valid, tri=tri, end_row=end_row):
            c = i if d == 0 else nchunk - 1 - i
            r0 = pl.multiple_of(c * SSD_CHUNK, SSD_CHUNK)
            rows = pl.ds(r0, SSD_CHUNK)
            bm = bs_ref[rows, :]
            cm = cs_ref[rows, :]
            dt = _softplus(dt_ref[0, 0, rows, :] + dtb)
            cum = _dot32(tri, dt * a_neg)
            cum_t = cum.T
            dt_t = dt.T
            cb = lax.dot_general(cm.astype(BF16), bm.astype(BF16), (((1,), (1,)), ((), ())),
                                 preferred_element_type=F32)
            bm_t = bm.T.astype(BF16)
            cm16 = cm.astype(BF16)
            for k in range(nblk):
                xk = xs_ref[k, rows, :]
                ms = []
                for e in range(2):
                    idx = d * SSD_HPG + 2 * k + e
                    seg = jnp.where(valid, cum[:, idx:idx + 1] - cum_t[idx:idx + 1, :], -jnp.inf)
                    ms.append((cb * jnp.exp(seg) * dt_t[idx:idx + 1, :]).astype(BF16))
                x_lo = jnp.where(lo_half, xk, 0.0).astype(BF16)
                x_hi = jnp.where(lo_half, 0.0, xk).astype(BF16)
                yk = (jnp.dot(ms[0], x_lo, preferred_element_type=F32)
                      + jnp.dot(ms[1], x_hi, preferred_element_type=F32))
                i0 = d * SSD_HPG + 2 * k
                ecol = jnp.where(lo_half, cum[:, i0:i0 + 1], cum[:, i0 + 1:i0 + 2])
                dcol = jnp.where(lo_half, dt[:, i0:i0 + 1], dt[:, i0 + 1:i0 + 2])
                tot = jnp.where(lo_half, cum[end_row:end_row + 1, i0:i0 + 1],
                                cum[end_row:end_row + 1, i0 + 1:i0 + 2])
                sk = s_ref[k]
                yk = yk + jnp.dot(cm16, sk.astype(BF16), preferred_element_type=F32) * jnp.exp(ecol)
                xw = (xk * dcol * jnp.exp(tot - ecol)).astype(BF16)
                s_ref[k] = sk * jnp.exp(tot) + jnp.dot(bm_t, xw, preferred_element_type=F32)
                if d == 1:
                    y_ref[0, k, rows, :] = yk
                else:
                    ytot = y_ref[0, k, rows, :] + yk + dvec_ref[k] * xk
                    y_ref[0, k, rows, :] = ytot * _silu(z_ref[0, k, rows, :])
            return carry

        lax.fori_loop(0, nchunk, chunk, 0)
        for k in range(nblk):
            ht_ref[0, d, 0, k] = s_ref[k]


def _ssd(p, h0, conv_w, conv_b, dtb, alog, dvec, *, z_blk0, xbc_blk0, dt_blk0):
    bsz, _, t, _ = p.shape
    g4 = lambda off: (lambda b, g: (b, off // 4 + g, 0, 0))
    g1 = lambda off: (lambda b, g: (b, off + g, 0, 0))
    big = pl.BlockSpec((1, 4, t, LANES), g4(xbc_blk0))
    one = lambda off: pl.BlockSpec((1, 1, t, LANES), g1(off))
    st_spec = pl.BlockSpec((1, 2, 1, 4, SSD_STATE, LANES), lambda b, g: (b, 0, g, 0, 0, 0))
    return pl.pallas_call(
        functools.partial(_ssd_kernel, t=t),
        grid=(bsz, SSD_GROUPS),
        in_specs=[big, one(xbc_blk0 + 8), one(xbc_blk0 + 10), one(dt_blk0),
                  pl.BlockSpec((1, 4, t, LANES), g4(z_blk0)),
                  pl.BlockSpec((4, 4, LANES), lambda b, g: (g, 0, 0)),
                  pl.BlockSpec((1, 4, LANES), lambda b, g: (8 + g, 0, 0)),
                  pl.BlockSpec((1, 4, LANES), lambda b, g: (10 + g, 0, 0)),
                  pl.BlockSpec((4, 1, LANES), lambda b, g: (g, 0, 0)),
                  pl.BlockSpec((1, 1, LANES), lambda b, g: (8 + g, 0, 0)),
                  pl.BlockSpec((1, 1, LANES), lambda b, g: (10 + g, 0, 0)),
                  pl.BlockSpec((1, 1, LANES), lambda b, g: (g, 0, 0)),
                  pl.BlockSpec((1, 1, LANES), lambda b, g: (g, 0, 0)),
                  pl.BlockSpec((4, 1, LANES), lambda b, g: (g, 0, 0)),
                  st_spec],
        out_specs=[pl.BlockSpec((1, 4, t, LANES), lambda b, g: (b, g, 0, 0)), st_spec],
        out_shape=[jax.ShapeDtypeStruct((bsz, 8, t, LANES), F32),
                   jax.ShapeDtypeStruct(h0.shape, F32)],
        scratch_shapes=[pltpu.VMEM((4, t, LANES), F32), pltpu.VMEM((t, LANES), F32),
                        pltpu.VMEM((t, LANES), F32), pltpu.VMEM((4, SSD_STATE, LANES), F32)],
        compiler_params=_cparams(("parallel", "parallel")),
        name="ssd",
    )(p, p, p, p, p, conv_w, conv_w, conv_w, conv_b, conv_b, conv_b, dtb, alog, dvec, h0)


LRU_NB = 2
N_SEG = 8


def _seg_rows(t):
    seg = t // N_SEG
    return seg, seg + 8


def _conv4(v, w, b):
    return _conv4_silu(v, w, b)


def _seg_scan(a_ref, b_ref, lead, seg, pitch, reverse):
    def step(i, carry):
        h, acc = carry
        j = (seg - 1 - i) if reverse else i
        idx = lead + (pl.ds(j, N_SEG, stride=pitch), slice(None))
        a = a_ref[idx]
        h = a * h + b_ref[idx]
        acc = acc * a
        a_ref[idx] = acc
        b_ref[idx] = h
        return h, acc

    return lax.fori_loop(0, seg, step, (jnp.zeros((N_SEG, LANES), F32), jnp.ones((N_SEG, LANES), F32)))


def _seg_inputs(h0, end, tot, reverse):
    rows = [None] * N_SEG
    hin = h0
    for s in (range(N_SEG - 1, -1, -1) if reverse else range(N_SEG)):
        rows[s] = hin
        hin = tot[s:s + 1] * hin + end[s:s + 1]
    return rows, hin


def _lru_kernel(u_ref, gy_ref, cw_ref, cb_ref, wa_ref, wi_ref, ba_ref, bi_ref, lam_ref, h0_ref,
                r_ref, ht_ref, uc_ref, a_ref, b_ref, *, t):
    seg, pitch = _seg_rows(t)
    for k in range(LRU_NB):
        uc_ref[k] = _conv4(u_ref[0, k], cw_ref[k], cb_ref[k])
    for d in (1, 0):
        for k in range(LRU_NB):
            decay_rate = -LRU_C * _softplus(-lam_ref[d, k])
            for s in range(N_SEG):
                u = uc_ref[k, s * seg:(s + 1) * seg, :]
                u16 = u.astype(BF16)
                r = _sigmoid(jnp.dot(u16, wa_ref[d, k], preferred_element_type=F32) + ba_ref[d, k])
                i = _sigmoid(jnp.dot(u16, wi_ref[d, k], preferred_element_type=F32) + bi_ref[d, k])
                a = jnp.exp(decay_rate * r)
                a_ref[d, k, s * pitch:s * pitch + seg, :] = a
                b_ref[d, k, s * pitch:s * pitch + seg, :] = jnp.sqrt((1.0 - a) * (1.0 + a)) * (i * u)
    for d in (1, 0):
        for k in range(LRU_NB):
            end, tot = _seg_scan(a_ref, b_ref, (d, k), seg, pitch, reverse=(d == 1))
            rows, hfin = _seg_inputs(h0_ref[0, d, k], end, tot, reverse=(d == 1))
            ht_ref[0, d, k] = hfin
            for s in range(N_SEG):
                src = slice(s * pitch, s * pitch + seg)
                dst = slice(s * seg, (s + 1) * seg)
                h = b_ref[d, k, src, :] + a_ref[d, k, src, :] * rows[s]
                if d == 1:
                    r_ref[0, k, dst, :] = h
                else:
                    r_ref[0, k, dst, :] = (r_ref[0, k, dst, :] + h) * _gelu_tanh(gy_ref[0, k, dst, :])


def _lru(p, h0, conv_w, conv_b, wa, wi, ba, bi, lam, *, gy_blk0, u_blk0):
    bsz, _, t, _ = p.shape
    nb = LRU_NB
    seg, pitch = _seg_rows(t)
    blk = lambda off: pl.BlockSpec((1, nb, t, LANES), lambda b, g: (b, off // nb + g, 0, 0))
    par = lambda shape: pl.BlockSpec(shape, lambda b, g: (0, g) + (0,) * (len(shape) - 2))
    st_spec = pl.BlockSpec((1, 2, nb, 1, LANES), lambda b, g: (b, 0, g, 0, 0))
    return pl.pallas_call(
        functools.partial(_lru_kernel, t=t),
        grid=(bsz, LRU_BLOCKS // nb),
        in_specs=[blk(u_blk0), blk(gy_blk0),
                  pl.BlockSpec((nb, 4, LANES), lambda b, g: (g, 0, 0)),
                  pl.BlockSpec((nb, 1, LANES), lambda b, g: (g, 0, 0)),
                  par((2, nb, LANES, LANES)), par((2, nb, LANES, LANES)),
                  par((2, nb, 1, LANES)), par((2, nb, 1, LANES)), par((2, nb, 1, LANES)),
                  st_spec],
        out_specs=[pl.BlockSpec((1, nb, t, LANES), lambda b, g: (b, g, 0, 0)), st_spec],
        out_shape=[jax.ShapeDtypeStruct((bsz, LRU_BLOCKS, t, LANES), F32),
                   jax.ShapeDtypeStruct(h0.shape, F32)],
        scratch_shapes=[pltpu.VMEM((nb, t, LANES), F32),
                        pltpu.VMEM((2, nb, N_SEG * pitch, LANES), F32),
                        pltpu.VMEM((2, nb, N_SEG * pitch, LANES), F32)],
        compiler_params=_cparams(("parallel", "parallel")),
        name="rglru",
    )(p, p, conv_w, conv_b, wa, wi, ba, bi, lam, h0)


EV_Z0, EV_XBC0, EV_DT0, EV_GY0, EV_U0, EV_NBLK = 0, 8, 20, 22, 30, 38


def _blocks(v, n):
    return v.reshape(n, 1, LANES)


def _per_group_heads(v):
    v = v.reshape(2, SSD_GROUPS, SSD_HPG).transpose(1, 0, 2).reshape(SSD_GROUPS, 2 * SSD_HPG)
    return jnp.pad(v, ((0, 0), (0, LANES - 2 * SSD_HPG))).reshape(SSD_GROUPS, 1, LANES)


def _even_weights(w_in, conv_w, conv_b, dt_bias, a_log, ssd_d, lru_conv_w, lru_conv_b, w_a, b_a, w_i, b_i, lam):
    d = w_in.shape[0]
    z, xbc, dt, gy, u = jnp.split(w_in, (1024, 2560, 2592, 3616), axis=1)
    dt = dt.reshape(d, 2, SSD_GROUPS, SSD_HPG).transpose(0, 2, 1, 3).reshape(d, SSD_GROUPS, 2 * SSD_HPG)
    dt = jnp.pad(dt, ((0, 0), (0, 0), (0, LANES - 2 * SSD_HPG))).reshape(d, SSD_GROUPS * LANES)
    return dict(
        w_in=jnp.concatenate([z, xbc, dt, gy, u], axis=1).astype(BF16),
        conv_w=conv_w.reshape(4, 12, LANES).transpose(1, 0, 2), conv_b=_blocks(conv_b, 12),
        dtb=_per_group_heads(dt_bias), alog=_per_group_heads(a_log),
        dvec=_blocks(jnp.repeat(ssd_d, SSD_HEAD_DIM), 8),
        lru_conv_w=lru_conv_w.reshape(4, LRU_BLOCKS, LANES).transpose(1, 0, 2),
        lru_conv_b=_blocks(lru_conv_b, LRU_BLOCKS),
        w_a=w_a.astype(BF16), w_i=w_i.astype(BF16),
        b_a=b_a.reshape(2, LRU_BLOCKS, 1, LANES), b_i=b_i.reshape(2, LRU_BLOCKS, 1, LANES),
        lam=lam.reshape(2, LRU_BLOCKS, 1, LANES))


def _even_scans(p, states, w):
    ssd_h, lru_h = states
    y, ssd_h = _ssd(p, ssd_h, w["conv_w"], w["conv_b"], w["dtb"], w["alog"], w["dvec"],
                    z_blk0=EV_Z0, xbc_blk0=EV_XBC0, dt_blk0=EV_DT0)
    r, lru_h = _lru(p, lru_h, w["lru_conv_w"], w["lru_conv_b"], w["w_a"], w["w_i"], w["b_a"], w["b_i"],
                    w["lam"], gy_blk0=EV_GY0, u_blk0=EV_U0)
    return y, r, (ssd_h, lru_h)


def _even_zero_states(bsz):
    return (jnp.zeros((bsz, 2, SSD_GROUPS, 4, SSD_STATE, LANES), F32),
            jnp.zeros((bsz, 2, LRU_BLOCKS, 1, LANES), F32))


HG_CHUNK = 128
HG_PAR = 2
SUBLANES = 8


def _group_boundary(cum, c, reverse):
    m = c // 2
    off = m if reverse else m - 1
    n = cum.shape[0]
    if c >= 2 * SUBLANES:
        r = cum.reshape(n // c, c, LANES)
        return jnp.broadcast_to(r[:, off:off + 1, :], r.shape).reshape(n, LANES)
    r = cum.reshape(n // SUBLANES, SUBLANES, LANES)
    sub = lax.broadcasted_iota(jnp.int32, r.shape, 1)
    p = None
    for g0 in range(0, SUBLANES, c):
        cand = jnp.broadcast_to(r[:, g0 + off:g0 + off + 1, :], r.shape)
        p = cand if p is None else jnp.where(sub >= g0, cand, p)
    return p.reshape(n, LANES)


def _hgrn_masks():
    l = np.arange(HG_CHUNK)[:, None]
    s = np.arange(HG_CHUNK)[None, :]
    fwd = []
    size = HG_CHUNK
    while size >= 2:
        half = size // 2
        fwd.append((l // size == s // size) & (l % size >= half) & (s % size < half))
        size = half
    fwd = np.stack(fwd).astype(np.float32)
    pairs = np.stack([fwd, fwd.transpose(0, 2, 1)])
    tri = np.stack([l >= s, l <= s]).astype(np.float32)
    return jnp.asarray(pairs), jnp.asarray(tri)


def _hgrn_chunks(jobs, q_ref, ff_ref, fb_ref, v_ref, lb_ref, sf_ref, sb_ref, pairs_ref, tri_ref):
    n = len(jobs)
    qq, kk, vv, cum = [], [], [], []
    for hd, reverse, rows in jobs:
        lb = lb_ref[hd]
        fx = (fb_ref if reverse else ff_ref)[0, hd, rows, :]
        e = jnp.exp(-jnp.abs(fx))
        big = 1.0 / (1.0 + e)
        small = e * big
        pos = fx >= 0.0
        log2_f = jnp.log2(lb + (1.0 - lb) * jnp.where(pos, big, small))
        kk.append((1.0 - lb) * jnp.where(pos, small, big))
        cum.append(_dot32(tri_ref[1 if reverse else 0], log2_f))
    for hd, reverse, rows in jobs:
        qq.append(_silu(q_ref[0, hd, rows, :]))
        vv.append(v_ref[0, hd, rows, :])
    att = [None] * n
    size = HG_CHUNK
    level = 0
    while size >= 2:
        for j, (hd, reverse, rows) in enumerate(jobs):
            fac = jnp.exp2(-jnp.abs(cum[j] - _group_boundary(cum[j], size, reverse)))
            a_l = lax.dot_general((qq[j] * fac).astype(BF16), (kk[j] * fac).astype(BF16),
                                  (((1,), (1,)), ((), ())), preferred_element_type=F32)
            a_l = a_l * pairs_ref[1 if reverse else 0, level]
            att[j] = a_l if att[j] is None else att[j] + a_l
        size //= 2
        level += 1
    outs = []
    for j, (hd, reverse, rows) in enumerate(jobs):
        s_ref = (sb_ref if reverse else sf_ref).at[hd]
        sv = s_ref[...]
        diag = jnp.sum(qq[j] * kk[j], axis=-1, keepdims=True)
        o = jnp.dot(att[j].astype(BF16), vv[j].astype(BF16), preferred_element_type=F32) + diag * vv[j]
        o = o + jnp.dot((qq[j] * jnp.exp2(cum[j])).astype(BF16), sv.astype(BF16), preferred_element_type=F32)
        end_row = 0 if reverse else HG_CHUNK - 1
        cum_end = cum[j][end_row:end_row + 1, :]
        kw = (kk[j] * jnp.exp2(cum_end - cum[j])).T.astype(BF16)
        keep = jnp.broadcast_to(jnp.exp2(cum_end), (HG_CHUNK, LANES)).T
        s_ref[...] = sv * keep + jnp.dot(kw, vv[j].astype(BF16), preferred_element_type=F32)
        outs.append(o)
    return outs


def _hgrn_kernel(q_ref, ff_ref, fb_ref, v_ref, g_ref, lb_ref, ng_ref, pairs_ref, tri_ref, h0_ref,
                 o_ref, ht_ref, sf_ref, sb_ref, *, t):
    nchunk = t // HG_CHUNK
    for hd in range(HG_PAR):
        sf_ref[hd] = h0_ref[0, 0, hd]
        sb_ref[hd] = h0_ref[0, 1, hd]

    def pair(i, carry, finish):
        jobs = []
        for hd in range(HG_PAR):
            for reverse in (False, True):
                c = (nchunk - 1 - i) if reverse else i
                jobs.append((hd, reverse, pl.ds(pl.multiple_of(c * HG_CHUNK, HG_CHUNK), HG_CHUNK)))
        outs = _hgrn_chunks(jobs, q_ref, ff_ref, fb_ref, v_ref, lb_ref, sf_ref, sb_ref, pairs_ref, tri_ref)
        for (hd, reverse, rows), o in zip(jobs, outs):
            if finish:
                o = o_ref[0, hd, rows, :] + o
                o = _rms(o, ng_ref[hd]) * _silu(g_ref[0, hd, rows, :])
            o_ref[0, hd, rows, :] = o
        return carry

    lax.fori_loop(0, nchunk // 2, functools.partial(pair, finish=False), 0)
    lax.fori_loop(nchunk // 2, nchunk, functools.partial(pair, finish=True), 0)
    for hd in range(HG_PAR):
        ht_ref[0, 0, hd] = sf_ref[hd]
        ht_ref[0, 1, hd] = sb_ref[hd]


def _hgrn(p, h0, lb, norm_g):
    bsz, _, t, _ = p.shape
    assert (t // HG_CHUNK) % 2 == 0
    hp = HG_PAR
    blk = lambda off: pl.BlockSpec((1, hp, t, LANES), lambda b, h: (b, off // hp + h, 0, 0))
    par = pl.BlockSpec((hp, 1, LANES), lambda b, h: (h, 0, 0))
    st_spec = pl.BlockSpec((1, 2, hp, LANES, LANES), lambda b, h: (b, 0, h, 0, 0))
    pairs, tri = _hgrn_masks()
    return pl.pallas_call(
        functools.partial(_hgrn_kernel, t=t),
        grid=(bsz, HG_HEADS // hp),
        in_specs=[blk(0), blk(HG_HEADS), blk(2 * HG_HEADS), blk(3 * HG_HEADS), blk(4 * HG_HEADS),
                  par, par,
                  pl.BlockSpec(pairs.shape, lambda b, h: (0, 0, 0, 0)),
                  pl.BlockSpec(tri.shape, lambda b, h: (0, 0, 0)),
                  st_spec],
        out_specs=[pl.BlockSpec((1, hp, t, LANES), lambda b, h: (b, h, 0, 0)), st_spec],
        out_shape=[jax.ShapeDtypeStruct((bsz, HG_HEADS, t, LANES), F32),
                   jax.ShapeDtypeStruct(h0.shape, F32)],
        scratch_shapes=[pltpu.VMEM((hp, LANES, LANES), F32), pltpu.VMEM((hp, LANES, LANES), F32)],
        compiler_params=_cparams(("parallel", "parallel")),
        name="hgrn2",
    )(p, p, p, p, p, lb.reshape(HG_HEADS, 1, LANES), norm_g.reshape(HG_HEADS, 1, LANES), pairs, tri, h0)


S5_W = S5_GROUPS * S5_GROUP_CH
S5_NSTATE = S5_GROUPS * S5_STATE
S5_NB = S5_NSTATE // LANES
S5_PAR = 4


def _s5_kernel(u_ref, are_ref, aim_ref, bd_ref, cd_ref, dvec_ref, gw_ref, gb_ref, h0_ref,
               y_ref, ht_ref, xr_ref, xi_ref, pw_ref, acc_ref, *, t):
    seg, pitch = _seg_rows(t)
    u = jnp.concatenate([u_ref[0, 0], u_ref[0, 1]], axis=-1)
    u16 = u.astype(BF16)
    acc_ref[...] = dvec_ref[...] * u
    for d in (1, 0):
        reverse = d == 1
        for j0 in range(0, S5_NB, S5_PAR):
            cols = slice(j0 * LANES, (j0 + S5_PAR) * LANES)
            for s in range(N_SEG):
                us = u16[s * seg:(s + 1) * seg]
                xr = jnp.dot(us, bd_ref[d, 0, :, cols], preferred_element_type=F32)
                xi = jnp.dot(us, bd_ref[d, 1, :, cols], preferred_element_type=F32)
                for k in range(S5_PAR):
                    xr_ref[k, s * pitch:s * pitch + seg, :] = xr[:, k * LANES:(k + 1) * LANES]
                    xi_ref[k, s * pitch:s * pitch + seg, :] = xi[:, k * LANES:(k + 1) * LANES]
            ar = [jnp.broadcast_to(are_ref[d, j0 + k], (N_SEG, LANES)) for k in range(S5_PAR)]
            ai = [jnp.broadcast_to(aim_ref[d, j0 + k], (N_SEG, LANES)) for k in range(S5_PAR)]

            def step(i, carry, reverse=reverse, ar=ar, ai=ai):
                j = (seg - 1 - i) if reverse else i
                out = []
                for k in range(S5_PAR):
                    hr, hi, pr, pi = carry[k]
                    idx = (k, pl.ds(j, N_SEG, stride=pitch), slice(None))
                    nhr = ar[k] * hr - ai[k] * hi + xr_ref[idx]
                    nhi = ar[k] * hi + ai[k] * hr + xi_ref[idx]
                    xr_ref[idx] = nhr
                    xi_ref[idx] = nhi
                    npr = ar[k] * pr - ai[k] * pi
                    npi = ar[k] * pi + ai[k] * pr
                    pw_ref[0, k, pl.ds(j, 1), :] = npr[0:1]
                    pw_ref[1, k, pl.ds(j, 1), :] = npi[0:1]
                    out.append((nhr, nhi, npr, npi))
                return tuple(out)

            zero = jnp.zeros((N_SEG, LANES), F32)
            one = jnp.ones((N_SEG, LANES), F32)
            fin = lax.fori_loop(0, seg, step, tuple((zero, zero, one, zero) for _ in range(S5_PAR)))
            ins = []
            for k in range(S5_PAR):
                end_r, end_i, tot_r, tot_i = fin[k]
                hin_r, hin_i = h0_ref[0, d, 0, j0 + k], h0_ref[0, d, 1, j0 + k]
                rows = [None] * N_SEG
                for s in (range(N_SEG - 1, -1, -1) if reverse else range(N_SEG)):
                    rows[s] = (hin_r, hin_i)
                    hin_r, hin_i = (tot_r[0:1] * hin_r - tot_i[0:1] * hin_i + end_r[s:s + 1],
                                    tot_r[0:1] * hin_i + tot_i[0:1] * hin_r + end_i[s:s + 1])
                ht_ref[0, d, 0, j0 + k] = hin_r
                ht_ref[0, d, 1, j0 + k] = hin_i
                ins.append(rows)
            for s in range(N_SEG):
                src = slice(s * pitch, s * pitch + seg)
                hr_parts, hi_parts = [], []
                for k in range(S5_PAR):
                    in_r, in_i = ins[k][s]
                    pr, pi = pw_ref[0, k], pw_ref[1, k]
                    hr_parts.append((xr_ref[k, src, :] + pr * in_r - pi * in_i).astype(BF16))
                    hi_parts.append((xi_ref[k, src, :] + pr * in_i + pi * in_r).astype(BF16))
                hr = jnp.concatenate(hr_parts, axis=-1)
                hi = jnp.concatenate(hi_parts, axis=-1)
                acc_ref[s * seg:(s + 1) * seg, :] += (
                    jnp.dot(hr, cd_ref[d, 0, cols, :], preferred_element_type=F32)
                    + jnp.dot(hi, cd_ref[d, 1, cols, :], preferred_element_type=F32))
    y = _gelu_tanh(acc_ref[...])
    y = y * _sigmoid(jnp.dot(y.astype(BF16), gw_ref[...], preferred_element_type=F32) + gb_ref[...])
    y_ref[0, 0] = y[:, :LANES]
    y_ref[0, 1] = y[:, LANES:]


def _s5(p, h0, a_re, a_im, bd, cd, dvec, glu_w, glu_b, *, u_blk0):
    bsz, _, t, _ = p.shape
    seg, pitch = _seg_rows(t)
    full = lambda a: pl.BlockSpec(a.shape, lambda b: (0,) * a.ndim)
    st_spec = pl.BlockSpec((1, 2, 2, S5_NB, 1, LANES), lambda b: (b, 0, 0, 0, 0, 0))
    return pl.pallas_call(
        functools.partial(_s5_kernel, t=t),
        grid=(bsz,),
        in_specs=[pl.BlockSpec((1, 2, t, LANES), lambda b: (b, u_blk0 // 2, 0, 0)),
                  full(a_re), full(a_im), full(bd), full(cd), full(dvec), full(glu_w), full(glu_b), st_spec],
        out_specs=[pl.BlockSpec((1, 2, t, LANES), lambda b: (b, 0, 0, 0)), st_spec],
        out_shape=[jax.ShapeDtypeStruct((bsz, 2, t, LANES), F32), jax.ShapeDtypeStruct(h0.shape, F32)],
        scratch_shapes=[pltpu.VMEM((S5_PAR, N_SEG * pitch, LANES), F32),
                        pltpu.VMEM((S5_PAR, N_SEG * pitch, LANES), F32),
                        pltpu.VMEM((2, S5_PAR, seg, LANES), F32),
                        pltpu.VMEM((t, S5_W), F32)],
        compiler_params=_cparams(("parallel",)),
        name="s5",
    )(p, a_re, a_im, bd, cd, dvec, glu_w, glu_b, h0)


def _s5_params(lam_re, lam_im, log_step, b_re, b_im, c_re, c_im):
    step = jnp.exp(log_step)[..., None]
    mag = jnp.exp(lam_re * step)
    ar, ai = mag * jnp.cos(lam_im * step), mag * jnp.sin(lam_im * step)
    den = lam_re * lam_re + lam_im * lam_im
    zr = ((ar - 1) * lam_re + ai * lam_im) / den
    zi = (ai * lam_re - (ar - 1) * lam_im) / den
    bbr = zr[..., None] * b_re - zi[..., None] * b_im
    bbi = zr[..., None] * b_im + zi[..., None] * b_re
    eye = jnp.eye(S5_GROUPS, dtype=F32)

    def in_mat(m):
        return jnp.einsum("dgpk,gh->dgkhp", m, eye).reshape(2, S5_W, S5_NSTATE)

    def out_mat(m):
        return jnp.einsum("dgkp,gh->dgphk", m, eye).reshape(2, S5_NSTATE, S5_W)

    bd = jnp.stack([in_mat(bbr), in_mat(bbi)], axis=1).astype(BF16)
    cd = jnp.stack([out_mat(c_re), -out_mat(c_im)], axis=1).astype(BF16)
    shape = (2, S5_NB, 1, LANES)
    return ar.reshape(shape), ai.reshape(shape), bd, cd


OD_U0 = 5 * HG_HEADS


def _odd_weights(w_in, lower_bound, hg_norm_g, lam_re, lam_im, log_step, b_re, b_im, c_re, c_im, s5_d,
                 glu_w, glu_b):
    a_re, a_im, bd, cd = _s5_params(lam_re, lam_im, log_step, b_re, b_im, c_re, c_im)
    return dict(w_in=w_in.astype(BF16), lb=lower_bound, norm_g=hg_norm_g, a_re=a_re, a_im=a_im, bd=bd, cd=cd,
                dvec=s5_d.reshape(1, S5_W), glu_w=glu_w.astype(BF16), glu_b=glu_b.reshape(1, S5_W))


def _odd_scans(p, states, w):
    hg_h, s5_h = states
    o, hg_h = _hgrn(p, hg_h, w["lb"], w["norm_g"])
    y, s5_h = _s5(p, s5_h, w["a_re"], w["a_im"], w["bd"], w["cd"], w["dvec"], w["glu_w"], w["glu_b"],
                  u_blk0=OD_U0)
    return o, y, (hg_h, s5_h)


def _odd_zero_states(bsz):
    return (jnp.zeros((bsz, 2, HG_HEADS, LANES, LANES), F32),
            jnp.zeros((bsz, 2, 2, S5_NB, 1, LANES), F32))


PROJ_TILE = 256
MIX_TILE = 512
FFN_TILE = 1024
FFN_COLS = 256


def kernel(x, c, ctx, c_ctx, w_mod, b_mod, norm_mix_g, norm_ffn_g, final_norm_g,
           ev_w_in, ev_w_out, ssd_conv_w, ssd_conv_b, ssd_dt_bias, ssd_a_log, ssd_d, ssd_norm_g,
           lru_conv_w, lru_conv_b, lru_w_a, lru_b_a, lru_w_i, lru_b_i, lru_lam,
           od_w_in, od_w_out, hg_lb_logits, hg_norm_g,
           s5_lam_re, s5_lam_im, s5_log_step, s5_b_re, s5_b_im, s5_c_re, s5_c_im, s5_d,
           s5_glu_w, s5_glu_b,
           ffn_w_gate, ffn_w_up, ffn_conv_w, ffn_conv_b, ffn_w_down):
    bsz, _, d = x.shape
    depth = w_mod.shape[0]
    prob = jax.nn.softmax(hg_lb_logits.astype(F32), axis=0)
    lower_bounds = (jnp.cumsum(prob, axis=0) - prob[0]).astype(hg_lb_logits.dtype)

    pad = (-(bsz + 1)) % SUBLANES
    cond = jnp.concatenate([c, c_ctx[None], jnp.zeros((pad, d), c.dtype)], axis=0)
    mods = _modulation(cond, w_mod, b_mod).transpose(0, 2, 1, 3)

    for layer in range(depth):
        last = layer == depth - 1
        j = layer // 2
        mod_x = mods[layer, :bsz]
        mod_c = mods[layer, bsz:bsz + 1]
        if layer % 2 == 0:
            w = _even_weights(ev_w_in[j], ssd_conv_w[j], ssd_conv_b[j], ssd_dt_bias[j], ssd_a_log[j], ssd_d[j],
                              lru_conv_w[j], lru_conv_b[j], lru_w_a[j], lru_b_a[j], lru_w_i[j], lru_b_i[j],
                              lru_lam[j])
            scans, zero_states = _even_scans, _even_zero_states
            w_out, norm_gain, norm_first = ev_w_out[j].astype(BF16), ssd_norm_g[j], True
        else:
            w = _odd_weights(od_w_in[j], lower_bounds[layer], hg_norm_g[j], s5_lam_re[j], s5_lam_im[j],
                             s5_log_step[j], s5_b_re[j], s5_b_im[j], s5_c_re[j], s5_c_im[j], s5_d[j],
                             s5_glu_w[j], s5_glu_b[j])
            scans, zero_states = _odd_scans, _odd_zero_states
            w_out, norm_gain, norm_first = od_w_out[j].astype(BF16), hg_norm_g[j], False
        ffn_w = _ffn_weights(ffn_w_gate[layer], ffn_w_up[layer], ffn_conv_w[layer], ffn_conv_b[layer],
                             ffn_w_down[layer], FFN_COLS)

        p_c = _project(ctx, mod_c, norm_mix_g[layer], w["w_in"], PROJ_TILE)
        a_c, b_c, states = scans(p_c, zero_states(bsz), w)
        p_x = _project(x, mod_x, norm_mix_g[layer], w["w_in"], PROJ_TILE)
        a_x, b_x, _ = scans(p_x, states, w)
        x = _mix_out(x, a_x, b_x, mod_x, norm_gain, w_out, norm_first, MIX_TILE)
        x = _conv_ffn(x, mod_x, norm_ffn_g[layer], final_norm_g, *ffn_w, grid_conv=True, final_norm=last,
                      tile=FFN_TILE, tf=FFN_COLS)
        if not last:
            ctx = _mix_out(ctx, a_c, b_c, mod_c, norm_gain, w_out, norm_first, MIX_TILE)
            ctx = _conv_ffn(ctx, mod_c, norm_ffn_g[layer], final_norm_g, *ffn_w, grid_conv=False,
                            final_norm=False, tile=FFN_TILE, tf=FFN_COLS)
    return x
```

```python
import functools
import math

import jax
import jax.numpy as jnp
import numpy as np
from jax import lax
from jax.experimental import pallas as pl
from jax.experimental.pallas import tpu as pltpu

LANES = 128
RMS_EPS = 1e-6
N_MOD = 6
GRID_W = 64
SSD_HEAD_DIM = 64
SSD_HEADS = 16
SSD_GROUPS = 2
SSD_HPG = SSD_HEADS // SSD_GROUPS
SSD_STATE = 128
SSD_CHUNK = 128
LRU_BLOCKS = 8
LRU_C = 8.0
HG_HEADS = 6
S5_GROUPS = 16
S5_GROUP_CH = 16
S5_STATE = 64
VMEM_LIMIT = 56 * 1024 * 1024

BF16 = jnp.bfloat16
F32 = jnp.float32
HIGHEST = lax.Precision.HIGHEST


def _cparams(sem):
    return pltpu.CompilerParams(dimension_semantics=sem, vmem_limit_bytes=VMEM_LIMIT)


def _dot(a, b):
    return jnp.dot(a.astype(BF16), b.astype(BF16), preferred_element_type=F32)


def _dot32(a, b):
    return jnp.dot(a, b, preferred_element_type=F32, precision=HIGHEST)


def _sigmoid(v):
    return 0.5 * jnp.tanh(0.5 * v) + 0.5


def _silu(v):
    return v * _sigmoid(v)


def _gelu_tanh(v):
    return 0.5 * v * (1.0 + jnp.tanh(math.sqrt(2.0 / math.pi) * (v + 0.044715 * (v * v * v))))


def _softplus(v):
    return jnp.maximum(v, 0.0) + jnp.log(1.0 + jnp.exp(-jnp.abs(v)))


def _rms(v, g):
    return v * lax.rsqrt(jnp.mean(v * v, axis=-1, keepdims=True) + RMS_EPS) * g


def _norm_mod(xv, g, shift, scale):
    return _rms(xv, g) * (1.0 + scale) + shift


def _mod_kernel(s_ref, w_ref, b_ref, o_ref):
    o_ref[0, 0] = _dot(_silu(s_ref[...]), w_ref[0]) + b_ref[0, 0]


def _modulation(s, w_mod, b_mod):
    depth, d, _ = w_mod.shape
    rows = s.shape[0]
    return pl.pallas_call(
        _mod_kernel,
        grid=(depth, N_MOD),
        in_specs=[pl.BlockSpec((rows, d), lambda l, j: (0, 0)),
                  pl.BlockSpec((1, d, d), lambda l, j: (l, 0, j)),
                  pl.BlockSpec((1, 1, 1, d), lambda l, j: (l, j, 0, 0))],
        out_specs=pl.BlockSpec((1, 1, rows, d), lambda l, j: (l, j, 0, 0)),
        out_shape=jax.ShapeDtypeStruct((depth, N_MOD, rows, d), F32),
        compiler_params=_cparams(("arbitrary", "arbitrary")),
        name="modulation",
    )(s, w_mod, b_mod.reshape(depth, N_MOD, 1, d))


def _proj_kernel(x_ref, mod_ref, g_ref, w_ref, o_ref, *, nblk):
    m = mod_ref[0]
    h = _norm_mod(x_ref[0], g_ref[...], m[0:1], m[1:2]).astype(BF16)
    group = 4
    for b0 in range(0, nblk, group):
        nb = min(group, nblk - b0)
        r = jnp.dot(h, w_ref[:, b0 * LANES:(b0 + nb) * LANES], preferred_element_type=F32)
        for k in range(nb):
            o_ref[0, b0 + k] = r[:, k * LANES:(k + 1) * LANES]


def _project(x, mod, gain, w, tile):
    bsz, t, d = x.shape
    nblk = w.shape[1] // LANES
    tile = min(tile, t)
    mod_map = (lambda b, i: (b, 0, 0)) if mod.shape[0] == bsz else (lambda b, i: (0, 0, 0))
    return pl.pallas_call(
        functools.partial(_proj_kernel, nblk=nblk),
        grid=(bsz, t // tile),
        in_specs=[pl.BlockSpec((1, tile, d), lambda b, i: (b, i, 0)),
                  pl.BlockSpec((1, N_MOD, d), mod_map),
                  pl.BlockSpec((1, d), lambda b, i: (0, 0)),
                  pl.BlockSpec((d, nblk * LANES), lambda b, i: (0, 0), pipeline_mode=pl.Buffered(1))],
        out_specs=pl.BlockSpec((1, nblk, tile, LANES), lambda b, i: (b, 0, i, 0)),
        out_shape=jax.ShapeDtypeStruct((bsz, nblk, t, LANES), F32),
        compiler_params=_cparams(("parallel", "parallel")),
        name="project",
    )(x, mod, gain.reshape(1, d), w)


def _mix_out_kernel(x_ref, ma_ref, mb_ref, mod_ref, ng_ref, w_ref, o_ref, *, norm_first):
    m = mod_ref[0]
    pa = [ma_ref[0, k] for k in range(ma_ref.shape[1])]
    pb = [mb_ref[0, k].astype(BF16) for k in range(mb_ref.shape[1])]
    if norm_first:
        pa = [_rms(jnp.concatenate(pa, axis=-1), ng_ref[...]).astype(BF16)]
    else:
        pa = [p.astype(BF16) for p in pa]
    v = jnp.concatenate(pa + pb, axis=-1)
    o_ref[0] = x_ref[0] + m[2:3] * jnp.dot(v, w_ref[...], preferred_element_type=F32)


def _mix_out(x, mix_a, mix_b, mod, norm_gain, w, norm_first, tile):
    bsz, t, d = x.shape
    tile = min(tile, t)
    mod_map = (lambda b, i: (b, 0, 0)) if mod.shape[0] == bsz else (lambda b, i: (0, 0, 0))
    ng = norm_gain.reshape(1, -1)
    mix_spec = lambda a: pl.BlockSpec((1, a.shape[1], tile, LANES), lambda b, i: (b, 0, i, 0))
    return pl.pallas_call(
        functools.partial(_mix_out_kernel, norm_first=norm_first),
        grid=(bsz, t // tile),
        in_specs=[pl.BlockSpec((1, tile, d), lambda b, i: (b, i, 0)),
                  mix_spec(mix_a), mix_spec(mix_b),
                  pl.BlockSpec((1, N_MOD, d), mod_map),
                  pl.BlockSpec(ng.shape, lambda b, i: (0, 0)),
                  pl.BlockSpec(w.shape, lambda b, i: (0, 0), pipeline_mode=pl.Buffered(1))],
        out_specs=pl.BlockSpec((1, tile, d), lambda b, i: (b, i, 0)),
        out_shape=jax.ShapeDtypeStruct((bsz, t, d), F32),
        compiler_params=_cparams(("parallel", "parallel")),
        name="mix_out",
    )(x, mix_a, mix_b, mod, ng, w)


FFN_ROWS = 128
FFN_PAD = 8
def _ffn_kernel(x_ref, xp_ref, xn_ref, mod_ref, g_ref, fg_ref, wg_ref, wu_ref, cw_ref, cb_ref, wd_ref,
                o_ref, fx_ref, gt_ref, a_ref, *, tile, halo, tf, grid_conv, final_norm):
    i = pl.program_id(1)
    nt = pl.num_programs(1)
    nf = wg_ref.shape[0]
    m = mod_ref[0]
    fx_ref[halo:halo + tile] = _norm_mod(x_ref[0], g_ref[...], m[3:4], m[4:5]).astype(BF16)
    if halo:
        keep_p = jnp.where(i > 0, 1.0, 0.0)
        keep_n = jnp.where(i < nt - 1, 1.0, 0.0)
        fx_ref[0:halo] = (keep_p * _norm_mod(xp_ref[0], g_ref[...], m[3:4], m[4:5])).astype(BF16)
        fx_ref[halo + tile:] = (keep_n * _norm_mod(xn_ref[0], g_ref[...], m[3:4], m[4:5])).astype(BF16)

    rc = FFN_ROWS
    n_gate = (tile + 2 * halo) // rc
    n_out = tile // rc
    for slot in range(2):
        a_ref[slot, 0:FFN_PAD] = jnp.zeros((FFN_PAD, tf), F32)
        a_ref[slot, FFN_PAD + tile + 2 * halo:] = jnp.zeros((FFN_PAD, tf), F32)
    pos = lax.broadcasted_iota(jnp.int32, (rc, tf), 0)

    def gate_rows(f, slot, r):
        a_ref[slot, FFN_PAD + r * rc:FFN_PAD + (r + 1) * rc] = jnp.dot(
            fx_ref[r * rc:(r + 1) * rc], wg_ref[f], preferred_element_type=F32)

    def out_rows(f, slot, r):
        up = jnp.dot(fx_ref[halo + r * rc:halo + (r + 1) * rc], wu_ref[f], preferred_element_type=F32)
        cw = cw_ref[f]
        conv = cb_ref[f]
        if grid_conv:
            first = (pos % GRID_W) == 0
            last = (pos % GRID_W) == GRID_W - 1
            taps = [(dr, dw) for dr in range(3) for dw in range(3)]
        else:
            first = (pos + r * rc) == 0
            last = (pos + r * rc) == tile - 1
            taps = [(1, dw) for dw in range(3)]
        for dr, dw in taps:
            off = FFN_PAD + r * rc + (GRID_W * dr if grid_conv else 0) + dw - 1
            v = a_ref[slot, off:off + rc]
            if dw == 0:
                v = jnp.where(first, 0.0, v)
            elif dw == 2:
                v = jnp.where(last, 0.0, v)
            conv = conv + cw[3 * dr + dw:3 * dr + dw + 1] * v
        start = f * tf if isinstance(f, int) else pl.multiple_of(f * tf, tf)
        gt_ref[r * rc:(r + 1) * rc, pl.ds(start, tf)] = (_silu(conv) * up).astype(BF16)

    def columns(f, carry, prefetch):
        slot = f % 2
        for r in range(max(n_gate, n_out)):
            if prefetch and r < n_gate:
                gate_rows(f + 1, 1 - slot, r)
            if r < n_out:
                out_rows(f, slot, r)
        return carry

    for r in range(n_gate):
        gate_rows(0, 0, r)
    lax.fori_loop(0, nf - 1, functools.partial(columns, prefetch=True), 0)
    columns(nf - 1, 0, prefetch=False)
    y = x_ref[0] + m[5:6] * jnp.dot(gt_ref[...], wd_ref[...], preferred_element_type=F32)
    if final_norm:
        y = _rms(y, fg_ref[...])
    o_ref[0] = y


def _conv_ffn(x, mod, gain, final_gain, w_gate, w_up, conv_w, conv_b, w_down, grid_conv, final_norm,
              tile, tf):
    bsz, t, d = x.shape
    nf = w_gate.shape[0]
    dff = nf * tf
    tile = min(tile, t)
    halo = GRID_W if grid_conv else 0
    nh = t // GRID_W
    per = tile // GRID_W
    mod_map = (lambda b, i: (b, 0, 0)) if mod.shape[0] == bsz else (lambda b, i: (0, 0, 0))
    resident = lambda a: pl.BlockSpec(a.shape, lambda b, i: (0,) * a.ndim, pipeline_mode=pl.Buffered(1))
    return pl.pallas_call(
        functools.partial(_ffn_kernel, tile=tile, halo=halo, tf=tf, grid_conv=grid_conv, final_norm=final_norm),
        grid=(bsz, t // tile),
        in_specs=[pl.BlockSpec((1, tile, d), lambda b, i: (b, i, 0)),
                  pl.BlockSpec((1, GRID_W, d), lambda b, i: (b, jnp.maximum(i * per - 1, 0), 0)),
                  pl.BlockSpec((1, GRID_W, d), lambda b, i: (b, jnp.minimum((i + 1) * per, nh - 1), 0)),
                  pl.BlockSpec((1, N_MOD, d), mod_map),
                  pl.BlockSpec((1, d), lambda b, i: (0, 0)),
                  pl.BlockSpec((1, d), lambda b, i: (0, 0)),
                  resident(w_gate), resident(w_up), resident(conv_w), resident(conv_b), resident(w_down)],
        out_specs=pl.BlockSpec((1, tile, d), lambda b, i: (b, i, 0)),
        out_shape=jax.ShapeDtypeStruct((bsz, t, d), F32),
        scratch_shapes=[pltpu.VMEM((tile + 2 * halo, d), BF16), pltpu.VMEM((tile, dff), BF16),
                        pltpu.VMEM((2, tile + 2 * halo + 2 * FFN_PAD, tf), F32)],
        compiler_params=_cparams(("parallel", "parallel")),
        name="conv_ffn",
    )(x, x, x, mod, gain.reshape(1, d), final_gain.reshape(1, d), w_gate, w_up, conv_w, conv_b, w_down)


def _ffn_weights(w_gate, w_up, conv_w, conv_b, w_down, tf):
    d, dff = w_gate.shape
    nf = dff // tf
    tiles = lambda w: w.reshape(w.shape[0], nf, tf).transpose(1, 0, 2)
    return (tiles(w_gate).astype(BF16), tiles(w_up).astype(BF16), tiles(conv_w.reshape(9, dff)),
            tiles(conv_b.reshape(1, dff)), w_down.astype(BF16))


def _conv4_silu(v, w, b):
    t = v.shape[0]
    pos = lax.broadcasted_iota(jnp.int32, v.shape, 0)
    acc = b + w[1:2] * v
    acc = acc + w[0:1] * jnp.where(pos < 1, 0.0, pltpu.roll(v, 1, axis=0))
    acc = acc + w[2:3] * jnp.where(pos >= t - 1, 0.0, pltpu.roll(v, t - 1, axis=0))
    acc = acc + w[3:4] * jnp.where(pos >= t - 2, 0.0, pltpu.roll(v, t - 2, axis=0))
    return acc


def _ssd_kernel(xr_ref, br_ref, cr_ref, dt_ref, z_ref, cwx_ref, cwb_ref, cwc_ref, cbx_ref, cbb_ref, cbc_ref,
                dtb_ref, alog_ref, dvec_ref, h0_ref, y_ref, ht_ref, xs_ref, bs_ref, cs_ref, s_ref, *, t):
    nblk = 4
    nchunk = t // SSD_CHUNK
    for k in range(nblk):
        xs_ref[k] = _silu(_conv4_silu(xr_ref[0, k], cwx_ref[k], cbx_ref[k]))
    bs_ref[...] = _silu(_conv4_silu(br_ref[0, 0], cwb_ref[0], cbb_ref[0]))
    cs_ref[...] = _silu(_conv4_silu(cr_ref[0, 0], cwc_ref[0], cbc_ref[0]))

    dtb = dtb_ref[0]
    a_neg = -jnp.exp(alog_ref[0])
    li = lax.broadcasted_iota(jnp.int32, (SSD_CHUNK, SSD_CHUNK), 0)
    si = lax.broadcasted_iota(jnp.int32, (SSD_CHUNK, SSD_CHUNK), 1)
    lane = lax.broadcasted_iota(jnp.int32, (1, LANES), 1)
    lo_half = lane < SSD_HEAD_DIM

    valid = [li >= si, li <= si]
    tri = [v.astype(F32) for v in valid]
    for d in (0, 1):
        for k in range(nblk):
            s_ref[d, k] = h0_ref[0, d, 0, k]

    def pair(i, carry, finish):
        jobs = [(0, pl.ds(pl.multiple_of(i * SSD_CHUNK, SSD_CHUNK), SSD_CHUNK)),
                (1, pl.ds(pl.multiple_of((nchunk - 1 - i) * SSD_CHUNK, SSD_CHUNK), SSD_CHUNK))]
        pre = []
        for d, rows in jobs:
            bm = bs_ref[rows, :]
            cm = cs_ref[rows, :]
            dt = _softplus(dt_ref[0, 0, rows, :] + dtb)
            cum = _dot32(tri[d], dt * a_neg)
            cb = lax.dot_general(cm.astype(BF16), bm.astype(BF16), (((1,), (1,)), ((), ())),
                                 preferred_element_type=F32)
            pre.append((dt, cum, cum.T, dt.T, cb, bm.T.astype(BF16), cm.astype(BF16)))
        for k in range(nblk):
            for (d, rows), (dt, cum, cum_t, dt_t, cb, bm_t, cm16) in zip(jobs, pre):
                end_row = SSD_CHUNK - 1 if d == 0 else 0
                xk = xs_ref[k, rows, :]
                ms = []
                for e in range(2):
                    idx = d * SSD_HPG + 2 * k + e
                    seg = jnp.where(valid[d], cum[:, idx:idx + 1] - cum_t[idx:idx + 1, :], -jnp.inf)
                    ms.append((cb * jnp.exp(seg) * dt_t[idx:idx + 1, :]).astype(BF16))
                x_lo = jnp.where(lo_half, xk, 0.0).astype(BF16)
                x_hi = jnp.where(lo_half, 0.0, xk).astype(BF16)
                yk = (jnp.dot(ms[0], x_lo, preferred_element_type=F32)
                      + jnp.dot(ms[1], x_hi, preferred_element_type=F32))
                i0 = d * SSD_HPG + 2 * k
                ecol = jnp.where(lo_half, cum[:, i0:i0 + 1], cum[:, i0 + 1:i0 + 2])
                dcol = jnp.where(lo_half, dt[:, i0:i0 + 1], dt[:, i0 + 1:i0 + 2])
                tot = jnp.where(lo_half, cum[end_row:end_row + 1, i0:i0 + 1],
                                cum[end_row:end_row + 1, i0 + 1:i0 + 2])
                sk = s_ref[d, k]
                yk = yk + jnp.dot(cm16, sk.astype(BF16), preferred_element_type=F32) * jnp.exp(ecol)
                xw = (xk * dcol * jnp.exp(tot - ecol)).astype(BF16)
                s_ref[d, k] = sk * jnp.exp(tot) + jnp.dot(bm_t, xw, preferred_element_type=F32)
                if finish:
                    ytot = y_ref[0, k, rows, :] + yk + dvec_ref[k] * xk
                    y_ref[0, k, rows, :] = ytot * _silu(z_ref[0, k, rows, :])
                else:
                    y_ref[0, k, rows, :] = yk
        return carry

    lax.fori_loop(0, nchunk // 2, functools.partial(pair, finish=False), 0)
    lax.fori_loop(nchunk // 2, nchunk, functools.partial(pair, finish=True), 0)
    for d in (0, 1):
        for k in range(nblk):
            ht_ref[0, d, 0, k] = s_ref[d, k]


def _ssd(p, h0, conv_w, conv_b, dtb, alog, dvec, *, z_blk0, xbc_blk0, dt_blk0):
    bsz, _, t, _ = p.shape
    assert (t // SSD_CHUNK) % 2 == 0
    g4 = lambda off: (lambda b, g: (b, off // 4 + g, 0, 0))
    g1 = lambda off: (lambda b, g: (b, off + g, 0, 0))
    big = pl.BlockSpec((1, 4, t, LANES), g4(xbc_blk0))
    one = lambda off: pl.BlockSpec((1, 1, t, LANES), g1(off))
    st_spec = pl.BlockSpec((1, 2, 1, 4, SSD_STATE, LANES), lambda b, g: (b, 0, g, 0, 0, 0))
    return pl.pallas_call(
        functools.partial(_ssd_kernel, t=t),
        grid=(bsz, SSD_GROUPS),
        in_specs=[big, one(xbc_blk0 + 8), one(xbc_blk0 + 10), one(dt_blk0),
                  pl.BlockSpec((1, 4, t, LANES), g4(z_blk0)),
                  pl.BlockSpec((4, 4, LANES), lambda b, g: (g, 0, 0)),
                  pl.BlockSpec((1, 4, LANES), lambda b, g: (8 + g, 0, 0)),
                  pl.BlockSpec((1, 4, LANES), lambda b, g: (10 + g, 0, 0)),
                  pl.BlockSpec((4, 1, LANES), lambda b, g: (g, 0, 0)),
                  pl.BlockSpec((1, 1, LANES), lambda b, g: (8 + g, 0, 0)),
                  pl.BlockSpec((1, 1, LANES), lambda b, g: (10 + g, 0, 0)),
                  pl.BlockSpec((1, 1, LANES), lambda b, g: (g, 0, 0)),
                  pl.BlockSpec((1, 1, LANES), lambda b, g: (g, 0, 0)),
                  pl.BlockSpec((4, 1, LANES), lambda b, g: (g, 0, 0)),
                  st_spec],
        out_specs=[pl.BlockSpec((1, 4, t, LANES), lambda b, g: (b, g, 0, 0)), st_spec],
        out_shape=[jax.ShapeDtypeStruct((bsz, 8, t, LANES), F32),
                   jax.ShapeDtypeStruct(h0.shape, F32)],
        scratch_shapes=[pltpu.VMEM((4, t, LANES), F32), pltpu.VMEM((t, LANES), F32),
                        pltpu.VMEM((t, LANES), F32), pltpu.VMEM((2, 4, SSD_STATE, LANES), F32)],
        compiler_params=_cparams(("parallel", "parallel")),
        name="ssd",
    )(p, p, p, p, p, conv_w, conv_w, conv_w, conv_b, conv_b, conv_b, dtb, alog, dvec, h0)


LRU_NB = 2
N_SEG = 8
SCAN_UNROLL = 8


def _seg_rows(t):
    seg = t // N_SEG
    return seg, seg + 8


def _conv4(v, w, b):
    return _conv4_silu(v, w, b)


def _seg_scan(a_ref, b_ref, acc_ref, h_ref, lead, seg, pitch, reverse):
    def step(i, carry):
        out = []
        for (h, acc), ld, rev in zip(carry, lead, reverse):
            j = (seg - 1 - i) if rev else i
            idx = ld + (pl.ds(j, N_SEG, stride=pitch), slice(None))
            a = a_ref[idx]
            h = a * h + b_ref[idx]
            acc = acc * a
            acc_ref[idx] = acc
            h_ref[idx] = h
            out.append((h, acc))
        return tuple(out)

    init = (jnp.zeros((N_SEG, LANES), F32), jnp.ones((N_SEG, LANES), F32))
    return lax.fori_loop(0, seg, step, tuple(init for _ in lead), unroll=SCAN_UNROLL)


def _seg_inputs(h0, end, tot, reverse):
    rows = [None] * N_SEG
    hin = h0
    for s in (range(N_SEG - 1, -1, -1) if reverse else range(N_SEG)):
        rows[s] = hin
        hin = tot[s:s + 1] * hin + end[s:s + 1]
    return rows, hin


def _lru_kernel(u_ref, gy_ref, cw_ref, cb_ref, wa_ref, wi_ref, ba_ref, bi_ref, lam_ref, h0_ref,
                r_ref, ht_ref, uc_ref, a_ref, b_ref, acc_ref, hl_ref, *, t):
    seg, pitch = _seg_rows(t)
    for k in range(LRU_NB):
        uc_ref[k] = _conv4(u_ref[0, k], cw_ref[k], cb_ref[k])
    for d in (1, 0):
        for k in range(LRU_NB):
            decay_rate = -LRU_C * _softplus(-lam_ref[d, k])
            for s in range(N_SEG):
                u = uc_ref[k, s * seg:(s + 1) * seg, :]
                u16 = u.astype(BF16)
                r = _sigmoid(jnp.dot(u16, wa_ref[d, k], preferred_element_type=F32) + ba_ref[d, k])
                i = _sigmoid(jnp.dot(u16, wi_ref[d, k], preferred_element_type=F32) + bi_ref[d, k])
                a = jnp.exp(decay_rate * r)
                a_ref[d, k, s * pitch:s * pitch + seg, :] = a
                b_ref[d, k, s * pitch:s * pitch + seg, :] = jnp.sqrt((1.0 - a) * (1.0 + a)) * (i * u)
    chains = [(d, k) for d in (1, 0) for k in range(LRU_NB)]
    scanned = _seg_scan(a_ref, b_ref, acc_ref, hl_ref, chains, seg, pitch, [d == 1 for d, _ in chains])
    for (d, k), (end, tot) in zip(chains, scanned):
        rows, hfin = _seg_inputs(h0_ref[0, d, k], end, tot, reverse=(d == 1))
        ht_ref[0, d, k] = hfin
        for s in range(N_SEG):
            src = slice(s * pitch, s * pitch + seg)
            dst = slice(s * seg, (s + 1) * seg)
            h = hl_ref[d, k, src, :] + acc_ref[d, k, src, :] * rows[s]
            if d == 1:
                r_ref[0, k, dst, :] = h
            else:
                r_ref[0, k, dst, :] = (r_ref[0, k, dst, :] + h) * _gelu_tanh(gy_ref[0, k, dst, :])


def _lru(p, h0, conv_w, conv_b, wa, wi, ba, bi, lam, *, gy_blk0, u_blk0):
    bsz, _, t, _ = p.shape
    nb = LRU_NB
    seg, pitch = _seg_rows(t)
    blk = lambda off: pl.BlockSpec((1, nb, t, LANES), lambda b, g: (b, off // nb + g, 0, 0))
    par = lambda shape: pl.BlockSpec(shape, lambda b, g: (0, g) + (0,) * (len(shape) - 2))
    st_spec = pl.BlockSpec((1, 2, nb, 1, LANES), lambda b, g: (b, 0, g, 0, 0))
    return pl.pallas_call(
        functools.partial(_lru_kernel, t=t),
        grid=(bsz, LRU_BLOCKS // nb),
        in_specs=[blk(u_blk0), blk(gy_blk0),
                  pl.BlockSpec((nb, 4, LANES), lambda b, g: (g, 0, 0)),
                  pl.BlockSpec((nb, 1, LANES), lambda b, g: (g, 0, 0)),
                  par((2, nb, LANES, LANES)), par((2, nb, LANES, LANES)),
                  par((2, nb, 1, LANES)), par((2, nb, 1, LANES)), par((2, nb, 1, LANES)),
                  st_spec],
        out_specs=[pl.BlockSpec((1, nb, t, LANES), lambda b, g: (b, g, 0, 0)), st_spec],
        out_shape=[jax.ShapeDtypeStruct((bsz, LRU_BLOCKS, t, LANES), F32),
                   jax.ShapeDtypeStruct(h0.shape, F32)],
        scratch_shapes=[pltpu.VMEM((nb, t, LANES), F32)] + [pltpu.VMEM((2, nb, N_SEG * pitch, LANES), F32)] * 4,
        compiler_params=_cparams(("parallel", "parallel")),
        name="rglru",
    )(p, p, conv_w, conv_b, wa, wi, ba, bi, lam, h0)


EV_Z0, EV_XBC0, EV_DT0, EV_GY0, EV_U0, EV_NBLK = 0, 8, 20, 22, 30, 38


def _blocks(v, n):
    return v.reshape(n, 1, LANES)


def _per_group_heads(v):
    v = v.reshape(2, SSD_GROUPS, SSD_HPG).transpose(1, 0, 2).reshape(SSD_GROUPS, 2 * SSD_HPG)
    return jnp.pad(v, ((0, 0), (0, LANES - 2 * SSD_HPG))).reshape(SSD_GROUPS, 1, LANES)


def _even_weights(w_in, conv_w, conv_b, dt_bias, a_log, ssd_d, lru_conv_w, lru_conv_b, w_a, b_a, w_i, b_i, lam):
    d = w_in.shape[0]
    z, xbc, dt, gy, u = jnp.split(w_in, (1024, 2560, 2592, 3616), axis=1)
    dt = dt.reshape(d, 2, SSD_GROUPS, SSD_HPG).transpose(0, 2, 1, 3).reshape(d, SSD_GROUPS, 2 * SSD_HPG)
    dt = jnp.pad(dt, ((0, 0), (0, 0), (0, LANES - 2 * SSD_HPG))).reshape(d, SSD_GROUPS * LANES)
    return dict(
        w_in=jnp.concatenate([z, xbc, dt, gy, u], axis=1).astype(BF16),
        conv_w=conv_w.reshape(4, 12, LANES).transpose(1, 0, 2), conv_b=_blocks(conv_b, 12),
        dtb=_per_group_heads(dt_bias), alog=_per_group_heads(a_log),
        dvec=_blocks(jnp.repeat(ssd_d, SSD_HEAD_DIM), 8),
        lru_conv_w=lru_conv_w.reshape(4, LRU_BLOCKS, LANES).transpose(1, 0, 2),
        lru_conv_b=_blocks(lru_conv_b, LRU_BLOCKS),
        w_a=w_a.astype(BF16), w_i=w_i.astype(BF16),
        b_a=b_a.reshape(2, LRU_BLOCKS, 1, LANES), b_i=b_i.reshape(2, LRU_BLOCKS, 1, LANES),
        lam=lam.reshape(2, LRU_BLOCKS, 1, LANES))


def _even_scans(p, states, w):
    ssd_h, lru_h = states
    y, ssd_h = _ssd(p, ssd_h, w["conv_w"], w["conv_b"], w["dtb"], w["alog"], w["dvec"],
                    z_blk0=EV_Z0, xbc_blk0=EV_XBC0, dt_blk0=EV_DT0)
    r, lru_h = _lru(p, lru_h, w["lru_conv_w"], w["lru_conv_b"], w["w_a"], w["w_i"], w["b_a"], w["b_i"],
                    w["lam"], gy_blk0=EV_GY0, u_blk0=EV_U0)
    return y, r, (ssd_h, lru_h)


def _even_zero_states(bsz):
    return (jnp.zeros((bsz, 2, SSD_GROUPS, 4, SSD_STATE, LANES), F32),
            jnp.zeros((bsz, 2, LRU_BLOCKS, 1, LANES), F32))


HG_CHUNK = 128
HG_PAR = 2
SUBLANES = 8


def _group_boundary(cum, c, reverse):
    m = c // 2
    off = m if reverse else m - 1
    n = cum.shape[0]
    if c >= 2 * SUBLANES:
        r = cum.reshape(n // c, c, LANES)
        return jnp.broadcast_to(r[:, off:off + 1, :], r.shape).reshape(n, LANES)
    r = cum.reshape(n // SUBLANES, SUBLANES, LANES)
    sub = lax.broadcasted_iota(jnp.int32, r.shape, 1)
    p = None
    for g0 in range(0, SUBLANES, c):
        cand = jnp.broadcast_to(r[:, g0 + off:g0 + off + 1, :], r.shape)
        p = cand if p is None else jnp.where(sub >= g0, cand, p)
    return p.reshape(n, LANES)


def _hgrn_masks():
    l = np.arange(HG_CHUNK)[:, None]
    s = np.arange(HG_CHUNK)[None, :]
    fwd = []
    size = HG_CHUNK
    while size >= 2:
        half = size // 2
        fwd.append((l // size == s // size) & (l % size >= half) & (s % size < half))
        size = half
    fwd = np.stack(fwd).astype(np.float32)
    pairs = np.stack([fwd, fwd.transpose(0, 2, 1)])
    tri = np.stack([l >= s, l <= s]).astype(np.float32)
    return jnp.asarray(pairs), jnp.asarray(tri)


def _hgrn_chunks(jobs, q_ref, ff_ref, fb_ref, v_ref, lb_ref, sf_ref, sb_ref, pairs_ref, tri_ref):
    n = len(jobs)
    qq, kk, vv, cum = [], [], [], []
    for hd, reverse, rows in jobs:
        lb = lb_ref[hd]
        fx = (fb_ref if reverse else ff_ref)[0, hd, rows, :]
        e = jnp.exp(-jnp.abs(fx))
        big = 1.0 / (1.0 + e)
        small = e * big
        pos = fx >= 0.0
        log2_f = jnp.log2(lb + (1.0 - lb) * jnp.where(pos, big, small))
        kk.append((1.0 - lb) * jnp.where(pos, small, big))
        cum.append(_dot32(tri_ref[1 if reverse else 0], log2_f))
    for hd, reverse, rows in jobs:
        qq.append(_silu(q_ref[0, hd, rows, :]))
        vv.append(v_ref[0, hd, rows, :])
    att = [None] * n
    size = HG_CHUNK
    level = 0
    while size >= 2:
        for j, (hd, reverse, rows) in enumerate(jobs):
            fac = jnp.exp2(-jnp.abs(cum[j] - _group_boundary(cum[j], size, reverse)))
            a_l = lax.dot_general((qq[j] * fac).astype(BF16), (kk[j] * fac).astype(BF16),
                                  (((1,), (1,)), ((), ())), preferred_element_type=F32)
            a_l = a_l * pairs_ref[1 if reverse else 0, level]
            att[j] = a_l if att[j] is None else att[j] + a_l
        size //= 2
        level += 1
    outs = []
    for j, (hd, reverse, rows) in enumerate(jobs):
        s_ref = (sb_ref if reverse else sf_ref).at[hd]
        sv = s_ref[...]
        diag = jnp.sum(qq[j] * kk[j], axis=-1, keepdims=True)
        o = jnp.dot(att[j].astype(BF16), vv[j].astype(BF16), preferred_element_type=F32) + diag * vv[j]
        o = o + jnp.dot((qq[j] * jnp.exp2(cum[j])).astype(BF16), sv.astype(BF16), preferred_element_type=F32)
        end_row = 0 if reverse else HG_CHUNK - 1
        cum_end = cum[j][end_row:end_row + 1, :]
        kw = (kk[j] * jnp.exp2(cum_end - cum[j])).T.astype(BF16)
        keep = jnp.broadcast_to(jnp.exp2(cum_end), (HG_CHUNK, LANES)).T
        s_ref[...] = sv * keep + jnp.dot(kw, vv[j].astype(BF16), preferred_element_type=F32)
        outs.append(o)
    return outs


def _hgrn_kernel(q_ref, ff_ref, fb_ref, v_ref, g_ref, lb_ref, ng_ref, pairs_ref, tri_ref, h0_ref,
                 o_ref, ht_ref, sf_ref, sb_ref, *, t):
    nchunk = t // HG_CHUNK
    for hd in range(HG_PAR):
        sf_ref[hd] = h0_ref[0, 0, hd]
        sb_ref[hd] = h0_ref[0, 1, hd]

    def pair(i, carry, finish):
        jobs = []
        for hd in range(HG_PAR):
            for reverse in (False, True):
                c = (nchunk - 1 - i) if reverse else i
                jobs.append((hd, reverse, pl.ds(pl.multiple_of(c * HG_CHUNK, HG_CHUNK), HG_CHUNK)))
        outs = _hgrn_chunks(jobs, q_ref, ff_ref, fb_ref, v_ref, lb_ref, sf_ref, sb_ref, pairs_ref, tri_ref)
        for (hd, reverse, rows), o in zip(jobs, outs):
            if finish:
                o = o_ref[0, hd, rows, :] + o
                o = _rms(o, ng_ref[hd]) * _silu(g_ref[0, hd, rows, :])
            o_ref[0, hd, rows, :] = o
        return carry

    lax.fori_loop(0, nchunk // 2, functools.partial(pair, finish=False), 0)
    lax.fori_loop(nchunk // 2, nchunk, functools.partial(pair, finish=True), 0)
    for hd in range(HG_PAR):
        ht_ref[0, 0, hd] = sf_ref[hd]
        ht_ref[0, 1, hd] = sb_ref[hd]


def _hgrn(p, h0, lb, norm_g):
    bsz, _, t, _ = p.shape
    assert (t // HG_CHUNK) % 2 == 0
    hp = HG_PAR
    blk = lambda off: pl.BlockSpec((1, hp, t, LANES), lambda b, h: (b, off // hp + h, 0, 0))
    par = pl.BlockSpec((hp, 1, LANES), lambda b, h: (h, 0, 0))
    st_spec = pl.BlockSpec((1, 2, hp, LANES, LANES), lambda b, h: (b, 0, h, 0, 0))
    pairs, tri = _hgrn_masks()
    return pl.pallas_call(
        functools.partial(_hgrn_kernel, t=t),
        grid=(bsz, HG_HEADS // hp),
        in_specs=[blk(0), blk(HG_HEADS), blk(2 * HG_HEADS), blk(3 * HG_HEADS), blk(4 * HG_HEADS),
                  par, par,
                  pl.BlockSpec(pairs.shape, lambda b, h: (0, 0, 0, 0)),
                  pl.BlockSpec(tri.shape, lambda b, h: (0, 0, 0)),
                  st_spec],
        out_specs=[pl.BlockSpec((1, hp, t, LANES), lambda b, h: (b, h, 0, 0)), st_spec],
        out_shape=[jax.ShapeDtypeStruct((bsz, HG_HEADS, t, LANES), F32),
                   jax.ShapeDtypeStruct(h0.shape, F32)],
        scratch_shapes=[pltpu.VMEM((hp, LANES, LANES), F32), pltpu.VMEM((hp, LANES, LANES), F32)],
        compiler_params=_cparams(("parallel", "parallel")),
        name="hgrn2",
    )(p, p, p, p, p, lb.reshape(HG_HEADS, 1, LANES), norm_g.reshape(HG_HEADS, 1, LANES), pairs, tri, h0)


S5_W = S5_GROUPS * S5_GROUP_CH
S5_NSTATE = S5_GROUPS * S5_STATE
S5_NB = S5_NSTATE // LANES
S5_PAR = 2


def _s5_kernel(u_ref, are_ref, aim_ref, pw_ref, bd_ref, cd_ref, dvec_ref, gw_ref, gb_ref, h0_ref,
               y_ref, ht_ref, wr_ref, wi_ref, hr_ref, hi_ref, acc_ref, *, t):
    seg, pitch = _seg_rows(t)
    u = jnp.concatenate([u_ref[0, 0], u_ref[0, 1]], axis=-1)
    u16 = u.astype(BF16)
    acc_ref[...] = dvec_ref[...] * u
    chains = [(d, k) for d in (0, 1) for k in range(S5_PAR)]
    for j0 in range(0, S5_NB, S5_PAR):
        cols = slice(j0 * LANES, (j0 + S5_PAR) * LANES)
        for s in range(N_SEG):
            us = u16[s * seg:(s + 1) * seg]
            wr = jnp.dot(us, bd_ref[0, :, cols], preferred_element_type=F32)
            wi = jnp.dot(us, bd_ref[1, :, cols], preferred_element_type=F32)
            for k in range(S5_PAR):
                wr_ref[k, s * pitch:s * pitch + seg, :] = wr[:, k * LANES:(k + 1) * LANES]
                wi_ref[k, s * pitch:s * pitch + seg, :] = wi[:, k * LANES:(k + 1) * LANES]
        ar = [jnp.broadcast_to(are_ref[d, j0 + k], (N_SEG, LANES)) for d, k in chains]
        ai = [jnp.broadcast_to(aim_ref[d, j0 + k], (N_SEG, LANES)) for d, k in chains]

        def step(i, carry, ar=ar, ai=ai):
            out = []
            for c, (d, k) in enumerate(chains):
                gr, gi = carry[c]
                j = (seg - 1 - i) if d == 1 else i
                src = (k, pl.ds(j, N_SEG, stride=pitch), slice(None))
                dst = (d, k, pl.ds(j, N_SEG, stride=pitch), slice(None))
                ngr = ar[c] * gr - ai[c] * gi + wr_ref[src]
                ngi = ar[c] * gi + ai[c] * gr + wi_ref[src]
                hr_ref[dst] = ngr
                hi_ref[dst] = ngi
                out.append((ngr, ngi))
            return tuple(out)

        zero = jnp.zeros((N_SEG, LANES), F32)
        fin = lax.fori_loop(0, seg, step, tuple((zero, zero) for _ in chains), unroll=SCAN_UNROLL)
        for d in (0, 1):
            ins = []
            last = 0 if d == 1 else seg - 1
            for k in range(S5_PAR):
                end_r, end_i = fin[d * S5_PAR + k]
                tot_r = pw_ref[d, 0, j0 + k, last:last + 1, :]
                tot_i = pw_ref[d, 1, j0 + k, last:last + 1, :]
                hin_r, hin_i = h0_ref[0, d, 0, j0 + k], h0_ref[0, d, 1, j0 + k]
                rows = [None] * N_SEG
                for s in (range(N_SEG - 1, -1, -1) if d == 1 else range(N_SEG)):
                    rows[s] = (hin_r, hin_i)
                    hin_r, hin_i = (tot_r[0:1] * hin_r - tot_i[0:1] * hin_i + end_r[s:s + 1],
                                    tot_r[0:1] * hin_i + tot_i[0:1] * hin_r + end_i[s:s + 1])
                ht_ref[0, d, 0, j0 + k] = hin_r
                ht_ref[0, d, 1, j0 + k] = hin_i
                ins.append(rows)
            for s in range(N_SEG):
                src = slice(s * pitch, s * pitch + seg)
                gr_parts, gi_parts = [], []
                for k in range(S5_PAR):
                    in_r, in_i = ins[k][s]
                    pr, pi = pw_ref[d, 0, j0 + k], pw_ref[d, 1, j0 + k]
                    gr_parts.append((hr_ref[d, k, src, :] + pr * in_r - pi * in_i).astype(BF16))
                    gi_parts.append((hi_ref[d, k, src, :] + pr * in_i + pi * in_r).astype(BF16))
                gr = jnp.concatenate(gr_parts, axis=-1)
                gi = jnp.concatenate(gi_parts, axis=-1)
                acc_ref[s * seg:(s + 1) * seg, :] += (
                    jnp.dot(gr, cd_ref[d, 0, cols, :], preferred_element_type=F32)
                    + jnp.dot(gi, cd_ref[d, 1, cols, :], preferred_element_type=F32))
    y = _gelu_tanh(acc_ref[...])
    y = y * _sigmoid(jnp.dot(y.astype(BF16), gw_ref[...], preferred_element_type=F32) + gb_ref[...])
    y_ref[0, 0] = y[:, :LANES]
    y_ref[0, 1] = y[:, LANES:]


def _s5(p, h0, log_a, bd, cd, dvec, glu_w, glu_b, *, u_blk0):
    bsz, _, t, _ = p.shape
    seg, pitch = _seg_rows(t)
    la_re, la_im = log_a
    pos = jnp.arange(seg, dtype=F32)
    n = jnp.stack([pos + 1.0, seg - pos]).reshape(2, 1, seg, 1)
    mag = jnp.exp(n * la_re)
    pw = jnp.stack([mag * jnp.cos(n * la_im), mag * jnp.sin(n * la_im)], axis=1)
    a_re, a_im = pw[0, 0, :, 0:1, :], pw[0, 1, :, 0:1, :]
    a_re = jnp.stack([a_re, pw[1, 0, :, seg - 1:seg, :]])
    a_im = jnp.stack([a_im, pw[1, 1, :, seg - 1:seg, :]])
    full = lambda a: pl.BlockSpec(a.shape, lambda b: (0,) * a.ndim)
    st_spec = pl.BlockSpec((1, 2, 2, S5_NB, 1, LANES), lambda b: (b, 0, 0, 0, 0, 0))
    return pl.pallas_call(
        functools.partial(_s5_kernel, t=t),
        grid=(bsz,),
        in_specs=[pl.BlockSpec((1, 2, t, LANES), lambda b: (b, u_blk0 // 2, 0, 0)),
                  full(a_re), full(a_im), full(pw), full(bd), full(cd), full(dvec), full(glu_w), full(glu_b),
                  st_spec],
        out_specs=[pl.BlockSpec((1, 2, t, LANES), lambda b: (b, 0, 0, 0)), st_spec],
        out_shape=[jax.ShapeDtypeStruct((bsz, 2, t, LANES), F32), jax.ShapeDtypeStruct(h0.shape, F32)],
        scratch_shapes=[pltpu.VMEM((S5_PAR, N_SEG * pitch, LANES), F32),
                        pltpu.VMEM((S5_PAR, N_SEG * pitch, LANES), F32),
                        pltpu.VMEM((2, S5_PAR, N_SEG * pitch, LANES), F32),
                        pltpu.VMEM((2, S5_PAR, N_SEG * pitch, LANES), F32),
                        pltpu.VMEM((t, S5_W), F32)],
        compiler_params=_cparams(("parallel",)),
        name="s5",
    )(p, a_re, a_im, pw, bd, cd, dvec, glu_w, glu_b, h0)


def _s5_params(lam_re, lam_im, log_step, b_re, b_im, c_re, c_im):
    step = jnp.exp(log_step)[..., None]
    mag = jnp.exp(lam_re * step)
    ar, ai = mag * jnp.cos(lam_im * step), mag * jnp.sin(lam_im * step)
    den = lam_re * lam_re + lam_im * lam_im
    zr = ((ar - 1) * lam_re + ai * lam_im) / den
    zi = (ai * lam_re - (ar - 1) * lam_im) / den
    czr = c_re * zr[:, :, None, :] - c_im * zi[:, :, None, :]
    czi = c_re * zi[:, :, None, :] + c_im * zr[:, :, None, :]
    eye = jnp.eye(S5_GROUPS, dtype=F32)

    def in_mat(m):
        return jnp.einsum("gpk,gh->gkhp", m, eye).reshape(S5_W, S5_NSTATE)

    def out_mat(m):
        return jnp.einsum("dgkp,gh->dgphk", m, eye).reshape(2, S5_NSTATE, S5_W)

    bd = jnp.stack([in_mat(b_re), in_mat(b_im)], axis=0).astype(BF16)
    cd = jnp.stack([out_mat(czr), -out_mat(czi)], axis=1).astype(BF16)
    shape = (2, S5_NB, 1, LANES)
    log_a = ((lam_re * step).reshape(shape), (lam_im * step).reshape(shape))
    return log_a, bd, cd


OD_U0 = 5 * HG_HEADS


def _odd_weights(w_in, lower_bound, hg_norm_g, lam_re, lam_im, log_step, b_re, b_im, c_re, c_im, s5_d,
                 glu_w, glu_b):
    log_a, bd, cd = _s5_params(lam_re, lam_im, log_step, b_re, b_im, c_re, c_im)
    return dict(w_in=w_in.astype(BF16), lb=lower_bound, norm_g=hg_norm_g, log_a=log_a, bd=bd, cd=cd,
                dvec=s5_d.reshape(1, S5_W), glu_w=glu_w.astype(BF16), glu_b=glu_b.reshape(1, S5_W))


def _odd_scans(p, states, w):
    hg_h, s5_h = states
    o, hg_h = _hgrn(p, hg_h, w["lb"], w["norm_g"])
    y, s5_h = _s5(p, s5_h, w["log_a"], w["bd"], w["cd"], w["dvec"], w["glu_w"], w["glu_b"], u_blk0=OD_U0)
    return o, y, (hg_h, s5_h)


def _odd_zero_states(bsz):
    return (jnp.zeros((bsz, 2, HG_HEADS, LANES, LANES), F32),
            jnp.zeros((bsz, 2, 2, S5_NB, 1, LANES), F32))


PROJ_TILE = 256
MIX_TILE = 512
FFN_TILE = 1024
FFN_COLS = 256


def kernel(x, c, ctx, c_ctx, w_mod, b_mod, norm_mix_g, norm_ffn_g, final_norm_g,
           ev_w_in, ev_w_out, ssd_conv_w, ssd_conv_b, ssd_dt_bias, ssd_a_log, ssd_d, ssd_norm_g,
           lru_conv_w, lru_conv_b, lru_w_a, lru_b_a, lru_w_i, lru_b_i, lru_lam,
           od_w_in, od_w_out, hg_lb_logits, hg_norm_g,
           s5_lam_re, s5_lam_im, s5_log_step, s5_b_re, s5_b_im, s5_c_re, s5_c_im, s5_d,
           s5_glu_w, s5_glu_b,
           ffn_w_gate, ffn_w_up, ffn_conv_w, ffn_conv_b, ffn_w_down):
    bsz, _, d = x.shape
    depth = w_mod.shape[0]
    prob = jax.nn.softmax(hg_lb_logits.astype(F32), axis=0)
    lower_bounds = (jnp.cumsum(prob, axis=0) - prob[0]).astype(hg_lb_logits.dtype)

    pad = (-(bsz + 1)) % SUBLANES
    cond = jnp.concatenate([c, c_ctx[None], jnp.zeros((pad, d), c.dtype)], axis=0)
    mods = _modulation(cond, w_mod, b_mod).transpose(0, 2, 1, 3)

    for layer in range(depth):
        last = layer == depth - 1
        j = layer // 2
        mod_x = mods[layer, :bsz]
        mod_c = mods[layer, bsz:bsz + 1]
        if layer % 2 == 0:
            w = _even_weights(ev_w_in[j], ssd_conv_w[j], ssd_conv_b[j], ssd_dt_bias[j], ssd_a_log[j], ssd_d[j],
                              lru_conv_w[j], lru_conv_b[j], lru_w_a[j], lru_b_a[j], lru_w_i[j], lru_b_i[j],
                              lru_lam[j])
            scans, zero_states = _even_scans, _even_zero_states
            w_out, norm_gain, norm_first = ev_w_out[j].astype(BF16), ssd_norm_g[j], True
        else:
            w = _odd_weights(od_w_in[j], lower_bounds[layer], hg_norm_g[j], s5_lam_re[j], s5_lam_im[j],
                             s5_log_step[j], s5_b_re[j], s5_b_im[j], s5_c_re[j], s5_c_im[j], s5_d[j],
                             s5_glu_w[j], s5_glu_b[j])
            scans, zero_states = _odd_scans, _odd_zero_states
            w_out, norm_gain, norm_first = od_w_out[j].astype(BF16), hg_norm_g[j], False
        ffn_w = _ffn_weights(ffn_w_gate[layer], ffn_w_up[layer], ffn_conv_w[layer], ffn_conv_b[layer],
                             ffn_w_down[layer], FFN_COLS)

        p_c = _project(ctx, mod_c, norm_mix_g[layer], w["w_in"], PROJ_TILE)
        a_c, b_c, states = scans(p_c, zero_states(bsz), w)
        p_x = _project(x, mod_x, norm_mix_g[layer], w["w_in"], PROJ_TILE)
        a_x, b_x, _ = scans(p_x, states, w)
        x = _mix_out(x, a_x, b_x, mod_x, norm_gain, w_out, norm_first, MIX_TILE)
        x = _conv_ffn(x, mod_x, norm_ffn_g[layer], final_norm_g, *ffn_w, grid_conv=True, final_norm=last,
                      tile=FFN_TILE, tf=FFN_COLS)
        if not last:
            ctx = _mix_out(ctx, a_c, b_c, mod_c, norm_gain, w_out, norm_first, MIX_TILE)
            ctx = _conv_ffn(ctx, mod_c, norm_ffn_g[layer], final_norm_g, *ffn_w, grid_conv=False,
                            final_norm=False, tile=FFN_TILE, tf=FFN_COLS)
    return x
```

```python
import functools
import math

import jax
import jax.numpy as jnp
import numpy as np
from jax import lax
from jax.experimental import pallas as pl
from jax.experimental.pallas import tpu as pltpu

LANES = 128
RMS_EPS = 1e-6
N_MOD = 6
GRID_W = 64
SSD_HEAD_DIM = 64
SSD_HEADS = 16
SSD_GROUPS = 2
SSD_HPG = SSD_HEADS // SSD_GROUPS
SSD_STATE = 128
SSD_CHUNK = 128
LRU_BLOCKS = 8
LRU_C = 8.0
HG_HEADS = 6
S5_GROUPS = 16
S5_GROUP_CH = 16
S5_STATE = 64
VMEM_LIMIT = 56 * 1024 * 1024

BF16 = jnp.bfloat16
F32 = jnp.float32
HIGHEST = lax.Precision.HIGHEST


def _cparams(sem):
    return pltpu.CompilerParams(dimension_semantics=sem, vmem_limit_bytes=VMEM_LIMIT)


def _dot(a, b):
    return jnp.dot(a.astype(BF16), b.astype(BF16), preferred_element_type=F32)


def _dot32(a, b):
    return jnp.dot(a, b, preferred_element_type=F32, precision=HIGHEST)


def _sigmoid(v):
    return 0.5 * jnp.tanh(0.5 * v) + 0.5


def _silu(v):
    return v * _sigmoid(v)


def _gelu_tanh(v):
    return 0.5 * v * (1.0 + jnp.tanh(math.sqrt(2.0 / math.pi) * (v + 0.044715 * (v * v * v))))


def _softplus(v):
    return jnp.maximum(v, 0.0) + jnp.log(1.0 + jnp.exp(-jnp.abs(v)))


def _rms(v, g):
    return v * lax.rsqrt(jnp.mean(v * v, axis=-1, keepdims=True) + RMS_EPS) * g


def _norm_mod(xv, g, shift, scale):
    return _rms(xv, g) * (1.0 + scale) + shift


def _mod_kernel(s_ref, w_ref, b_ref, o_ref):
    o_ref[0, 0] = _dot(_silu(s_ref[...]), w_ref[0]) + b_ref[0, 0]


def _modulation(s, w_mod, b_mod):
    depth, d, _ = w_mod.shape
    rows = s.shape[0]
    return pl.pallas_call(
        _mod_kernel,
        grid=(depth, N_MOD),
        in_specs=[pl.BlockSpec((rows, d), lambda l, j: (0, 0)),
                  pl.BlockSpec((1, d, d), lambda l, j: (l, 0, j)),
                  pl.BlockSpec((1, 1, 1, d), lambda l, j: (l, j, 0, 0))],
        out_specs=pl.BlockSpec((1, 1, rows, d), lambda l, j: (l, j, 0, 0)),
        out_shape=jax.ShapeDtypeStruct((depth, N_MOD, rows, d), F32),
        compiler_params=_cparams(("arbitrary", "arbitrary")),
        name="modulation",
    )(s, w_mod, b_mod.reshape(depth, N_MOD, 1, d))


def _proj_kernel(x_ref, mod_ref, g_ref, w_ref, o_ref, *, nblk):
    m = mod_ref[0]
    h = _norm_mod(x_ref[0], g_ref[...], m[0:1], m[1:2]).astype(BF16)
    group = 4
    for b0 in range(0, nblk, group):
        nb = min(group, nblk - b0)
        r = jnp.dot(h, w_ref[:, b0 * LANES:(b0 + nb) * LANES], preferred_element_type=F32)
        for k in range(nb):
            o_ref[0, b0 + k] = r[:, k * LANES:(k + 1) * LANES]


def _project(x, mod, gain, w, tile):
    bsz, t, d = x.shape
    nblk = w.shape[1] // LANES
    tile = min(tile, t)
    mod_map = (lambda b, i: (b, 0, 0)) if mod.shape[0] == bsz else (lambda b, i: (0, 0, 0))
    return pl.pallas_call(
        functools.partial(_proj_kernel, nblk=nblk),
        grid=(bsz, t // tile),
        in_specs=[pl.BlockSpec((1, tile, d), lambda b, i: (b, i, 0)),
                  pl.BlockSpec((1, N_MOD, d), mod_map),
                  pl.BlockSpec((1, d), lambda b, i: (0, 0)),
                  pl.BlockSpec((d, nblk * LANES), lambda b, i: (0, 0), pipeline_mode=pl.Buffered(1))],
        out_specs=pl.BlockSpec((1, nblk, tile, LANES), lambda b, i: (b, 0, i, 0)),
        out_shape=jax.ShapeDtypeStruct((bsz, nblk, t, LANES), F32),
        compiler_params=_cparams(("parallel", "parallel")),
        name="project",
    )(x, mod, gain.reshape(1, d), w)


def _mix_out_kernel(x_ref, ma_ref, mb_ref, mod_ref, ng_ref, w_ref, o_ref, *, norm_first):
    m = mod_ref[0]
    pa = [ma_ref[0, k] for k in range(ma_ref.shape[1])]
    pb = [mb_ref[0, k].astype(BF16) for k in range(mb_ref.shape[1])]
    if norm_first:
        pa = [_rms(jnp.concatenate(pa, axis=-1), ng_ref[...]).astype(BF16)]
    else:
        pa = [p.astype(BF16) for p in pa]
    v = jnp.concatenate(pa + pb, axis=-1)
    o_ref[0] = x_ref[0] + m[2:3] * jnp.dot(v, w_ref[...], preferred_element_type=F32)


def _mix_out(x, mix_a, mix_b, mod, norm_gain, w, norm_first, tile):
    bsz, t, d = x.shape
    tile = min(tile, t)
    mod_map = (lambda b, i: (b, 0, 0)) if mod.shape[0] == bsz else (lambda b, i: (0, 0, 0))
    ng = norm_gain.reshape(1, -1)
    mix_spec = lambda a: pl.BlockSpec((1, a.shape[1], tile, LANES), lambda b, i: (b, 0, i, 0))
    return pl.pallas_call(
        functools.partial(_mix_out_kernel, norm_first=norm_first),
        grid=(bsz, t // tile),
        in_specs=[pl.BlockSpec((1, tile, d), lambda b, i: (b, i, 0)),
                  mix_spec(mix_a), mix_spec(mix_b),
                  pl.BlockSpec((1, N_MOD, d), mod_map),
                  pl.BlockSpec(ng.shape, lambda b, i: (0, 0)),
                  pl.BlockSpec(w.shape, lambda b, i: (0, 0), pipeline_mode=pl.Buffered(1))],
        out_specs=pl.BlockSpec((1, tile, d), lambda b, i: (b, i, 0)),
        out_shape=jax.ShapeDtypeStruct((bsz, t, d), F32),
        compiler_params=_cparams(("parallel", "parallel")),
        name="mix_out",
    )(x, mix_a, mix_b, mod, ng, w)


def _ffn_kernel(x_ref, xp_ref, xn_ref, mod_ref, g_ref, fg_ref, wg_ref, wu_ref, cw_ref, cb_ref, wd_ref,
                o_ref, fx_ref, gt_ref, a0_ref, a1_ref, *, tile, halo, tf, grid_conv, final_norm):
    i = pl.program_id(1)
    nt = pl.num_programs(1)
    nf = wg_ref.shape[0]
    m = mod_ref[0]
    fx_ref[halo:halo + tile] = _norm_mod(x_ref[0], g_ref[...], m[3:4], m[4:5]).astype(BF16)
    if halo:
        keep_p = jnp.where(i > 0, 1.0, 0.0)
        keep_n = jnp.where(i < nt - 1, 1.0, 0.0)
        fx_ref[0:halo] = (keep_p * _norm_mod(xp_ref[0], g_ref[...], m[3:4], m[4:5])).astype(BF16)
        fx_ref[halo + tile:] = (keep_n * _norm_mod(xn_ref[0], g_ref[...], m[3:4], m[4:5])).astype(BF16)

    def gate(f, dst_ref):
        dst_ref[...] = jnp.dot(fx_ref[...], wg_ref[f], preferred_element_type=F32)

    def column(f, src_ref):
        up = jnp.dot(fx_ref[halo:halo + tile], wu_ref[f], preferred_element_type=F32)
        a = src_ref[...]
        cw = cw_ref[f]
        rows = a.shape[0]
        pos = lax.broadcasted_iota(jnp.int32, a.shape, 0)
        if grid_conv:
            col = pos % GRID_W
            a_m1 = jnp.where(col == 0, 0.0, pltpu.roll(a, 1, axis=0))
            a_p1 = jnp.where(col == GRID_W - 1, 0.0, pltpu.roll(a, rows - 1, axis=0))
            conv = cb_ref[f]
            for dr in range(3):
                lo = dr * GRID_W
                conv = conv + (cw[3 * dr + 0:3 * dr + 1] * a_m1[lo:lo + tile]
                               + cw[3 * dr + 1:3 * dr + 2] * a[lo:lo + tile]
                               + cw[3 * dr + 2:3 * dr + 3] * a_p1[lo:lo + tile])
        else:
            a_m1 = jnp.where(pos == 0, 0.0, pltpu.roll(a, 1, axis=0))
            a_p1 = jnp.where(pos == rows - 1, 0.0, pltpu.roll(a, rows - 1, axis=0))
            conv = cb_ref[f] + cw[3:4] * a_m1 + cw[4:5] * a + cw[5:6] * a_p1
        start = f * tf if isinstance(f, int) else pl.multiple_of(f * tf, tf)
        gt_ref[:, pl.ds(start, tf)] = (_silu(conv) * up).astype(BF16)

    def column_pair(p, carry):
        f = 2 * p
        gate(f + 1, a1_ref)
        column(f, a0_ref)
        gate(f + 2, a0_ref)
        column(f + 1, a1_ref)
        return carry

    gate(0, a0_ref)
    pairs = (nf - 1) // 2
    lax.fori_loop(0, pairs, column_pair, 0)
    if nf - 2 * pairs == 2:
        gate(nf - 1, a1_ref)
        column(nf - 2, a0_ref)
        column(nf - 1, a1_ref)
    else:
        column(nf - 1, a0_ref)
    y = x_ref[0] + m[5:6] * jnp.dot(gt_ref[...], wd_ref[...], preferred_element_type=F32)
    if final_norm:
        y = _rms(y, fg_ref[...])
    o_ref[0] = y


def _conv_ffn(x, mod, gain, final_gain, w_gate, w_up, conv_w, conv_b, w_down, grid_conv, final_norm,
              tile, tf):
    bsz, t, d = x.shape
    nf = w_gate.shape[0]
    dff = nf * tf
    tile = min(tile, t)
    halo = GRID_W if grid_conv else 0
    nh = t // GRID_W
    per = tile // GRID_W
    mod_map = (lambda b, i: (b, 0, 0)) if mod.shape[0] == bsz else (lambda b, i: (0, 0, 0))
    resident = lambda a: pl.BlockSpec(a.shape, lambda b, i: (0,) * a.ndim, pipeline_mode=pl.Buffered(1))
    return pl.pallas_call(
        functools.partial(_ffn_kernel, tile=tile, halo=halo, tf=tf, grid_conv=grid_conv, final_norm=final_norm),
        grid=(bsz, t // tile),
        in_specs=[pl.BlockSpec((1, tile, d), lambda b, i: (b, i, 0)),
                  pl.BlockSpec((1, GRID_W, d), lambda b, i: (b, jnp.maximum(i * per - 1, 0), 0)),
                  pl.BlockSpec((1, GRID_W, d), lambda b, i: (b, jnp.minimum((i + 1) * per, nh - 1), 0)),
                  pl.BlockSpec((1, N_MOD, d), mod_map),
                  pl.BlockSpec((1, d), lambda b, i: (0, 0)),
                  pl.BlockSpec((1, d), lambda b, i: (0, 0)),
                  resident(w_gate), resident(w_up), resident(conv_w), resident(conv_b), resident(w_down)],
        out_specs=pl.BlockSpec((1, tile, d), lambda b, i: (b, i, 0)),
        out_shape=jax.ShapeDtypeStruct((bsz, t, d), F32),
        scratch_shapes=[pltpu.VMEM((tile + 2 * halo, d), BF16), pltpu.VMEM((tile, dff), BF16),
                        pltpu.VMEM((tile + 2 * halo, tf), F32), pltpu.VMEM((tile + 2 * halo, tf), F32)],
        compiler_params=_cparams(("parallel", "parallel")),
        name="conv_ffn",
    )(x, x, x, mod, gain.reshape(1, d), final_gain.reshape(1, d), w_gate, w_up, conv_w, conv_b, w_down)


def _ffn_weights(w_gate, w_up, conv_w, conv_b, w_down, tf):
    d, dff = w_gate.shape
    nf = dff // tf
    tiles = lambda w: w.reshape(w.shape[0], nf, tf).transpose(1, 0, 2)
    return (tiles(w_gate).astype(BF16), tiles(w_up).astype(BF16), tiles(conv_w.reshape(9, dff)),
            tiles(conv_b.reshape(1, dff)), w_down.astype(BF16))


def _conv4_silu(v, w, b):
    t = v.shape[0]
    pos = lax.broadcasted_iota(jnp.int32, v.shape, 0)
    acc = b + w[1:2] * v
    acc = acc + w[0:1] * jnp.where(pos < 1, 0.0, pltpu.roll(v, 1, axis=0))
    acc = acc + w[2:3] * jnp.where(pos >= t - 1, 0.0, pltpu.roll(v, t - 1, axis=0))
    acc = acc + w[3:4] * jnp.where(pos >= t - 2, 0.0, pltpu.roll(v, t - 2, axis=0))
    return acc


def _ssd_kernel(xr_ref, br_ref, cr_ref, dt_ref, z_ref, cwx_ref, cwb_ref, cwc_ref, cbx_ref, cbb_ref, cbc_ref,
                dtb_ref, alog_ref, dvec_ref, h0_ref, y_ref, ht_ref, xs_ref, bs_ref, cs_ref, s_ref, *, t):
    nblk = 4
    nchunk = t // SSD_CHUNK
    for k in range(nblk):
        xs_ref[k] = _silu(_conv4_silu(xr_ref[0, k], cwx_ref[k], cbx_ref[k]))
    bs_ref[...] = _silu(_conv4_silu(br_ref[0, 0], cwb_ref[0], cbb_ref[0]))
    cs_ref[...] = _silu(_conv4_silu(cr_ref[0, 0], cwc_ref[0], cbc_ref[0]))

    dtb = dtb_ref[0]
    a_neg = -jnp.exp(alog_ref[0])
    li = lax.broadcasted_iota(jnp.int32, (SSD_CHUNK, SSD_CHUNK), 0)
    si = lax.broadcasted_iota(jnp.int32, (SSD_CHUNK, SSD_CHUNK), 1)
    lane = lax.broadcasted_iota(jnp.int32, (1, LANES), 1)
    lo_half = lane < SSD_HEAD_DIM

    valid = [li >= si, li <= si]
    tri = [v.astype(F32) for v in valid]
    for d in (0, 1):
        for k in range(nblk):
            s_ref[d, k] = h0_ref[0, d, 0, k]

    def pair(i, carry, finish):
        jobs = [(0, pl.ds(pl.multiple_of(i * SSD_CHUNK, SSD_CHUNK), SSD_CHUNK)),
                (1, pl.ds(pl.multiple_of((nchunk - 1 - i) * SSD_CHUNK, SSD_CHUNK), SSD_CHUNK))]
        pre = []
        for d, rows in jobs:
            bm = bs_ref[rows, :]
            cm = cs_ref[rows, :]
            dt = _softplus(dt_ref[0, 0, rows, :] + dtb)
            cum = _dot32(tri[d], dt * a_neg)
            cb = lax.dot_general(cm.astype(BF16), bm.astype(BF16), (((1,), (1,)), ((), ())),
                                 preferred_element_type=F32)
            pre.append((dt, cum, cum.T, dt.T, cb, bm.T.astype(BF16), cm.astype(BF16)))
        for k in range(nblk):
            for (d, rows), (dt, cum, cum_t, dt_t, cb, bm_t, cm16) in zip(jobs, pre):
                end_row = SSD_CHUNK - 1 if d == 0 else 0
                xk = xs_ref[k, rows, :]
                ms = []
                for e in range(2):
                    idx = d * SSD_HPG + 2 * k + e
                    seg = jnp.where(valid[d], cum[:, idx:idx + 1] - cum_t[idx:idx + 1, :], -jnp.inf)
                    ms.append((cb * jnp.exp(seg) * dt_t[idx:idx + 1, :]).astype(BF16))
                x_lo = jnp.where(lo_half, xk, 0.0).astype(BF16)
                x_hi = jnp.where(lo_half, 0.0, xk).astype(BF16)
                yk = (jnp.dot(ms[0], x_lo, preferred_element_type=F32)
                      + jnp.dot(ms[1], x_hi, preferred_element_type=F32))
                i0 = d * SSD_HPG + 2 * k
                ecol = jnp.where(lo_half, cum[:, i0:i0 + 1], cum[:, i0 + 1:i0 + 2])
                dcol = jnp.where(lo_half, dt[:, i0:i0 + 1], dt[:, i0 + 1:i0 + 2])
                tot = jnp.where(lo_half, cum[end_row:end_row + 1, i0:i0 + 1],
                                cum[end_row:end_row + 1, i0 + 1:i0 + 2])
                sk = s_ref[d, k]
                yk = yk + jnp.dot(cm16, sk.astype(BF16), preferred_element_type=F32) * jnp.exp(ecol)
                xw = (xk * dcol * jnp.exp(tot - ecol)).astype(BF16)
                s_ref[d, k] = sk * jnp.exp(tot) + jnp.dot(bm_t, xw, preferred_element_type=F32)
                if finish:
                    ytot = y_ref[0, k, rows, :] + yk + dvec_ref[k] * xk
                    y_ref[0, k, rows, :] = ytot * _silu(z_ref[0, k, rows, :])
                else:
                    y_ref[0, k, rows, :] = yk
        return carry

    lax.fori_loop(0, nchunk // 2, functools.partial(pair, finish=False), 0)
    lax.fori_loop(nchunk // 2, nchunk, functools.partial(pair, finish=True), 0)
    for d in (0, 1):
        for k in range(nblk):
            ht_ref[0, d, 0, k] = s_ref[d, k]


def _ssd(p, h0, conv_w, conv_b, dtb, alog, dvec, *, z_blk0, xbc_blk0, dt_blk0):
    bsz, _, t, _ = p.shape
    assert (t // SSD_CHUNK) % 2 == 0
    g4 = lambda off: (lambda b, g: (b, off // 4 + g, 0, 0))
    g1 = lambda off: (lambda b, g: (b, off + g, 0, 0))
    big = pl.BlockSpec((1, 4, t, LANES), g4(xbc_blk0))
    one = lambda off: pl.BlockSpec((1, 1, t, LANES), g1(off))
    st_spec = pl.BlockSpec((1, 2, 1, 4, SSD_STATE, LANES), lambda b, g: (b, 0, g, 0, 0, 0))
    return pl.pallas_call(
        functools.partial(_ssd_kernel, t=t),
        grid=(bsz, SSD_GROUPS),
        in_specs=[big, one(xbc_blk0 + 8), one(xbc_blk0 + 10), one(dt_blk0),
                  pl.BlockSpec((1, 4, t, LANES), g4(z_blk0)),
                  pl.BlockSpec((4, 4, LANES), lambda b, g: (g, 0, 0)),
                  pl.BlockSpec((1, 4, LANES), lambda b, g: (8 + g, 0, 0)),
                  pl.BlockSpec((1, 4, LANES), lambda b, g: (10 + g, 0, 0)),
                  pl.BlockSpec((4, 1, LANES), lambda b, g: (g, 0, 0)),
                  pl.BlockSpec((1, 1, LANES), lambda b, g: (8 + g, 0, 0)),
                  pl.BlockSpec((1, 1, LANES), lambda b, g: (10 + g, 0, 0)),
                  pl.BlockSpec((1, 1, LANES), lambda b, g: (g, 0, 0)),
                  pl.BlockSpec((1, 1, LANES), lambda b, g: (g, 0, 0)),
                  pl.BlockSpec((4, 1, LANES), lambda b, g: (g, 0, 0)),
                  st_spec],
        out_specs=[pl.BlockSpec((1, 4, t, LANES), lambda b, g: (b, g, 0, 0)), st_spec],
        out_shape=[jax.ShapeDtypeStruct((bsz, 8, t, LANES), F32),
                   jax.ShapeDtypeStruct(h0.shape, F32)],
        scratch_shapes=[pltpu.VMEM((4, t, LANES), F32), pltpu.VMEM((t, LANES), F32),
                        pltpu.VMEM((t, LANES), F32), pltpu.VMEM((2, 4, SSD_STATE, LANES), F32)],
        compiler_params=_cparams(("parallel", "parallel")),
        name="ssd",
    )(p, p, p, p, p, conv_w, conv_w, conv_w, conv_b, conv_b, conv_b, dtb, alog, dvec, h0)


LRU_NB = 2
N_SEG = 8
SCAN_UNROLL = 8


def _seg_rows(t):
    seg = t // N_SEG
    return seg, seg + 8


def _conv4(v, w, b):
    return _conv4_silu(v, w, b)


def _seg_scan(a_ref, b_ref, acc_ref, h_ref, lead, seg, pitch, reverse):
    def step(i, carry):
        out = []
        for (h, acc), ld, rev in zip(carry, lead, reverse):
            j = (seg - 1 - i) if rev else i
            idx = ld + (pl.ds(j, N_SEG, stride=pitch), slice(None))
            a = a_ref[idx]
            h = a * h + b_ref[idx]
            acc = acc * a
            acc_ref[idx] = acc
            h_ref[idx] = h
            out.append((h, acc))
        return tuple(out)

    init = (jnp.zeros((N_SEG, LANES), F32), jnp.ones((N_SEG, LANES), F32))
    return lax.fori_loop(0, seg, step, tuple(init for _ in lead), unroll=SCAN_UNROLL)


def _seg_inputs(h0, end, tot, reverse):
    rows = [None] * N_SEG
    hin = h0
    for s in (range(N_SEG - 1, -1, -1) if reverse else range(N_SEG)):
        rows[s] = hin
        hin = tot[s:s + 1] * hin + end[s:s + 1]
    return rows, hin


def _lru_kernel(u_ref, gy_ref, cw_ref, cb_ref, wa_ref, wi_ref, ba_ref, bi_ref, lam_ref, h0_ref,
                r_ref, ht_ref, uc_ref, a_ref, b_ref, acc_ref, hl_ref, *, t):
    seg, pitch = _seg_rows(t)
    for k in range(LRU_NB):
        uc_ref[k] = _conv4(u_ref[0, k], cw_ref[k], cb_ref[k])
    for d in (1, 0):
        for k in range(LRU_NB):
            decay_rate = -LRU_C * _softplus(-lam_ref[d, k])
            for s in range(N_SEG):
                u = uc_ref[k, s * seg:(s + 1) * seg, :]
                u16 = u.astype(BF16)
                r = _sigmoid(jnp.dot(u16, wa_ref[d, k], preferred_element_type=F32) + ba_ref[d, k])
                i = _sigmoid(jnp.dot(u16, wi_ref[d, k], preferred_element_type=F32) + bi_ref[d, k])
                a = jnp.exp(decay_rate * r)
                a_ref[d, k, s * pitch:s * pitch + seg, :] = a
                b_ref[d, k, s * pitch:s * pitch + seg, :] = jnp.sqrt((1.0 - a) * (1.0 + a)) * (i * u)
    chains = [(d, k) for d in (1, 0) for k in range(LRU_NB)]
    scanned = _seg_scan(a_ref, b_ref, acc_ref, hl_ref, chains, seg, pitch, [d == 1 for d, _ in chains])
    for (d, k), (end, tot) in zip(chains, scanned):
        rows, hfin = _seg_inputs(h0_ref[0, d, k], end, tot, reverse=(d == 1))
        ht_ref[0, d, k] = hfin
        for s in range(N_SEG):
            src = slice(s * pitch, s * pitch + seg)
            dst = slice(s * seg, (s + 1) * seg)
            h = hl_ref[d, k, src, :] + acc_ref[d, k, src, :] * rows[s]
            if d == 1:
                r_ref[0, k, dst, :] = h
            else:
                r_ref[0, k, dst, :] = (r_ref[0, k, dst, :] + h) * _gelu_tanh(gy_ref[0, k, dst, :])


def _lru(p, h0, conv_w, conv_b, wa, wi, ba, bi, lam, *, gy_blk0, u_blk0):
    bsz, _, t, _ = p.shape
    nb = LRU_NB
    seg, pitch = _seg_rows(t)
    blk = lambda off: pl.BlockSpec((1, nb, t, LANES), lambda b, g: (b, off // nb + g, 0, 0))
    par = lambda shape: pl.BlockSpec(shape, lambda b, g: (0, g) + (0,) * (len(shape) - 2))
    st_spec = pl.BlockSpec((1, 2, nb, 1, LANES), lambda b, g: (b, 0, g, 0, 0))
    return pl.pallas_call(
        functools.partial(_lru_kernel, t=t),
        grid=(bsz, LRU_BLOCKS // nb),
        in_specs=[blk(u_blk0), blk(gy_blk0),
                  pl.BlockSpec((nb, 4, LANES), lambda b, g: (g, 0, 0)),
                  pl.BlockSpec((nb, 1, LANES), lambda b, g: (g, 0, 0)),
                  par((2, nb, LANES, LANES)), par((2, nb, LANES, LANES)),
                  par((2, nb, 1, LANES)), par((2, nb, 1, LANES)), par((2, nb, 1, LANES)),
                  st_spec],
        out_specs=[pl.BlockSpec((1, nb, t, LANES), lambda b, g: (b, g, 0, 0)), st_spec],
        out_shape=[jax.ShapeDtypeStruct((bsz, LRU_BLOCKS, t, LANES), F32),
                   jax.ShapeDtypeStruct(h0.shape, F32)],
        scratch_shapes=[pltpu.VMEM((nb, t, LANES), F32)] + [pltpu.VMEM((2, nb, N_SEG * pitch, LANES), F32)] * 4,
        compiler_params=_cparams(("parallel", "parallel")),
        name="rglru",
    )(p, p, conv_w, conv_b, wa, wi, ba, bi, lam, h0)


EV_Z0, EV_XBC0, EV_DT0, EV_GY0, EV_U0, EV_NBLK = 0, 8, 20, 22, 30, 38


def _blocks(v, n):
    return v.reshape(n, 1, LANES)


def _per_group_heads(v):
    v = v.reshape(2, SSD_GROUPS, SSD_HPG).transpose(1, 0, 2).reshape(SSD_GROUPS, 2 * SSD_HPG)
    return jnp.pad(v, ((0, 0), (0, LANES - 2 * SSD_HPG))).reshape(SSD_GROUPS, 1, LANES)


def _even_weights(w_in, conv_w, conv_b, dt_bias, a_log, ssd_d, lru_conv_w, lru_conv_b, w_a, b_a, w_i, b_i, lam):
    d = w_in.shape[0]
    z, xbc, dt, gy, u = jnp.split(w_in, (1024, 2560, 2592, 3616), axis=1)
    dt = dt.reshape(d, 2, SSD_GROUPS, SSD_HPG).transpose(0, 2, 1, 3).reshape(d, SSD_GROUPS, 2 * SSD_HPG)
    dt = jnp.pad(dt, ((0, 0), (0, 0), (0, LANES - 2 * SSD_HPG))).reshape(d, SSD_GROUPS * LANES)
    return dict(
        w_in=jnp.concatenate([z, xbc, dt, gy, u], axis=1).astype(BF16),
        conv_w=conv_w.reshape(4, 12, LANES).transpose(1, 0, 2), conv_b=_blocks(conv_b, 12),
        dtb=_per_group_heads(dt_bias), alog=_per_group_heads(a_log),
        dvec=_blocks(jnp.repeat(ssd_d, SSD_HEAD_DIM), 8),
        lru_conv_w=lru_conv_w.reshape(4, LRU_BLOCKS, LANES).transpose(1, 0, 2),
        lru_conv_b=_blocks(lru_conv_b, LRU_BLOCKS),
        w_a=w_a.astype(BF16), w_i=w_i.astype(BF16),
        b_a=b_a.reshape(2, LRU_BLOCKS, 1, LANES), b_i=b_i.reshape(2, LRU_BLOCKS, 1, LANES),
        lam=lam.reshape(2, LRU_BLOCKS, 1, LANES))


def _even_scans(p, states, w):
    ssd_h, lru_h = states
    y, ssd_h = _ssd(p, ssd_h, w["conv_w"], w["conv_b"], w["dtb"], w["alog"], w["dvec"],
                    z_blk0=EV_Z0, xbc_blk0=EV_XBC0, dt_blk0=EV_DT0)
    r, lru_h = _lru(p, lru_h, w["lru_conv_w"], w["lru_conv_b"], w["w_a"], w["w_i"], w["b_a"], w["b_i"],
                    w["lam"], gy_blk0=EV_GY0, u_blk0=EV_U0)
    return y, r, (ssd_h, lru_h)


def _even_zero_states(bsz):
    return (jnp.zeros((bsz, 2, SSD_GROUPS, 4, SSD_STATE, LANES), F32),
            jnp.zeros((bsz, 2, LRU_BLOCKS, 1, LANES), F32))


HG_CHUNK = 128
HG_PAR = 3
SUBLANES = 8


def _group_boundary(cum, c, reverse):
    m = c // 2
    off = m if reverse else m - 1
    n = cum.shape[0]
    if c >= 2 * SUBLANES:
        r = cum.reshape(n // c, c, LANES)
        return jnp.broadcast_to(r[:, off:off + 1, :], r.shape).reshape(n, LANES)
    r = cum.reshape(n // SUBLANES, SUBLANES, LANES)
    sub = lax.broadcasted_iota(jnp.int32, r.shape, 1)
    p = None
    for g0 in range(0, SUBLANES, c):
        cand = jnp.broadcast_to(r[:, g0 + off:g0 + off + 1, :], r.shape)
        p = cand if p is None else jnp.where(sub >= g0, cand, p)
    return p.reshape(n, LANES)


def _hgrn_masks():
    l = np.arange(HG_CHUNK)[:, None]
    s = np.arange(HG_CHUNK)[None, :]
    fwd = []
    size = HG_CHUNK
    while size >= 2:
        half = size // 2
        fwd.append((l // size == s // size) & (l % size >= half) & (s % size < half))
        size = half
    fwd = np.stack(fwd).astype(np.float32)
    pairs = np.stack([fwd, fwd.transpose(0, 2, 1)])
    tri = np.stack([l >= s, l <= s]).astype(np.float32)
    return jnp.asarray(pairs), jnp.asarray(tri)


def _hgrn_chunks(jobs, q_ref, ff_ref, fb_ref, v_ref, lb_ref, sf_ref, sb_ref, pairs_ref, tri_ref):
    n = len(jobs)
    qq, kk, vv, cum = [], [], [], []
    for hd, reverse, rows in jobs:
        lb = lb_ref[hd]
        fx = (fb_ref if reverse else ff_ref)[0, hd, rows, :]
        e = jnp.exp(-jnp.abs(fx))
        big = 1.0 / (1.0 + e)
        small = e * big
        pos = fx >= 0.0
        log2_f = jnp.log2(lb + (1.0 - lb) * jnp.where(pos, big, small))
        kk.append((1.0 - lb) * jnp.where(pos, small, big))
        cum.append(_dot32(tri_ref[1 if reverse else 0], log2_f))
    for hd, reverse, rows in jobs:
        qq.append(_silu(q_ref[0, hd, rows, :]))
        vv.append(v_ref[0, hd, rows, :])
    att = [None] * n
    q16 = [v.astype(BF16) for v in qq]
    k16 = [v.astype(BF16) for v in kk]
    size = HG_CHUNK
    level = 0
    while size >= 2:
        for j, (hd, reverse, rows) in enumerate(jobs):
            fac = jnp.exp2(-jnp.abs(cum[j] - _group_boundary(cum[j], size, reverse))).astype(BF16)
            a_l = lax.dot_general(q16[j] * fac, k16[j] * fac, (((1,), (1,)), ((), ())),
                                  preferred_element_type=F32)
            a_l = a_l * pairs_ref[1 if reverse else 0, level]
            att[j] = a_l if att[j] is None else att[j] + a_l
        size //= 2
        level += 1
    outs = []
    for j, (hd, reverse, rows) in enumerate(jobs):
        s_ref = (sb_ref if reverse else sf_ref).at[hd]
        sv = s_ref[...]
        diag = jnp.sum(qq[j] * kk[j], axis=-1, keepdims=True)
        o = jnp.dot(att[j].astype(BF16), vv[j].astype(BF16), preferred_element_type=F32) + diag * vv[j]
        o = o + jnp.dot((qq[j] * jnp.exp2(cum[j])).astype(BF16), sv.astype(BF16), preferred_element_type=F32)
        end_row = 0 if reverse else HG_CHUNK - 1
        cum_end = cum[j][end_row:end_row + 1, :]
        kw = (kk[j] * jnp.exp2(cum_end - cum[j])).T.astype(BF16)
        keep = jnp.broadcast_to(jnp.exp2(cum_end), (HG_CHUNK, LANES)).T
        s_ref[...] = sv * keep + jnp.dot(kw, vv[j].astype(BF16), preferred_element_type=F32)
        outs.append(o)
    return outs


def _hgrn_kernel(q_ref, ff_ref, fb_ref, v_ref, g_ref, lb_ref, ng_ref, pairs_ref, tri_ref, h0_ref,
                 o_ref, ht_ref, sf_ref, sb_ref, *, t):
    nchunk = t // HG_CHUNK
    for hd in range(HG_PAR):
        sf_ref[hd] = h0_ref[0, 0, hd]
        sb_ref[hd] = h0_ref[0, 1, hd]

    def pair(i, carry, finish):
        jobs = []
        for hd in range(HG_PAR):
            for reverse in (False, True):
                c = (nchunk - 1 - i) if reverse else i
                jobs.append((hd, reverse, pl.ds(pl.multiple_of(c * HG_CHUNK, HG_CHUNK), HG_CHUNK)))
        outs = _hgrn_chunks(jobs, q_ref, ff_ref, fb_ref, v_ref, lb_ref, sf_ref, sb_ref, pairs_ref, tri_ref)
        for (hd, reverse, rows), o in zip(jobs, outs):
            if finish:
                o = o_ref[0, hd, rows, :] + o
                o = _rms(o, ng_ref[hd]) * _silu(g_ref[0, hd, rows, :])
            o_ref[0, hd, rows, :] = o
        return carry

    lax.fori_loop(0, nchunk // 2, functools.partial(pair, finish=False), 0)
    lax.fori_loop(nchunk // 2, nchunk, functools.partial(pair, finish=True), 0)
    for hd in range(HG_PAR):
        ht_ref[0, 0, hd] = sf_ref[hd]
        ht_ref[0, 1, hd] = sb_ref[hd]


def _hgrn(p, h0, lb, norm_g):
    bsz, _, t, _ = p.shape
    assert (t // HG_CHUNK) % 2 == 0
    hp = HG_PAR
    blk = lambda off: pl.BlockSpec((1, hp, t, LANES), lambda b, h: (b, off // hp + h, 0, 0))
    par = pl.BlockSpec((hp, 1, LANES), lambda b, h: (h, 0, 0))
    st_spec = pl.BlockSpec((1, 2, hp, LANES, LANES), lambda b, h: (b, 0, h, 0, 0))
    pairs, tri = _hgrn_masks()
    return pl.pallas_call(
        functools.partial(_hgrn_kernel, t=t),
        grid=(bsz, HG_HEADS // hp),
        in_specs=[blk(0), blk(HG_HEADS), blk(2 * HG_HEADS), blk(3 * HG_HEADS), blk(4 * HG_HEADS),
                  par, par,
                  pl.BlockSpec(pairs.shape, lambda b, h: (0, 0, 0, 0)),
                  pl.BlockSpec(tri.shape, lambda b, h: (0, 0, 0)),
                  st_spec],
        out_specs=[pl.BlockSpec((1, hp, t, LANES), lambda b, h: (b, h, 0, 0)), st_spec],
        out_shape=[jax.ShapeDtypeStruct((bsz, HG_HEADS, t, LANES), F32),
                   jax.ShapeDtypeStruct(h0.shape, F32)],
        scratch_shapes=[pltpu.VMEM((hp, LANES, LANES), F32), pltpu.VMEM((hp, LANES, LANES), F32)],
        compiler_params=_cparams(("parallel", "parallel")),
        name="hgrn2",
    )(p, p, p, p, p, lb.reshape(HG_HEADS, 1, LANES), norm_g.reshape(HG_HEADS, 1, LANES), pairs, tri, h0)


S5_W = S5_GROUPS * S5_GROUP_CH
S5_NSTATE = S5_GROUPS * S5_STATE
S5_NB = S5_NSTATE // LANES
S5_PAR = 2


def _s5_kernel(u_ref, are_ref, aim_ref, pw_ref, bd_ref, cd_ref, dvec_ref, gw_ref, gb_ref, h0_ref,
               y_ref, ht_ref, wr_ref, wi_ref, hr_ref, hi_ref, acc_ref, *, t):
    seg, pitch = _seg_rows(t)
    u = jnp.concatenate([u_ref[0, 0], u_ref[0, 1]], axis=-1)
    u16 = u.astype(BF16)
    acc_ref[...] = dvec_ref[...] * u
    chains = [(d, k) for d in (0, 1) for k in range(S5_PAR)]
    for j0 in range(0, S5_NB, S5_PAR):
        cols = slice(j0 * LANES, (j0 + S5_PAR) * LANES)
        for s in range(N_SEG):
            us = u16[s * seg:(s + 1) * seg]
            wr = jnp.dot(us, bd_ref[0, :, cols], preferred_element_type=F32)
            wi = jnp.dot(us, bd_ref[1, :, cols], preferred_element_type=F32)
            for k in range(S5_PAR):
                wr_ref[k, s * pitch:s * pitch + seg, :] = wr[:, k * LANES:(k + 1) * LANES]
                wi_ref[k, s * pitch:s * pitch + seg, :] = wi[:, k * LANES:(k + 1) * LANES]
        ar = [jnp.broadcast_to(are_ref[d, j0 + k], (N_SEG, LANES)) for d, k in chains]
        ai = [jnp.broadcast_to(aim_ref[d, j0 + k], (N_SEG, LANES)) for d, k in chains]

        def step(i, carry, ar=ar, ai=ai):
            out = []
            for c, (d, k) in enumerate(chains):
                gr, gi = carry[c]
                j = (seg - 1 - i) if d == 1 else i
                src = (k, pl.ds(j, N_SEG, stride=pitch), slice(None))
                dst = (d, k, pl.ds(j, N_SEG, stride=pitch), slice(None))
                ngr = ar[c] * gr - ai[c] * gi + wr_ref[src]
                ngi = ar[c] * gi + ai[c] * gr + wi_ref[src]
                hr_ref[dst] = ngr
                hi_ref[dst] = ngi
                out.append((ngr, ngi))
            return tuple(out)

        zero = jnp.zeros((N_SEG, LANES), F32)
        fin = lax.fori_loop(0, seg, step, tuple((zero, zero) for _ in chains), unroll=SCAN_UNROLL)
        for d in (0, 1):
            ins = []
            last = 0 if d == 1 else seg - 1
            for k in range(S5_PAR):
                end_r, end_i = fin[d * S5_PAR + k]
                tot_r = pw_ref[d, 0, j0 + k, last:last + 1, :]
                tot_i = pw_ref[d, 1, j0 + k, last:last + 1, :]
                hin_r, hin_i = h0_ref[0, d, 0, j0 + k], h0_ref[0, d, 1, j0 + k]
                rows = [None] * N_SEG
                for s in (range(N_SEG - 1, -1, -1) if d == 1 else range(N_SEG)):
                    rows[s] = (hin_r, hin_i)
                    hin_r, hin_i = (tot_r[0:1] * hin_r - tot_i[0:1] * hin_i + end_r[s:s + 1],
                                    tot_r[0:1] * hin_i + tot_i[0:1] * hin_r + end_i[s:s + 1])
                ht_ref[0, d, 0, j0 + k] = hin_r
                ht_ref[0, d, 1, j0 + k] = hin_i
                ins.append(rows)
            for s in range(N_SEG):
                src = slice(s * pitch, s * pitch + seg)
                gr_parts, gi_parts = [], []
                for k in range(S5_PAR):
                    in_r, in_i = ins[k][s]
                    pr, pi = pw_ref[d, 0, j0 + k], pw_ref[d, 1, j0 + k]
                    gr_parts.append((hr_ref[d, k, src, :] + pr * in_r - pi * in_i).astype(BF16))
                    gi_parts.append((hi_ref[d, k, src, :] + pr * in_i + pi * in_r).astype(BF16))
                gr = jnp.concatenate(gr_parts, axis=-1)
                gi = jnp.concatenate(gi_parts, axis=-1)
                acc_ref[s * seg:(s + 1) * seg, :] += (
                    jnp.dot(gr, cd_ref[d, 0, cols, :], preferred_element_type=F32)
                    + jnp.dot(gi, cd_ref[d, 1, cols, :], preferred_element_type=F32))
    y = _gelu_tanh(acc_ref[...])
    y = y * _sigmoid(jnp.dot(y.astype(BF16), gw_ref[...], preferred_element_type=F32) + gb_ref[...])
    y_ref[0, 0] = y[:, :LANES]
    y_ref[0, 1] = y[:, LANES:]


def _s5(p, h0, log_a, bd, cd, dvec, glu_w, glu_b, *, u_blk0):
    bsz, _, t, _ = p.shape
    seg, pitch = _seg_rows(t)
    la_re, la_im = log_a
    pos = jnp.arange(seg, dtype=F32)
    n = jnp.stack([pos + 1.0, seg - pos]).reshape(2, 1, seg, 1)
    mag = jnp.exp(n * la_re)
    pw = jnp.stack([mag * jnp.cos(n * la_im), mag * jnp.sin(n * la_im)], axis=1)
    a_re, a_im = pw[0, 0, :, 0:1, :], pw[0, 1, :, 0:1, :]
    a_re = jnp.stack([a_re, pw[1, 0, :, seg - 1:seg, :]])
    a_im = jnp.stack([a_im, pw[1, 1, :, seg - 1:seg, :]])
    full = lambda a: pl.BlockSpec(a.shape, lambda b: (0,) * a.ndim)
    st_spec = pl.BlockSpec((1, 2, 2, S5_NB, 1, LANES), lambda b: (b, 0, 0, 0, 0, 0))
    return pl.pallas_call(
        functools.partial(_s5_kernel, t=t),
        grid=(bsz,),
        in_specs=[pl.BlockSpec((1, 2, t, LANES), lambda b: (b, u_blk0 // 2, 0, 0)),
                  full(a_re), full(a_im), full(pw), full(bd), full(cd), full(dvec), full(glu_w), full(glu_b),
                  st_spec],
        out_specs=[pl.BlockSpec((1, 2, t, LANES), lambda b: (b, 0, 0, 0)), st_spec],
        out_shape=[jax.ShapeDtypeStruct((bsz, 2, t, LANES), F32), jax.ShapeDtypeStruct(h0.shape, F32)],
        scratch_shapes=[pltpu.VMEM((S5_PAR, N_SEG * pitch, LANES), F32),
                        pltpu.VMEM((S5_PAR, N_SEG * pitch, LANES), F32),
                        pltpu.VMEM((2, S5_PAR, N_SEG * pitch, LANES), F32),
                        pltpu.VMEM((2, S5_PAR, N_SEG * pitch, LANES), F32),
                        pltpu.VMEM((t, S5_W), F32)],
        compiler_params=_cparams(("parallel",)),
        name="s5",
    )(p, a_re, a_im, pw, bd, cd, dvec, glu_w, glu_b, h0)


def _s5_params(lam_re, lam_im, log_step, b_re, b_im, c_re, c_im):
    step = jnp.exp(log_step)[..., None]
    mag = jnp.exp(lam_re * step)
    ar, ai = mag * jnp.cos(lam_im * step), mag * jnp.sin(lam_im * step)
    den = lam_re * lam_re + lam_im * lam_im
    zr = ((ar - 1) * lam_re + ai * lam_im) / den
    zi = (ai * lam_re - (ar - 1) * lam_im) / den
    czr = c_re * zr[:, :, None, :] - c_im * zi[:, :, None, :]
    czi = c_re * zi[:, :, None, :] + c_im * zr[:, :, None, :]
    eye = jnp.eye(S5_GROUPS, dtype=F32)

    def in_mat(m):
        return jnp.einsum("gpk,gh->gkhp", m, eye).reshape(S5_W, S5_NSTATE)

    def out_mat(m):
        return jnp.einsum("dgkp,gh->dgphk", m, eye).reshape(2, S5_NSTATE, S5_W)

    bd = jnp.stack([in_mat(b_re), in_mat(b_im)], axis=0).astype(BF16)
    cd = jnp.stack([out_mat(czr), -out_mat(czi)], axis=1).astype(BF16)
    shape = (2, S5_NB, 1, LANES)
    log_a = ((lam_re * step).reshape(shape), (lam_im * step).reshape(shape))
    return log_a, bd, cd


OD_U0 = 5 * HG_HEADS


def _odd_weights(w_in, lower_bound, hg_norm_g, lam_re, lam_im, log_step, b_re, b_im, c_re, c_im, s5_d,
                 glu_w, glu_b):
    log_a, bd, cd = _s5_params(lam_re, lam_im, log_step, b_re, b_im, c_re, c_im)
    return dict(w_in=w_in.astype(BF16), lb=lower_bound, norm_g=hg_norm_g, log_a=log_a, bd=bd, cd=cd,
                dvec=s5_d.reshape(1, S5_W), glu_w=glu_w.astype(BF16), glu_b=glu_b.reshape(1, S5_W))


def _odd_scans(p, states, w):
    hg_h, s5_h = states
    o, hg_h = _hgrn(p, hg_h, w["lb"], w["norm_g"])
    y, s5_h = _s5(p, s5_h, w["log_a"], w["bd"], w["cd"], w["dvec"], w["glu_w"], w["glu_b"], u_blk0=OD_U0)
    return o, y, (hg_h, s5_h)


def _odd_zero_states(bsz):
    return (jnp.zeros((bsz, 2, HG_HEADS, LANES, LANES), F32),
            jnp.zeros((bsz, 2, 2, S5_NB, 1, LANES), F32))


PROJ_TILE = 256
MIX_TILE = 512
FFN_TILE = 1024
FFN_COLS = 256


def kernel(x, c, ctx, c_ctx, w_mod, b_mod, norm_mix_g, norm_ffn_g, final_norm_g,
           ev_w_in, ev_w_out, ssd_conv_w, ssd_conv_b, ssd_dt_bias, ssd_a_log, ssd_d, ssd_norm_g,
           lru_conv_w, lru_conv_b, lru_w_a, lru_b_a, lru_w_i, lru_b_i, lru_lam,
           od_w_in, od_w_out, hg_lb_logits, hg_norm_g,
           s5_lam_re, s5_lam_im, s5_log_step, s5_b_re, s5_b_im, s5_c_re, s5_c_im, s5_d,
           s5_glu_w, s5_glu_b,
           ffn_w_gate, ffn_w_up, ffn_conv_w, ffn_conv_b, ffn_w_down):
    bsz, _, d = x.shape
    depth = w_mod.shape[0]
    prob = jax.nn.softmax(hg_lb_logits.astype(F32), axis=0)
    lower_bounds = (jnp.cumsum(prob, axis=0) - prob[0]).astype(hg_lb_logits.dtype)

    pad = (-(bsz + 1)) % SUBLANES
    cond = jnp.concatenate([c, c_ctx[None], jnp.zeros((pad, d), c.dtype)], axis=0)
    mods = _modulation(cond, w_mod, b_mod).transpose(0, 2, 1, 3)

    for layer in range(depth):
        last = layer == depth - 1
        j = layer // 2
        mod_x = mods[layer, :bsz]
        mod_c = mods[layer, bsz:bsz + 1]
        if layer % 2 == 0:
            w = _even_weights(ev_w_in[j], ssd_conv_w[j], ssd_conv_b[j], ssd_dt_bias[j], ssd_a_log[j], ssd_d[j],
                              lru_conv_w[j], lru_conv_b[j], lru_w_a[j], lru_b_a[j], lru_w_i[j], lru_b_i[j],
                              lru_lam[j])
            scans, zero_states = _even_scans, _even_zero_states
            w_out, norm_gain, norm_first = ev_w_out[j].astype(BF16), ssd_norm_g[j], True
        else:
            w = _odd_weights(od_w_in[j], lower_bounds[layer], hg_norm_g[j], s5_lam_re[j], s5_lam_im[j],
                             s5_log_step[j], s5_b_re[j], s5_b_im[j], s5_c_re[j], s5_c_im[j], s5_d[j],
                             s5_glu_w[j], s5_glu_b[j])
            scans, zero_states = _odd_scans, _odd_zero_states
            w_out, norm_gain, norm_first = od_w_out[j].astype(BF16), hg_norm_g[j], False
        ffn_w = _ffn_weights(ffn_w_gate[layer], ffn_w_up[layer], ffn_conv_w[layer], ffn_conv_b[layer],
                             ffn_w_down[layer], FFN_COLS)

        p_c = _project(ctx, mod_c, norm_mix_g[layer], w["w_in"], PROJ_TILE)
        a_c, b_c, states = scans(p_c, zero_states(bsz), w)
        p_x = _project(x, mod_x, norm_mix_g[layer], w["w_in"], PROJ_TILE)
        a_x, b_x, _ = scans(p_x, states, w)
        x = _mix_out(x, a_x, b_x, mod_x, norm_gain, w_out, norm_first, MIX_TILE)
        x = _conv_ffn(x, mod_x, norm_ffn_g[layer], final_norm_g, *ffn_w, grid_conv=True, final_norm=last,
                      tile=FFN_TILE, tf=FFN_COLS)
        if not last:
            ctx = _mix_out(ctx, a_c, b_c, mod_c, norm_gain, w_out, norm_first, MIX_TILE)
            ctx = _conv_ffn(ctx, mod_c, norm_ffn_g[layer], final_norm_g, *ffn_w, grid_conv=False,
                            final_norm=False, tile=FFN_TILE, tf=FFN_COLS)
    return x
```

```python
import functools
import math

import jax
import jax.numpy as jnp
import numpy as np
from jax import lax
from jax.experimental import pallas as pl
from jax.experimental.pallas import tpu as pltpu

LANES = 128
RMS_EPS = 1e-6
N_MOD = 6
GRID_W = 64
SSD_HEAD_DIM = 64
SSD_HEADS = 16
SSD_GROUPS = 2
SSD_HPG = SSD_HEADS // SSD_GROUPS
SSD_STATE = 128
SSD_CHUNK = 128
LRU_BLOCKS = 8
LRU_C = 8.0
HG_HEADS = 6
S5_GROUPS = 16
S5_GROUP_CH = 16
S5_STATE = 64
VMEM_LIMIT = 56 * 1024 * 1024

BF16 = jnp.bfloat16
F32 = jnp.float32
HIGHEST = lax.Precision.HIGHEST


def _cparams(sem):
    return pltpu.CompilerParams(dimension_semantics=sem, vmem_limit_bytes=VMEM_LIMIT)


def _dot(a, b):
    return jnp.dot(a.astype(BF16), b.astype(BF16), preferred_element_type=F32)


def _dot32(a, b):
    return jnp.dot(a, b, preferred_element_type=F32, precision=HIGHEST)


def _sigmoid(v):
    return 0.5 * jnp.tanh(0.5 * v) + 0.5


def _silu(v):
    return v * _sigmoid(v)


def _gelu_tanh(v):
    return 0.5 * v * (1.0 + jnp.tanh(math.sqrt(2.0 / math.pi) * (v + 0.044715 * (v * v * v))))


def _softplus(v):
    return jnp.maximum(v, 0.0) + jnp.log(1.0 + jnp.exp(-jnp.abs(v)))


def _rms(v, g):
    return v * lax.rsqrt(jnp.mean(v * v, axis=-1, keepdims=True) + RMS_EPS) * g


def _norm_mod(xv, g, shift, scale):
    return _rms(xv, g) * (1.0 + scale) + shift


def _mod_kernel(s_ref, w_ref, b_ref, o_ref):
    o_ref[0, 0] = _dot(_silu(s_ref[...]), w_ref[0]) + b_ref[0, 0]


def _modulation(s, w_mod, b_mod):
    depth, d, _ = w_mod.shape
    rows = s.shape[0]
    return pl.pallas_call(
        _mod_kernel,
        grid=(depth, N_MOD),
        in_specs=[pl.BlockSpec((rows, d), lambda l, j: (0, 0)),
                  pl.BlockSpec((1, d, d), lambda l, j: (l, 0, j)),
                  pl.BlockSpec((1, 1, 1, d), lambda l, j: (l, j, 0, 0))],
        out_specs=pl.BlockSpec((1, 1, rows, d), lambda l, j: (l, j, 0, 0)),
        out_shape=jax.ShapeDtypeStruct((depth, N_MOD, rows, d), F32),
        compiler_params=_cparams(("arbitrary", "arbitrary")),
        name="modulation",
    )(s, w_mod, b_mod.reshape(depth, N_MOD, 1, d))


def _proj_kernel(x_ref, mod_ref, g_ref, w_ref, o_ref, *, nblk):
    m = mod_ref[0]
    h = _norm_mod(x_ref[0], g_ref[...], m[0:1], m[1:2]).astype(BF16)
    group = 4
    for b0 in range(0, nblk, group):
        nb = min(group, nblk - b0)
        r = jnp.dot(h, w_ref[:, b0 * LANES:(b0 + nb) * LANES], preferred_element_type=F32)
        for k in range(nb):
            o_ref[0, b0 + k] = r[:, k * LANES:(k + 1) * LANES]


def _project(x, mod, gain, w, tile):
    bsz, t, d = x.shape
    nblk = w.shape[1] // LANES
    tile = min(tile, t)
    mod_map = (lambda b, i: (b, 0, 0)) if mod.shape[0] == bsz else (lambda b, i: (0, 0, 0))
    return pl.pallas_call(
        functools.partial(_proj_kernel, nblk=nblk),
        grid=(bsz, t // tile),
        in_specs=[pl.BlockSpec((1, tile, d), lambda b, i: (b, i, 0)),
                  pl.BlockSpec((1, N_MOD, d), mod_map),
                  pl.BlockSpec((1, d), lambda b, i: (0, 0)),
                  pl.BlockSpec((d, nblk * LANES), lambda b, i: (0, 0), pipeline_mode=pl.Buffered(1))],
        out_specs=pl.BlockSpec((1, nblk, tile, LANES), lambda b, i: (b, 0, i, 0)),
        out_shape=jax.ShapeDtypeStruct((bsz, nblk, t, LANES), F32),
        compiler_params=_cparams(("parallel", "parallel")),
        name="project",
    )(x, mod, gain.reshape(1, d), w)


def _mix_out_kernel(x_ref, ma_ref, mb_ref, mod_ref, ng_ref, w_ref, o_ref, *, norm_first):
    m = mod_ref[0]
    pa = [ma_ref[0, k] for k in range(ma_ref.shape[1])]
    pb = [mb_ref[0, k].astype(BF16) for k in range(mb_ref.shape[1])]
    if norm_first:
        pa = [_rms(jnp.concatenate(pa, axis=-1), ng_ref[...]).astype(BF16)]
    else:
        pa = [p.astype(BF16) for p in pa]
    v = jnp.concatenate(pa + pb, axis=-1)
    o_ref[0] = x_ref[0] + m[2:3] * jnp.dot(v, w_ref[...], preferred_element_type=F32)


def _mix_out(x, mix_a, mix_b, mod, norm_gain, w, norm_first, tile):
    bsz, t, d = x.shape
    tile = min(tile, t)
    mod_map = (lambda b, i: (b, 0, 0)) if mod.shape[0] == bsz else (lambda b, i: (0, 0, 0))
    ng = norm_gain.reshape(1, -1)
    mix_spec = lambda a: pl.BlockSpec((1, a.shape[1], tile, LANES), lambda b, i: (b, 0, i, 0))
    return pl.pallas_call(
        functools.partial(_mix_out_kernel, norm_first=norm_first),
        grid=(bsz, t // tile),
        in_specs=[pl.BlockSpec((1, tile, d), lambda b, i: (b, i, 0)),
                  mix_spec(mix_a), mix_spec(mix_b),
                  pl.BlockSpec((1, N_MOD, d), mod_map),
                  pl.BlockSpec(ng.shape, lambda b, i: (0, 0)),
                  pl.BlockSpec(w.shape, lambda b, i: (0, 0), pipeline_mode=pl.Buffered(1))],
        out_specs=pl.BlockSpec((1, tile, d), lambda b, i: (b, i, 0)),
        out_shape=jax.ShapeDtypeStruct((bsz, t, d), F32),
        compiler_params=_cparams(("parallel", "parallel")),
        name="mix_out",
    )(x, mix_a, mix_b, mod, ng, w)


def _ffn_kernel(x_ref, xp_ref, xn_ref, mod_ref, g_ref, fg_ref, wg_ref, wu_ref, cw_ref, cb_ref, wd_ref,
                o_ref, fx_ref, gt_ref, a0_ref, a1_ref, *, tile, halo, tf, grid_conv, period, final_norm):
    i = pl.program_id(1)
    nt = pl.num_programs(1)
    nf = wg_ref.shape[0]
    m = mod_ref[0]
    fx_ref[halo:halo + tile] = _norm_mod(x_ref[0], g_ref[...], m[3:4], m[4:5]).astype(BF16)
    if halo:
        keep_p = jnp.where(i > 0, 1.0, 0.0)
        keep_n = jnp.where(i < nt - 1, 1.0, 0.0)
        fx_ref[0:halo] = (keep_p * _norm_mod(xp_ref[0], g_ref[...], m[3:4], m[4:5])).astype(BF16)
        fx_ref[halo + tile:] = (keep_n * _norm_mod(xn_ref[0], g_ref[...], m[3:4], m[4:5])).astype(BF16)

    def gate(f, dst_ref):
        dst_ref[...] = jnp.dot(fx_ref[...], wg_ref[f], preferred_element_type=F32)

    def column(f, src_ref):
        up = jnp.dot(fx_ref[halo:halo + tile], wu_ref[f], preferred_element_type=F32)
        a = src_ref[...]
        cw = cw_ref[f]
        rows = a.shape[0]
        pos = lax.broadcasted_iota(jnp.int32, a.shape, 0)
        if grid_conv:
            col = pos % GRID_W
            a_m1 = jnp.where(col == 0, 0.0, pltpu.roll(a, 1, axis=0))
            a_p1 = jnp.where(col == GRID_W - 1, 0.0, pltpu.roll(a, rows - 1, axis=0))
            conv = cb_ref[f]
            for dr in range(3):
                lo = dr * GRID_W
                conv = conv + (cw[3 * dr + 0:3 * dr + 1] * a_m1[lo:lo + tile]
                               + cw[3 * dr + 1:3 * dr + 2] * a[lo:lo + tile]
                               + cw[3 * dr + 2:3 * dr + 3] * a_p1[lo:lo + tile])
        else:
            col = pos % period
            a_m1 = jnp.where(col == 0, 0.0, pltpu.roll(a, 1, axis=0))
            a_p1 = jnp.where(col == period - 1, 0.0, pltpu.roll(a, rows - 1, axis=0))
            conv = cb_ref[f] + cw[3:4] * a_m1 + cw[4:5] * a + cw[5:6] * a_p1
        start = f * tf if isinstance(f, int) else pl.multiple_of(f * tf, tf)
        gt_ref[:, pl.ds(start, tf)] = (_silu(conv) * up).astype(BF16)

    def column_pair(p, carry):
        f = 2 * p
        gate(f + 1, a1_ref)
        column(f, a0_ref)
        gate(f + 2, a0_ref)
        column(f + 1, a1_ref)
        return carry

    gate(0, a0_ref)
    pairs = (nf - 1) // 2
    lax.fori_loop(0, pairs, column_pair, 0)
    if nf - 2 * pairs == 2:
        gate(nf - 1, a1_ref)
        column(nf - 2, a0_ref)
        column(nf - 1, a1_ref)
    else:
        column(nf - 1, a0_ref)
    y = x_ref[0] + m[5:6] * jnp.dot(gt_ref[...], wd_ref[...], preferred_element_type=F32)
    if final_norm:
        y = _rms(y, fg_ref[...])
    o_ref[0] = y


def _conv_ffn(x, mod, gain, final_gain, w_gate, w_up, conv_w, conv_b, w_down, grid_conv, final_norm,
              tile, tf):
    shape = x.shape
    period = x.shape[1]
    if not grid_conv and mod.shape[0] == 1 and tile % period == 0 and (x.shape[0] * period) % tile == 0:
        x = x.reshape(-1, tile, x.shape[2])
    bsz, t, d = x.shape
    nf = w_gate.shape[0]
    dff = nf * tf
    tile = min(tile, t)
    halo = GRID_W if grid_conv else 0
    nh = t // GRID_W
    per = tile // GRID_W
    mod_map = (lambda b, i: (b, 0, 0)) if mod.shape[0] == bsz else (lambda b, i: (0, 0, 0))
    resident = lambda a: pl.BlockSpec(a.shape, lambda b, i: (0,) * a.ndim, pipeline_mode=pl.Buffered(1))
    return pl.pallas_call(
        functools.partial(_ffn_kernel, tile=tile, halo=halo, tf=tf, grid_conv=grid_conv, period=period,
                          final_norm=final_norm),
        grid=(bsz, t // tile),
        in_specs=[pl.BlockSpec((1, tile, d), lambda b, i: (b, i, 0)),
                  pl.BlockSpec((1, GRID_W, d), lambda b, i: (b, jnp.maximum(i * per - 1, 0), 0)),
                  pl.BlockSpec((1, GRID_W, d), lambda b, i: (b, jnp.minimum((i + 1) * per, nh - 1), 0)),
                  pl.BlockSpec((1, N_MOD, d), mod_map),
                  pl.BlockSpec((1, d), lambda b, i: (0, 0)),
                  pl.BlockSpec((1, d), lambda b, i: (0, 0)),
                  resident(w_gate), resident(w_up), resident(conv_w), resident(conv_b), resident(w_down)],
        out_specs=pl.BlockSpec((1, tile, d), lambda b, i: (b, i, 0)),
        out_shape=jax.ShapeDtypeStruct((bsz, t, d), F32),
        scratch_shapes=[pltpu.VMEM((tile + 2 * halo, d), BF16), pltpu.VMEM((tile, dff), BF16),
                        pltpu.VMEM((tile + 2 * halo, tf), F32), pltpu.VMEM((tile + 2 * halo, tf), F32)],
        compiler_params=_cparams(("parallel", "parallel")),
        name="conv_ffn",
    )(x, x, x, mod, gain.reshape(1, d), final_gain.reshape(1, d), w_gate, w_up, conv_w, conv_b,
      w_down).reshape(shape)


def _ffn_weights(w_gate, w_up, conv_w, conv_b, w_down, tf):
    d, dff = w_gate.shape
    nf = dff // tf
    tiles = lambda w: w.reshape(w.shape[0], nf, tf).transpose(1, 0, 2)
    return (tiles(w_gate).astype(BF16), tiles(w_up).astype(BF16), tiles(conv_w.reshape(9, dff)),
            tiles(conv_b.reshape(1, dff)), w_down.astype(BF16))


def _conv4_silu(v, w, b):
    t = v.shape[0]
    pos = lax.broadcasted_iota(jnp.int32, v.shape, 0)
    acc = b + w[1:2] * v
    acc = acc + w[0:1] * jnp.where(pos < 1, 0.0, pltpu.roll(v, 1, axis=0))
    acc = acc + w[2:3] * jnp.where(pos >= t - 1, 0.0, pltpu.roll(v, t - 1, axis=0))
    acc = acc + w[3:4] * jnp.where(pos >= t - 2, 0.0, pltpu.roll(v, t - 2, axis=0))
    return acc


def _ssd_kernel(xr_ref, br_ref, cr_ref, dt_ref, z_ref, cwx_ref, cwb_ref, cwc_ref, cbx_ref, cbb_ref, cbc_ref,
                dtb_ref, alog_ref, dvec_ref, h0_ref, y_ref, ht_ref, xs_ref, bs_ref, cs_ref, s_ref, *, t):
    nblk = 4
    nchunk = t // SSD_CHUNK
    for k in range(nblk):
        xs_ref[k] = _silu(_conv4_silu(xr_ref[0, k], cwx_ref[k], cbx_ref[k]))
    bs_ref[...] = _silu(_conv4_silu(br_ref[0, 0], cwb_ref[0], cbb_ref[0]))
    cs_ref[...] = _silu(_conv4_silu(cr_ref[0, 0], cwc_ref[0], cbc_ref[0]))

    dtb = dtb_ref[0]
    a_neg = -jnp.exp(alog_ref[0])
    li = lax.broadcasted_iota(jnp.int32, (SSD_CHUNK, SSD_CHUNK), 0)
    si = lax.broadcasted_iota(jnp.int32, (SSD_CHUNK, SSD_CHUNK), 1)
    lane = lax.broadcasted_iota(jnp.int32, (1, LANES), 1)
    lo_half = lane < SSD_HEAD_DIM

    valid = [li >= si, li <= si]
    tri = [v.astype(F32) for v in valid]
    for d in (0, 1):
        for k in range(nblk):
            s_ref[d, k] = h0_ref[0, d, 0, k]

    def pair(i, carry, finish):
        jobs = [(0, pl.ds(pl.multiple_of(i * SSD_CHUNK, SSD_CHUNK), SSD_CHUNK)),
                (1, pl.ds(pl.multiple_of((nchunk - 1 - i) * SSD_CHUNK, SSD_CHUNK), SSD_CHUNK))]
        pre = []
        for d, rows in jobs:
            bm = bs_ref[rows, :]
            cm = cs_ref[rows, :]
            dt = _softplus(dt_ref[0, 0, rows, :] + dtb)
            cum = _dot32(tri[d], dt * a_neg)
            cb = lax.dot_general(cm.astype(BF16), bm.astype(BF16), (((1,), (1,)), ((), ())),
                                 preferred_element_type=F32)
            pre.append((dt, cum, cum.T, dt.T, cb, bm.T.astype(BF16), cm.astype(BF16)))
        for k in range(nblk):
            for (d, rows), (dt, cum, cum_t, dt_t, cb, bm_t, cm16) in zip(jobs, pre):
                end_row = SSD_CHUNK - 1 if d == 0 else 0
                xk = xs_ref[k, rows, :]
                ms = []
                for e in range(2):
                    idx = d * SSD_HPG + 2 * k + e
                    seg = jnp.where(valid[d], cum[:, idx:idx + 1] - cum_t[idx:idx + 1, :], -jnp.inf)
                    ms.append((cb * jnp.exp(seg) * dt_t[idx:idx + 1, :]).astype(BF16))
                x_lo = jnp.where(lo_half, xk, 0.0).astype(BF16)
                x_hi = jnp.where(lo_half, 0.0, xk).astype(BF16)
                yk = (jnp.dot(ms[0], x_lo, preferred_element_type=F32)
                      + jnp.dot(ms[1], x_hi, preferred_element_type=F32))
                i0 = d * SSD_HPG + 2 * k
                ecol = jnp.where(lo_half, cum[:, i0:i0 + 1], cum[:, i0 + 1:i0 + 2])
                dcol = jnp.where(lo_half, dt[:, i0:i0 + 1], dt[:, i0 + 1:i0 + 2])
                tot = jnp.where(lo_half, cum[end_row:end_row + 1, i0:i0 + 1],
                                cum[end_row:end_row + 1, i0 + 1:i0 + 2])
                sk = s_ref[d, k]
                yk = yk + jnp.dot(cm16, sk.astype(BF16), preferred_element_type=F32) * jnp.exp(ecol)
                xw = (xk * dcol * jnp.exp(tot - ecol)).astype(BF16)
                s_ref[d, k] = sk * jnp.exp(tot) + jnp.dot(bm_t, xw, preferred_element_type=F32)
                if finish:
                    ytot = y_ref[0, k, rows, :] + yk + dvec_ref[k] * xk
                    y_ref[0, k, rows, :] = ytot * _silu(z_ref[0, k, rows, :])
                else:
                    y_ref[0, k, rows, :] = yk
        return carry

    lax.fori_loop(0, nchunk // 2, functools.partial(pair, finish=False), 0)
    lax.fori_loop(nchunk // 2, nchunk, functools.partial(pair, finish=True), 0)
    for d in (0, 1):
        for k in range(nblk):
            ht_ref[0, d, 0, k] = s_ref[d, k]


def _ssd(p, h0, conv_w, conv_b, dtb, alog, dvec, *, z_blk0, xbc_blk0, dt_blk0):
    bsz, _, t, _ = p.shape
    assert (t // SSD_CHUNK) % 2 == 0
    g4 = lambda off: (lambda b, g: (b, off // 4 + g, 0, 0))
    g1 = lambda off: (lambda b, g: (b, off + g, 0, 0))
    big = pl.BlockSpec((1, 4, t, LANES), g4(xbc_blk0))
    one = lambda off: pl.BlockSpec((1, 1, t, LANES), g1(off))
    st_spec = pl.BlockSpec((1, 2, 1, 4, SSD_STATE, LANES), lambda b, g: (b, 0, g, 0, 0, 0))
    return pl.pallas_call(
        functools.partial(_ssd_kernel, t=t),
        grid=(bsz, SSD_GROUPS),
        in_specs=[big, one(xbc_blk0 + 8), one(xbc_blk0 + 10), one(dt_blk0),
                  pl.BlockSpec((1, 4, t, LANES), g4(z_blk0)),
                  pl.BlockSpec((4, 4, LANES), lambda b, g: (g, 0, 0)),
                  pl.BlockSpec((1, 4, LANES), lambda b, g: (8 + g, 0, 0)),
                  pl.BlockSpec((1, 4, LANES), lambda b, g: (10 + g, 0, 0)),
                  pl.BlockSpec((4, 1, LANES), lambda b, g: (g, 0, 0)),
                  pl.BlockSpec((1, 1, LANES), lambda b, g: (8 + g, 0, 0)),
                  pl.BlockSpec((1, 1, LANES), lambda b, g: (10 + g, 0, 0)),
                  pl.BlockSpec((1, 1, LANES), lambda b, g: (g, 0, 0)),
                  pl.BlockSpec((1, 1, LANES), lambda b, g: (g, 0, 0)),
                  pl.BlockSpec((4, 1, LANES), lambda b, g: (g, 0, 0)),
                  st_spec],
        out_specs=[pl.BlockSpec((1, 4, t, LANES), lambda b, g: (b, g, 0, 0)), st_spec],
        out_shape=[jax.ShapeDtypeStruct((bsz, 8, t, LANES), F32),
                   jax.ShapeDtypeStruct(h0.shape, F32)],
        scratch_shapes=[pltpu.VMEM((4, t, LANES), F32), pltpu.VMEM((t, LANES), F32),
                        pltpu.VMEM((t, LANES), F32), pltpu.VMEM((2, 4, SSD_STATE, LANES), F32)],
        compiler_params=_cparams(("parallel", "parallel")),
        name="ssd",
    )(p, p, p, p, p, conv_w, conv_w, conv_w, conv_b, conv_b, conv_b, dtb, alog, dvec, h0)


LRU_NB = 2
N_SEG = 8
SCAN_UNROLL = 8


def _seg_rows(t):
    seg = t // N_SEG
    return seg, seg + 8


def _conv4(v, w, b):
    return _conv4_silu(v, w, b)


def _seg_scan(a_ref, b_ref, acc_ref, h_ref, lead, seg, pitch, reverse):
    def step(i, carry):
        out = []
        for (h, acc), ld, rev in zip(carry, lead, reverse):
            j = (seg - 1 - i) if rev else i
            idx = ld + (pl.ds(j, N_SEG, stride=pitch), slice(None))
            a = a_ref[idx]
            h = a * h + b_ref[idx]
            acc = acc * a
            acc_ref[idx] = acc
            h_ref[idx] = h
            out.append((h, acc))
        return tuple(out)

    init = (jnp.zeros((N_SEG, LANES), F32), jnp.ones((N_SEG, LANES), F32))
    return lax.fori_loop(0, seg, step, tuple(init for _ in lead), unroll=SCAN_UNROLL)


def _seg_inputs(h0, end, tot, reverse):
    rows = [None] * N_SEG
    hin = h0
    for s in (range(N_SEG - 1, -1, -1) if reverse else range(N_SEG)):
        rows[s] = hin
        hin = tot[s:s + 1] * hin + end[s:s + 1]
    return rows, hin


def _lru_kernel(u_ref, gy_ref, cw_ref, cb_ref, wa_ref, wi_ref, ba_ref, bi_ref, lam_ref, h0_ref,
                r_ref, ht_ref, uc_ref, a_ref, b_ref, acc_ref, hl_ref, *, t):
    seg, pitch = _seg_rows(t)
    for k in range(LRU_NB):
        uc_ref[k] = _conv4(u_ref[0, k], cw_ref[k], cb_ref[k])
    for d in (1, 0):
        for k in range(LRU_NB):
            c = (-0.5 * LRU_C * math.log2(math.e)) * _softplus(-lam_ref[d, k])
            for s in range(N_SEG):
                u = uc_ref[k, s * seg:(s + 1) * seg, :]
                u16 = u.astype(BF16)
                ta = jnp.tanh(jnp.dot(u16, wa_ref[d, k], preferred_element_type=F32) + ba_ref[d, k])
                ti = jnp.tanh(jnp.dot(u16, wi_ref[d, k], preferred_element_type=F32) + bi_ref[d, k])
                a = jnp.exp2(c * ta + c)
                a_ref[d, k, s * pitch:s * pitch + seg, :] = a
                half_u = 0.5 * u
                gated_u = half_u * ti + half_u
                y = (1.0 - a) * (1.0 + a)
                root = jnp.where(y > 0.0, y * lax.rsqrt(y), 0.0)
                b_ref[d, k, s * pitch:s * pitch + seg, :] = root * gated_u
    chains = [(d, k) for d in (1, 0) for k in range(LRU_NB)]
    scanned = _seg_scan(a_ref, b_ref, acc_ref, hl_ref, chains, seg, pitch, [d == 1 for d, _ in chains])
    for (d, k), (end, tot) in zip(chains, scanned):
        rows, hfin = _seg_inputs(h0_ref[0, d, k], end, tot, reverse=(d == 1))
        ht_ref[0, d, k] = hfin
        for s in range(N_SEG):
            src = slice(s * pitch, s * pitch + seg)
            dst = slice(s * seg, (s + 1) * seg)
            h = hl_ref[d, k, src, :] + acc_ref[d, k, src, :] * rows[s]
            if d == 1:
                r_ref[0, k, dst, :] = h
            else:
                r_ref[0, k, dst, :] = (r_ref[0, k, dst, :] + h) * _gelu_tanh(gy_ref[0, k, dst, :])


def _lru(p, h0, conv_w, conv_b, wa, wi, ba, bi, lam, *, gy_blk0, u_blk0):
    bsz, _, t, _ = p.shape
    nb = LRU_NB
    seg, pitch = _seg_rows(t)
    blk = lambda off: pl.BlockSpec((1, nb, t, LANES), lambda b, g: (b, off // nb + g, 0, 0))
    par = lambda shape: pl.BlockSpec(shape, lambda b, g: (0, g) + (0,) * (len(shape) - 2))
    st_spec = pl.BlockSpec((1, 2, nb, 1, LANES), lambda b, g: (b, 0, g, 0, 0))
    return pl.pallas_call(
        functools.partial(_lru_kernel, t=t),
        grid=(bsz, LRU_BLOCKS // nb),
        in_specs=[blk(u_blk0), blk(gy_blk0),
                  pl.BlockSpec((nb, 4, LANES), lambda b, g: (g, 0, 0)),
                  pl.BlockSpec((nb, 1, LANES), lambda b, g: (g, 0, 0)),
                  par((2, nb, LANES, LANES)), par((2, nb, LANES, LANES)),
                  par((2, nb, 1, LANES)), par((2, nb, 1, LANES)), par((2, nb, 1, LANES)),
                  st_spec],
        out_specs=[pl.BlockSpec((1, nb, t, LANES), lambda b, g: (b, g, 0, 0)), st_spec],
        out_shape=[jax.ShapeDtypeStruct((bsz, LRU_BLOCKS, t, LANES), F32),
                   jax.ShapeDtypeStruct(h0.shape, F32)],
        scratch_shapes=[pltpu.VMEM((nb, t, LANES), F32)] + [pltpu.VMEM((2, nb, N_SEG * pitch, LANES), F32)] * 4,
        compiler_params=_cparams(("parallel", "parallel")),
        name="rglru",
    )(p, p, conv_w, conv_b, wa, wi, ba, bi, lam, h0)


EV_Z0, EV_XBC0, EV_DT0, EV_GY0, EV_U0, EV_NBLK = 0, 8, 20, 22, 30, 38


def _blocks(v, n):
    return v.reshape(n, 1, LANES)


def _per_group_heads(v):
    v = v.reshape(2, SSD_GROUPS, SSD_HPG).transpose(1, 0, 2).reshape(SSD_GROUPS, 2 * SSD_HPG)
    return jnp.pad(v, ((0, 0), (0, LANES - 2 * SSD_HPG))).reshape(SSD_GROUPS, 1, LANES)


def _even_weights(w_in, conv_w, conv_b, dt_bias, a_log, ssd_d, lru_conv_w, lru_conv_b, w_a, b_a, w_i, b_i, lam):
    d = w_in.shape[0]
    z, xbc, dt, gy, u = jnp.split(w_in, (1024, 2560, 2592, 3616), axis=1)
    dt = dt.reshape(d, 2, SSD_GROUPS, SSD_HPG).transpose(0, 2, 1, 3).reshape(d, SSD_GROUPS, 2 * SSD_HPG)
    dt = jnp.pad(dt, ((0, 0), (0, 0), (0, LANES - 2 * SSD_HPG))).reshape(d, SSD_GROUPS * LANES)
    return dict(
        w_in=jnp.concatenate([z, xbc, dt, gy, u], axis=1).astype(BF16),
        conv_w=conv_w.reshape(4, 12, LANES).transpose(1, 0, 2), conv_b=_blocks(conv_b, 12),
        dtb=_per_group_heads(dt_bias), alog=_per_group_heads(a_log),
        dvec=_blocks(jnp.repeat(ssd_d, SSD_HEAD_DIM), 8),
        lru_conv_w=lru_conv_w.reshape(4, LRU_BLOCKS, LANES).transpose(1, 0, 2),
        lru_conv_b=_blocks(lru_conv_b, LRU_BLOCKS),
        w_a=(0.5 * w_a).astype(BF16), w_i=(0.5 * w_i).astype(BF16),
        b_a=0.5 * b_a.reshape(2, LRU_BLOCKS, 1, LANES), b_i=0.5 * b_i.reshape(2, LRU_BLOCKS, 1, LANES),
        lam=lam.reshape(2, LRU_BLOCKS, 1, LANES))


def _even_scans(p, states, w):
    ssd_h, lru_h = states
    y, ssd_h = _ssd(p, ssd_h, w["conv_w"], w["conv_b"], w["dtb"], w["alog"], w["dvec"],
                    z_blk0=EV_Z0, xbc_blk0=EV_XBC0, dt_blk0=EV_DT0)
    r, lru_h = _lru(p, lru_h, w["lru_conv_w"], w["lru_conv_b"], w["w_a"], w["w_i"], w["b_a"], w["b_i"],
                    w["lam"], gy_blk0=EV_GY0, u_blk0=EV_U0)
    return y, r, (ssd_h, lru_h)


def _even_zero_states(bsz):
    return (jnp.zeros((bsz, 2, SSD_GROUPS, 4, SSD_STATE, LANES), F32),
            jnp.zeros((bsz, 2, LRU_BLOCKS, 1, LANES), F32))


HG_CHUNK = 128
HG_PAR = 3
SUBLANES = 8


def _group_boundary(cum, c, reverse):
    m = c // 2
    off = m if reverse else m - 1
    n = cum.shape[0]
    if c >= 2 * SUBLANES:
        r = cum.reshape(n // c, c, LANES)
        return jnp.broadcast_to(r[:, off:off + 1, :], r.shape).reshape(n, LANES)
    r = cum.reshape(n // SUBLANES, SUBLANES, LANES)
    sub = lax.broadcasted_iota(jnp.int32, r.shape, 1)
    p = None
    for g0 in range(0, SUBLANES, c):
        cand = jnp.broadcast_to(r[:, g0 + off:g0 + off + 1, :], r.shape)
        p = cand if p is None else jnp.where(sub >= g0, cand, p)
    return p.reshape(n, LANES)


def _hgrn_masks():
    l = np.arange(HG_CHUNK)[:, None]
    s = np.arange(HG_CHUNK)[None, :]
    fwd = []
    size = HG_CHUNK
    while size >= 2:
        half = size // 2
        fwd.append((l // size == s // size) & (l % size >= half) & (s % size < half))
        size = half
    fwd = np.stack(fwd).astype(np.float32)
    pairs = np.stack([fwd, fwd.transpose(0, 2, 1)])
    tri = np.stack([l >= s, l <= s]).astype(np.float32)
    return jnp.asarray(pairs), jnp.asarray(tri)


def _hgrn_chunks(jobs, q_ref, ff_ref, fb_ref, v_ref, lb_ref, sf_ref, sb_ref, pairs_ref, tri_ref):
    n = len(jobs)
    qq, kk, vv, cum = [], [], [], []
    for hd, reverse, rows in jobs:
        lb = lb_ref[hd]
        fx = (fb_ref if reverse else ff_ref)[0, hd, rows, :]
        e = jnp.exp(-jnp.abs(fx))
        big = 1.0 / (1.0 + e)
        small = e * big
        pos = fx >= 0.0
        log2_f = jnp.log2(lb + (1.0 - lb) * jnp.where(pos, big, small))
        kk.append((1.0 - lb) * jnp.where(pos, small, big))
        cum.append(_dot32(tri_ref[1 if reverse else 0], log2_f))
    for hd, reverse, rows in jobs:
        qq.append(_silu(q_ref[0, hd, rows, :]))
        vv.append(v_ref[0, hd, rows, :])
    att = [None] * n
    q16 = [v.astype(BF16) for v in qq]
    k16 = [v.astype(BF16) for v in kk]
    size = HG_CHUNK
    level = 0
    while size >= 2:
        for j, (hd, reverse, rows) in enumerate(jobs):
            fac = jnp.exp2(-jnp.abs(cum[j] - _group_boundary(cum[j], size, reverse))).astype(BF16)
            a_l = lax.dot_general(q16[j] * fac, k16[j] * fac, (((1,), (1,)), ((), ())),
                                  preferred_element_type=F32)
            a_l = a_l * pairs_ref[1 if reverse else 0, level]
            att[j] = a_l if att[j] is None else att[j] + a_l
        size //= 2
        level += 1
    outs = []
    for j, (hd, reverse, rows) in enumerate(jobs):
        s_ref = (sb_ref if reverse else sf_ref).at[hd]
        sv = s_ref[...]
        diag = jnp.sum(qq[j] * kk[j], axis=-1, keepdims=True)
        o = jnp.dot(att[j].astype(BF16), vv[j].astype(BF16), preferred_element_type=F32) + diag * vv[j]
        o = o + jnp.dot((qq[j] * jnp.exp2(cum[j])).astype(BF16), sv.astype(BF16), preferred_element_type=F32)
        end_row = 0 if reverse else HG_CHUNK - 1
        cum_end = cum[j][end_row:end_row + 1, :]
        kw = (kk[j] * jnp.exp2(cum_end - cum[j])).T.astype(BF16)
        keep = jnp.broadcast_to(jnp.exp2(cum_end), (HG_CHUNK, LANES)).T
        s_ref[...] = sv * keep + jnp.dot(kw, vv[j].astype(BF16), preferred_element_type=F32)
        outs.append(o)
    return outs


def _hgrn_kernel(q_ref, ff_ref, fb_ref, v_ref, g_ref, lb_ref, ng_ref, pairs_ref, tri_ref, h0_ref,
                 o_ref, ht_ref, sf_ref, sb_ref, *, t):
    nchunk = t // HG_CHUNK
    for hd in range(HG_PAR):
        sf_ref[hd] = h0_ref[0, 0, hd]
        sb_ref[hd] = h0_ref[0, 1, hd]

    def pair(i, carry, finish):
        jobs = []
        for hd in range(HG_PAR):
            for reverse in (False, True):
                c = (nchunk - 1 - i) if reverse else i
                jobs.append((hd, reverse, pl.ds(pl.multiple_of(c * HG_CHUNK, HG_CHUNK), HG_CHUNK)))
        outs = _hgrn_chunks(jobs, q_ref, ff_ref, fb_ref, v_ref, lb_ref, sf_ref, sb_ref, pairs_ref, tri_ref)
        for (hd, reverse, rows), o in zip(jobs, outs):
            if finish:
                o = o_ref[0, hd, rows, :] + o
                o = _rms(o, ng_ref[hd]) * _silu(g_ref[0, hd, rows, :])
            o_ref[0, hd, rows, :] = o
        return carry

    lax.fori_loop(0, nchunk // 2, functools.partial(pair, finish=False), 0)
    lax.fori_loop(nchunk // 2, nchunk, functools.partial(pair, finish=True), 0)
    for hd in range(HG_PAR):
        ht_ref[0, 0, hd] = sf_ref[hd]
        ht_ref[0, 1, hd] = sb_ref[hd]


def _hgrn(p, h0, lb, norm_g):
    bsz, _, t, _ = p.shape
    assert (t // HG_CHUNK) % 2 == 0
    hp = HG_PAR
    blk = lambda off: pl.BlockSpec((1, hp, t, LANES), lambda b, h: (b, off // hp + h, 0, 0))
    par = pl.BlockSpec((hp, 1, LANES), lambda b, h: (h, 0, 0))
    st_spec = pl.BlockSpec((1, 2, hp, LANES, LANES), lambda b, h: (b, 0, h, 0, 0))
    pairs, tri = _hgrn_masks()
    return pl.pallas_call(
        functools.partial(_hgrn_kernel, t=t),
        grid=(bsz, HG_HEADS // hp),
        in_specs=[blk(0), blk(HG_HEADS), blk(2 * HG_HEADS), blk(3 * HG_HEADS), blk(4 * HG_HEADS),
                  par, par,
                  pl.BlockSpec(pairs.shape, lambda b, h: (0, 0, 0, 0)),
                  pl.BlockSpec(tri.shape, lambda b, h: (0, 0, 0)),
                  st_spec],
        out_specs=[pl.BlockSpec((1, hp, t, LANES), lambda b, h: (b, h, 0, 0)), st_spec],
        out_shape=[jax.ShapeDtypeStruct((bsz, HG_HEADS, t, LANES), F32),
                   jax.ShapeDtypeStruct(h0.shape, F32)],
        scratch_shapes=[pltpu.VMEM((hp, LANES, LANES), F32), pltpu.VMEM((hp, LANES, LANES), F32)],
        compiler_params=_cparams(("parallel", "parallel")),
        name="hgrn2",
    )(p, p, p, p, p, lb.reshape(HG_HEADS, 1, LANES), norm_g.reshape(HG_HEADS, 1, LANES), pairs, tri, h0)


S5_W = S5_GROUPS * S5_GROUP_CH
S5_NSTATE = S5_GROUPS * S5_STATE
S5_NB = S5_NSTATE // LANES
S5_PAR = 2


def _s5_kernel(u_ref, are_ref, aim_ref, pw_ref, bd_ref, cd_ref, dvec_ref, gw_ref, gb_ref, h0_ref,
               y_ref, ht_ref, wr_ref, wi_ref, hr_ref, hi_ref, acc_ref, *, t):
    seg, pitch = _seg_rows(t)
    u = jnp.concatenate([u_ref[0, 0], u_ref[0, 1]], axis=-1)
    u16 = u.astype(BF16)
    acc_ref[...] = dvec_ref[...] * u
    chains = [(d, k) for d in (0, 1) for k in range(S5_PAR)]
    for j0 in range(0, S5_NB, S5_PAR):
        cols = slice(j0 * LANES, (j0 + S5_PAR) * LANES)
        group = max(1, LANES // seg)
        for s0 in range(0, N_SEG, group):
            us = u16[s0 * seg:(s0 + group) * seg]
            wr = jnp.dot(us, bd_ref[0, :, cols], preferred_element_type=F32)
            wi = jnp.dot(us, bd_ref[1, :, cols], preferred_element_type=F32)
            for s in range(s0, s0 + group):
                part = slice((s - s0) * seg, (s - s0 + 1) * seg)
                for k in range(S5_PAR):
                    wr_ref[k, s * pitch:s * pitch + seg, :] = wr[part, k * LANES:(k + 1) * LANES]
                    wi_ref[k, s * pitch:s * pitch + seg, :] = wi[part, k * LANES:(k + 1) * LANES]
        ar = [jnp.broadcast_to(are_ref[d, j0 + k], (N_SEG, LANES)) for d, k in chains]
        ai = [jnp.broadcast_to(aim_ref[d, j0 + k], (N_SEG, LANES)) for d, k in chains]

        def step(i, carry, ar=ar, ai=ai):
            out = []
            for c, (d, k) in enumerate(chains):
                gr, gi = carry[c]
                j = (seg - 1 - i) if d == 1 else i
                src = (k, pl.ds(j, N_SEG, stride=pitch), slice(None))
                dst = (d, k, pl.ds(j, N_SEG, stride=pitch), slice(None))
                ngr = ar[c] * gr - ai[c] * gi + wr_ref[src]
                ngi = ar[c] * gi + ai[c] * gr + wi_ref[src]
                hr_ref[dst] = ngr
                hi_ref[dst] = ngi
                out.append((ngr, ngi))
            return tuple(out)

        zero = jnp.zeros((N_SEG, LANES), F32)
        fin = lax.fori_loop(0, seg, step, tuple((zero, zero) for _ in chains), unroll=SCAN_UNROLL)
        for d in (0, 1):
            ins = []
            last = 0 if d == 1 else seg - 1
            for k in range(S5_PAR):
                end_r, end_i = fin[d * S5_PAR + k]
                tot_r = pw_ref[d, 0, j0 + k, last:last + 1, :]
                tot_i = pw_ref[d, 1, j0 + k, last:last + 1, :]
                hin_r, hin_i = h0_ref[0, d, 0, j0 + k], h0_ref[0, d, 1, j0 + k]
                rows = [None] * N_SEG
                for s in (range(N_SEG - 1, -1, -1) if d == 1 else range(N_SEG)):
                    rows[s] = (hin_r, hin_i)
                    hin_r, hin_i = (tot_r[0:1] * hin_r - tot_i[0:1] * hin_i + end_r[s:s + 1],
                                    tot_r[0:1] * hin_i + tot_i[0:1] * hin_r + end_i[s:s + 1])
                ht_ref[0, d, 0, j0 + k] = hin_r
                ht_ref[0, d, 1, j0 + k] = hin_i
                ins.append(rows)
            gr_rows, gi_rows = [], []
            for s in range(N_SEG):
                src = slice(s * pitch, s * pitch + seg)
                gr_parts, gi_parts = [], []
                for k in range(S5_PAR):
                    in_r, in_i = ins[k][s]
                    pr, pi = pw_ref[d, 0, j0 + k], pw_ref[d, 1, j0 + k]
                    gr_parts.append((hr_ref[d, k, src, :] + pr * in_r - pi * in_i).astype(BF16))
                    gi_parts.append((hi_ref[d, k, src, :] + pr * in_i + pi * in_r).astype(BF16))
                gr_rows.append(jnp.concatenate(gr_parts, axis=-1))
                gi_rows.append(jnp.concatenate(gi_parts, axis=-1))
                if len(gr_rows) == group:
                    rows = slice((s + 1 - group) * seg, (s + 1) * seg)
                    acc_ref[rows, :] += (
                        jnp.dot(jnp.concatenate(gr_rows, axis=0), cd_ref[d, 0, cols, :],
                                preferred_element_type=F32)
                        + jnp.dot(jnp.concatenate(gi_rows, axis=0), cd_ref[d, 1, cols, :],
                                  preferred_element_type=F32))
                    gr_rows, gi_rows = [], []
    y = _gelu_tanh(acc_ref[...])
    y = y * _sigmoid(jnp.dot(y.astype(BF16), gw_ref[...], preferred_element_type=F32) + gb_ref[...])
    y_ref[0, 0] = y[:, :LANES]
    y_ref[0, 1] = y[:, LANES:]


def _s5(p, h0, log_a, bd, cd, dvec, glu_w, glu_b, *, u_blk0):
    bsz, _, t, _ = p.shape
    seg, pitch = _seg_rows(t)
    la_re, la_im = log_a
    pos = jnp.arange(seg, dtype=F32)
    n = jnp.stack([pos + 1.0, seg - pos]).reshape(2, 1, seg, 1)
    mag = jnp.exp(n * la_re)
    pw = jnp.stack([mag * jnp.cos(n * la_im), mag * jnp.sin(n * la_im)], axis=1)
    a_re, a_im = pw[0, 0, :, 0:1, :], pw[0, 1, :, 0:1, :]
    a_re = jnp.stack([a_re, pw[1, 0, :, seg - 1:seg, :]])
    a_im = jnp.stack([a_im, pw[1, 1, :, seg - 1:seg, :]])
    full = lambda a: pl.BlockSpec(a.shape, lambda b: (0,) * a.ndim)
    st_spec = pl.BlockSpec((1, 2, 2, S5_NB, 1, LANES), lambda b: (b, 0, 0, 0, 0, 0))
    return pl.pallas_call(
        functools.partial(_s5_kernel, t=t),
        grid=(bsz,),
        in_specs=[pl.BlockSpec((1, 2, t, LANES), lambda b: (b, u_blk0 // 2, 0, 0)),
                  full(a_re), full(a_im), full(pw), full(bd), full(cd), full(dvec), full(glu_w), full(glu_b),
                  st_spec],
        out_specs=[pl.BlockSpec((1, 2, t, LANES), lambda b: (b, 0, 0, 0)), st_spec],
        out_shape=[jax.ShapeDtypeStruct((bsz, 2, t, LANES), F32), jax.ShapeDtypeStruct(h0.shape, F32)],
        scratch_shapes=[pltpu.VMEM((S5_PAR, N_SEG * pitch, LANES), F32),
                        pltpu.VMEM((S5_PAR, N_SEG * pitch, LANES), F32),
                        pltpu.VMEM((2, S5_PAR, N_SEG * pitch, LANES), F32),
                        pltpu.VMEM((2, S5_PAR, N_SEG * pitch, LANES), F32),
                        pltpu.VMEM((t, S5_W), F32)],
        compiler_params=_cparams(("parallel",)),
        name="s5",
    )(p, a_re, a_im, pw, bd, cd, dvec, glu_w, glu_b, h0)


def _s5_params(lam_re, lam_im, log_step, b_re, b_im, c_re, c_im):
    step = jnp.exp(log_step)[..., None]
    mag = jnp.exp(lam_re * step)
    ar, ai = mag * jnp.cos(lam_im * step), mag * jnp.sin(lam_im * step)
    den = lam_re * lam_re + lam_im * lam_im
    zr = ((ar - 1) * lam_re + ai * lam_im) / den
    zi = (ai * lam_re - (ar - 1) * lam_im) / den
    czr = c_re * zr[:, :, None, :] - c_im * zi[:, :, None, :]
    czi = c_re * zi[:, :, None, :] + c_im * zr[:, :, None, :]
    eye = jnp.eye(S5_GROUPS, dtype=F32)

    def in_mat(m):
        return jnp.einsum("gpk,gh->gkhp", m, eye).reshape(S5_W, S5_NSTATE)

    def out_mat(m):
        return jnp.einsum("dgkp,gh->dgphk", m, eye).reshape(2, S5_NSTATE, S5_W)

    bd = jnp.stack([in_mat(b_re), in_mat(b_im)], axis=0).astype(BF16)
    cd = jnp.stack([out_mat(czr), -out_mat(czi)], axis=1).astype(BF16)
    shape = (2, S5_NB, 1, LANES)
    log_a = ((lam_re * step).reshape(shape), (lam_im * step).reshape(shape))
    return log_a, bd, cd


OD_U0 = 5 * HG_HEADS


def _odd_weights(w_in, lower_bound, hg_norm_g, lam_re, lam_im, log_step, b_re, b_im, c_re, c_im, s5_d,
                 glu_w, glu_b):
    log_a, bd, cd = _s5_params(lam_re, lam_im, log_step, b_re, b_im, c_re, c_im)
    return dict(w_in=w_in.astype(BF16), lb=lower_bound, norm_g=hg_norm_g, log_a=log_a, bd=bd, cd=cd,
                dvec=s5_d.reshape(1, S5_W), glu_w=glu_w.astype(BF16), glu_b=glu_b.reshape(1, S5_W))


def _odd_scans(p, states, w):
    hg_h, s5_h = states
    o, hg_h = _hgrn(p, hg_h, w["lb"], w["norm_g"])
    y, s5_h = _s5(p, s5_h, w["log_a"], w["bd"], w["cd"], w["dvec"], w["glu_w"], w["glu_b"], u_blk0=OD_U0)
    return o, y, (hg_h, s5_h)


def _odd_zero_states(bsz):
    return (jnp.zeros((bsz, 2, HG_HEADS, LANES, LANES), F32),
            jnp.zeros((bsz, 2, 2, S5_NB, 1, LANES), F32))


PROJ_TILE = 256
MIX_TILE = 512
FFN_TILE = 1024
FFN_COLS = 256


def kernel(x, c, ctx, c_ctx, w_mod, b_mod, norm_mix_g, norm_ffn_g, final_norm_g,
           ev_w_in, ev_w_out, ssd_conv_w, ssd_conv_b, ssd_dt_bias, ssd_a_log, ssd_d, ssd_norm_g,
           lru_conv_w, lru_conv_b, lru_w_a, lru_b_a, lru_w_i, lru_b_i, lru_lam,
           od_w_in, od_w_out, hg_lb_logits, hg_norm_g,
           s5_lam_re, s5_lam_im, s5_log_step, s5_b_re, s5_b_im, s5_c_re, s5_c_im, s5_d,
           s5_glu_w, s5_glu_b,
           ffn_w_gate, ffn_w_up, ffn_conv_w, ffn_conv_b, ffn_w_down):
    bsz, _, d = x.shape
    depth = w_mod.shape[0]
    prob = jax.nn.softmax(hg_lb_logits.astype(F32), axis=0)
    lower_bounds = (jnp.cumsum(prob, axis=0) - prob[0]).astype(hg_lb_logits.dtype)

    pad = (-(bsz + 1)) % SUBLANES
    cond = jnp.concatenate([c, c_ctx[None], jnp.zeros((pad, d), c.dtype)], axis=0)
    mods = _modulation(cond, w_mod, b_mod).transpose(0, 2, 1, 3)

    for layer in range(depth):
        last = layer == depth - 1
        j = layer // 2
        mod_x = mods[layer, :bsz]
        mod_c = mods[layer, bsz:bsz + 1]
        if layer % 2 == 0:
            w = _even_weights(ev_w_in[j], ssd_conv_w[j], ssd_conv_b[j], ssd_dt_bias[j], ssd_a_log[j], ssd_d[j],
                              lru_conv_w[j], lru_conv_b[j], lru_w_a[j], lru_b_a[j], lru_w_i[j], lru_b_i[j],
                              lru_lam[j])
            scans, zero_states = _even_scans, _even_zero_states
            w_out, norm_gain, norm_first = ev_w_out[j].astype(BF16), ssd_norm_g[j], True
        else:
            w = _odd_weights(od_w_in[j], lower_bounds[layer], hg_norm_g[j], s5_lam_re[j], s5_lam_im[j],
                             s5_log_step[j], s5_b_re[j], s5_b_im[j], s5_c_re[j], s5_c_im[j], s5_d[j],
                             s5_glu_w[j], s5_glu_b[j])
            scans, zero_states = _odd_scans, _odd_zero_states
            w_out, norm_gain, norm_first = od_w_out[j].astype(BF16), hg_norm_g[j], False
        ffn_w = _ffn_weights(ffn_w_gate[layer], ffn_w_up[layer], ffn_conv_w[layer], ffn_conv_b[layer],
                             ffn_w_down[layer], FFN_COLS)

        p_c = _project(ctx, mod_c, norm_mix_g[layer], w["w_in"], PROJ_TILE)
        a_c, b_c, states = scans(p_c, zero_states(bsz), w)
        p_x = _project(x, mod_x, norm_mix_g[layer], w["w_in"], PROJ_TILE)
        a_x, b_x, _ = scans(p_x, states, w)
        x = _mix_out(x, a_x, b_x, mod_x, norm_gain, w_out, norm_first, MIX_TILE)
        x = _conv_ffn(x, mod_x, norm_ffn_g[layer], final_norm_g, *ffn_w, grid_conv=True, final_norm=last,
                      tile=FFN_TILE, tf=FFN_COLS)
        if not last:
            ctx = _mix_out(ctx, a_c, b_c, mod_c, norm_gain, w_out, norm_first, MIX_TILE)
            ctx = _conv_ffn(ctx, mod_c, norm_ffn_g[layer], final_norm_g, *ffn_w, grid_conv=False,
                            final_norm=False, tile=FFN_TILE, tf=FFN_COLS)
    return x
```

```python
import functools
import math

import jax
import jax.numpy as jnp
import numpy as np
from jax import lax
from jax.experimental import pallas as pl
from jax.experimental.pallas import tpu as pltpu

LANES = 128
RMS_EPS = 1e-6
N_MOD = 6
GRID_W = 64
SSD_HEAD_DIM = 64
SSD_HEADS = 16
SSD_GROUPS = 2
SSD_HPG = SSD_HEADS // SSD_GROUPS
SSD_STATE = 128
SSD_CHUNK = 128
LRU_BLOCKS = 8
LRU_C = 8.0
HG_HEADS = 6
S5_GROUPS = 16
S5_GROUP_CH = 16
S5_STATE = 64
VMEM_LIMIT = 56 * 1024 * 1024

BF16 = jnp.bfloat16
F32 = jnp.float32


def _cparams(sem):
    return pltpu.CompilerParams(dimension_semantics=sem, vmem_limit_bytes=VMEM_LIMIT)


def _dot(a, b):
    return jnp.dot(a.astype(BF16), b.astype(BF16), preferred_element_type=F32)


def _dot_select(sel, v):
    hi = v.astype(BF16)
    rest = v - hi.astype(F32)
    mid = rest.astype(BF16)
    lo = (rest - mid.astype(F32)).astype(BF16)
    return (jnp.dot(sel, hi, preferred_element_type=F32) + jnp.dot(sel, mid, preferred_element_type=F32)
            + jnp.dot(sel, lo, preferred_element_type=F32))


def _sigmoid(v):
    return 0.5 * jnp.tanh(0.5 * v) + 0.5


def _silu(v):
    return v * _sigmoid(v)


def _gelu_tanh(v):
    return 0.5 * v * (1.0 + jnp.tanh(math.sqrt(2.0 / math.pi) * (v + 0.044715 * (v * v * v))))


def _softplus(v):
    return jnp.maximum(v, 0.0) + jnp.log(1.0 + jnp.exp(-jnp.abs(v)))


def _rms(v, g):
    return v * lax.rsqrt(jnp.mean(v * v, axis=-1, keepdims=True) + RMS_EPS) * g


def _norm_mod(xv, g, shift, scale):
    return _rms(xv, g) * (1.0 + scale) + shift


def _mod_kernel(s_ref, w_ref, b_ref, o_ref):
    o_ref[0, 0] = _dot(_silu(s_ref[...]), w_ref[0]) + b_ref[0, 0]


def _modulation(s, w_mod, b_mod):
    depth, d, _ = w_mod.shape
    rows = s.shape[0]
    return pl.pallas_call(
        _mod_kernel,
        grid=(depth, N_MOD),
        in_specs=[pl.BlockSpec((rows, d), lambda l, j: (0, 0)),
                  pl.BlockSpec((1, d, d), lambda l, j: (l, 0, j)),
                  pl.BlockSpec((1, 1, 1, d), lambda l, j: (l, j, 0, 0))],
        out_specs=pl.BlockSpec((1, 1, rows, d), lambda l, j: (l, j, 0, 0)),
        out_shape=jax.ShapeDtypeStruct((depth, N_MOD, rows, d), F32),
        compiler_params=_cparams(("arbitrary", "arbitrary")),
        name="modulation",
    )(s, w_mod, b_mod.reshape(depth, N_MOD, 1, d))


def _proj_kernel(x_ref, mod_ref, g_ref, w_ref, o_ref, *, nblk):
    m = mod_ref[0]
    h = _norm_mod(x_ref[0], g_ref[...], m[0:1], m[1:2]).astype(BF16)
    group = 4
    for b0 in range(0, nblk, group):
        nb = min(group, nblk - b0)
        r = jnp.dot(h, w_ref[:, b0 * LANES:(b0 + nb) * LANES], preferred_element_type=F32)
        for k in range(nb):
            o_ref[0, b0 + k] = r[:, k * LANES:(k + 1) * LANES]


def _project(x, mod, gain, w, tile):
    bsz, t, d = x.shape
    nblk = w.shape[1] // LANES
    tile = min(tile, t)
    mod_map = (lambda b, i: (b, 0, 0)) if mod.shape[0] == bsz else (lambda b, i: (0, 0, 0))
    return pl.pallas_call(
        functools.partial(_proj_kernel, nblk=nblk),
        grid=(bsz, t // tile),
        in_specs=[pl.BlockSpec((1, tile, d), lambda b, i: (b, i, 0)),
                  pl.BlockSpec((1, N_MOD, d), mod_map),
                  pl.BlockSpec((1, d), lambda b, i: (0, 0)),
                  pl.BlockSpec((d, nblk * LANES), lambda b, i: (0, 0), pipeline_mode=pl.Buffered(1))],
        out_specs=pl.BlockSpec((1, nblk, tile, LANES), lambda b, i: (b, 0, i, 0)),
        out_shape=jax.ShapeDtypeStruct((bsz, nblk, t, LANES), F32),
        compiler_params=_cparams(("parallel", "parallel")),
        name="project",
    )(x, mod, gain.reshape(1, d), w)


def _mix_out_kernel(x_ref, ma_ref, mb_ref, mod_ref, ng_ref, w_ref, o_ref, *, norm_first):
    m = mod_ref[0]
    pa = [ma_ref[0, k] for k in range(ma_ref.shape[1])]
    pb = [mb_ref[0, k].astype(BF16) for k in range(mb_ref.shape[1])]
    if norm_first:
        pa = [_rms(jnp.concatenate(pa, axis=-1), ng_ref[...]).astype(BF16)]
    else:
        pa = [p.astype(BF16) for p in pa]
    v = jnp.concatenate(pa + pb, axis=-1)
    o_ref[0] = x_ref[0] + m[2:3] * jnp.dot(v, w_ref[...], preferred_element_type=F32)


def _mix_out(x, mix_a, mix_b, mod, norm_gain, w, norm_first, tile):
    bsz, t, d = x.shape
    tile = min(tile, t)
    mod_map = (lambda b, i: (b, 0, 0)) if mod.shape[0] == bsz else (lambda b, i: (0, 0, 0))
    ng = norm_gain.reshape(1, -1)
    mix_spec = lambda a: pl.BlockSpec((1, a.shape[1], tile, LANES), lambda b, i: (b, 0, i, 0))
    return pl.pallas_call(
        functools.partial(_mix_out_kernel, norm_first=norm_first),
        grid=(bsz, t // tile),
        in_specs=[pl.BlockSpec((1, tile, d), lambda b, i: (b, i, 0)),
                  mix_spec(mix_a), mix_spec(mix_b),
                  pl.BlockSpec((1, N_MOD, d), mod_map),
                  pl.BlockSpec(ng.shape, lambda b, i: (0, 0)),
                  pl.BlockSpec(w.shape, lambda b, i: (0, 0), pipeline_mode=pl.Buffered(1))],
        out_specs=pl.BlockSpec((1, tile, d), lambda b, i: (b, i, 0)),
        out_shape=jax.ShapeDtypeStruct((bsz, t, d), F32),
        compiler_params=_cparams(("parallel", "parallel")),
        name="mix_out",
    )(x, mix_a, mix_b, mod, ng, w)


def _ffn_kernel(x_ref, xp_ref, xn_ref, mod_ref, g_ref, fg_ref, wg_ref, wu_ref, cw_ref, cb_ref, wd_ref,
                o_ref, fx_ref, gt_ref, a0_ref, a1_ref, *, tile, halo, tf, grid_conv, period, final_norm):
    i = pl.program_id(1)
    nt = pl.num_programs(1)
    nf = wg_ref.shape[0]
    m = mod_ref[0]
    fx_ref[halo:halo + tile] = _norm_mod(x_ref[0], g_ref[...], m[3:4], m[4:5]).astype(BF16)
    if halo:
        keep_p = jnp.where(i > 0, 1.0, 0.0)
        keep_n = jnp.where(i < nt - 1, 1.0, 0.0)
        fx_ref[0:halo] = (keep_p * _norm_mod(xp_ref[0], g_ref[...], m[3:4], m[4:5])).astype(BF16)
        fx_ref[halo + tile:] = (keep_n * _norm_mod(xn_ref[0], g_ref[...], m[3:4], m[4:5])).astype(BF16)

    def gate(f, dst_ref):
        dst_ref[...] = jnp.dot(fx_ref[...], wg_ref[f], preferred_element_type=F32)

    def column(f, src_ref):
        up = jnp.dot(fx_ref[halo:halo + tile], wu_ref[f], preferred_element_type=F32)
        a = src_ref[...]
        cw = cw_ref[f]
        rows = a.shape[0]
        pos = lax.broadcasted_iota(jnp.int32, a.shape, 0)
        if grid_conv:
            col = pos % GRID_W
            a_m1 = jnp.where(col == 0, 0.0, pltpu.roll(a, 1, axis=0))
            a_p1 = jnp.where(col == GRID_W - 1, 0.0, pltpu.roll(a, rows - 1, axis=0))
            conv = cb_ref[f]
            for dr in range(3):
                lo = dr * GRID_W
                conv = conv + (cw[3 * dr + 0:3 * dr + 1] * a_m1[lo:lo + tile]
                               + cw[3 * dr + 1:3 * dr + 2] * a[lo:lo + tile]
                               + cw[3 * dr + 2:3 * dr + 3] * a_p1[lo:lo + tile])
        else:
            col = pos % period
            a_m1 = jnp.where(col == 0, 0.0, pltpu.roll(a, 1, axis=0))
            a_p1 = jnp.where(col == period - 1, 0.0, pltpu.roll(a, rows - 1, axis=0))
            conv = cb_ref[f] + cw[3:4] * a_m1 + cw[4:5] * a + cw[5:6] * a_p1
        start = f * tf if isinstance(f, int) else pl.multiple_of(f * tf, tf)
        gt_ref[:, pl.ds(start, tf)] = (_silu(conv) * up).astype(BF16)

    def column_pair(p, carry):
        f = 2 * p
        gate(f + 1, a1_ref)
        column(f, a0_ref)
        gate(f + 2, a0_ref)
        column(f + 1, a1_ref)
        return carry

    gate(0, a0_ref)
    pairs = (nf - 1) // 2
    lax.fori_loop(0, pairs, column_pair, 0)
    if nf - 2 * pairs == 2:
        gate(nf - 1, a1_ref)
        column(nf - 2, a0_ref)
        column(nf - 1, a1_ref)
    else:
        column(nf - 1, a0_ref)
    y = x_ref[0] + m[5:6] * jnp.dot(gt_ref[...], wd_ref[...], preferred_element_type=F32)
    if final_norm:
        y = _rms(y, fg_ref[...])
    o_ref[0] = y


def _conv_ffn(x, mod, gain, final_gain, w_gate, w_up, conv_w, conv_b, w_down, grid_conv, final_norm,
              tile, tf):
    shape = x.shape
    period = x.shape[1]
    if not grid_conv and mod.shape[0] == 1 and tile % period == 0 and (x.shape[0] * period) % tile == 0:
        x = x.reshape(-1, tile, x.shape[2])
    bsz, t, d = x.shape
    nf = w_gate.shape[0]
    dff = nf * tf
    tile = min(tile, t)
    halo = GRID_W if grid_conv else 0
    nh = t // GRID_W
    per = tile // GRID_W
    mod_map = (lambda b, i: (b, 0, 0)) if mod.shape[0] == bsz else (lambda b, i: (0, 0, 0))
    resident = lambda a: pl.BlockSpec(a.shape, lambda b, i: (0,) * a.ndim, pipeline_mode=pl.Buffered(1))
    return pl.pallas_call(
        functools.partial(_ffn_kernel, tile=tile, halo=halo, tf=tf, grid_conv=grid_conv, period=period,
                          final_norm=final_norm),
        grid=(bsz, t // tile),
        in_specs=[pl.BlockSpec((1, tile, d), lambda b, i: (b, i, 0)),
                  pl.BlockSpec((1, GRID_W, d), lambda b, i: (b, jnp.maximum(i * per - 1, 0), 0)),
                  pl.BlockSpec((1, GRID_W, d), lambda b, i: (b, jnp.minimum((i + 1) * per, nh - 1), 0)),
                  pl.BlockSpec((1, N_MOD, d), mod_map),
                  pl.BlockSpec((1, d), lambda b, i: (0, 0)),
                  pl.BlockSpec((1, d), lambda b, i: (0, 0)),
                  resident(w_gate), resident(w_up), resident(conv_w), resident(conv_b), resident(w_down)],
        out_specs=pl.BlockSpec((1, tile, d), lambda b, i: (b, i, 0)),
        out_shape=jax.ShapeDtypeStruct((bsz, t, d), F32),
        scratch_shapes=[pltpu.VMEM((tile + 2 * halo, d), BF16), pltpu.VMEM((tile, dff), BF16),
                        pltpu.VMEM((tile + 2 * halo, tf), F32), pltpu.VMEM((tile + 2 * halo, tf), F32)],
        compiler_params=_cparams(("parallel", "parallel")),
        name="conv_ffn",
    )(x, x, x, mod, gain.reshape(1, d), final_gain.reshape(1, d), w_gate, w_up, conv_w, conv_b,
      w_down).reshape(shape)


def _ffn_weights(w_gate, w_up, conv_w, conv_b, w_down, tf):
    d, dff = w_gate.shape
    nf = dff // tf
    tiles = lambda w: w.reshape(w.shape[0], nf, tf).transpose(1, 0, 2)
    return (tiles(w_gate).astype(BF16), tiles(w_up).astype(BF16), tiles(conv_w.reshape(9, dff)),
            tiles(conv_b.reshape(1, dff)), w_down.astype(BF16))


def _conv4_silu(v, w, b):
    t = v.shape[0]
    pos = lax.broadcasted_iota(jnp.int32, v.shape, 0)
    acc = b + w[1:2] * v
    acc = acc + w[0:1] * jnp.where(pos < 1, 0.0, pltpu.roll(v, 1, axis=0))
    acc = acc + w[2:3] * jnp.where(pos >= t - 1, 0.0, pltpu.roll(v, t - 1, axis=0))
    acc = acc + w[3:4] * jnp.where(pos >= t - 2, 0.0, pltpu.roll(v, t - 2, axis=0))
    return acc


def _ssd_kernel(xr_ref, br_ref, cr_ref, dt_ref, z_ref, cwx_ref, cwb_ref, cwc_ref, cbx_ref, cbb_ref, cbc_ref,
                dtb_ref, alog_ref, dvec_ref, h0_ref, y_ref, ht_ref, xs_ref, bs_ref, cs_ref, s_ref, *, t):
    nblk = 4
    nchunk = t // SSD_CHUNK
    for k in range(nblk):
        xs_ref[k] = _silu(_conv4_silu(xr_ref[0, k], cwx_ref[k], cbx_ref[k]))
    bs_ref[...] = _silu(_conv4_silu(br_ref[0, 0], cwb_ref[0], cbb_ref[0]))
    cs_ref[...] = _silu(_conv4_silu(cr_ref[0, 0], cwc_ref[0], cbc_ref[0]))

    dtb = dtb_ref[0]
    a_neg = -jnp.exp(alog_ref[0])
    li = lax.broadcasted_iota(jnp.int32, (SSD_CHUNK, SSD_CHUNK), 0)
    si = lax.broadcasted_iota(jnp.int32, (SSD_CHUNK, SSD_CHUNK), 1)
    lane = lax.broadcasted_iota(jnp.int32, (1, LANES), 1)
    lo_half = lane < SSD_HEAD_DIM

    valid = [li >= si, li <= si]
    tri = [v.astype(F32).astype(BF16) for v in valid]
    for d in (0, 1):
        for k in range(nblk):
            s_ref[d, k] = h0_ref[0, d, 0, k]

    def pair(i, carry, finish):
        jobs = [(0, pl.ds(pl.multiple_of(i * SSD_CHUNK, SSD_CHUNK), SSD_CHUNK)),
                (1, pl.ds(pl.multiple_of((nchunk - 1 - i) * SSD_CHUNK, SSD_CHUNK), SSD_CHUNK))]
        pre = []
        for d, rows in jobs:
            bm = bs_ref[rows, :]
            cm = cs_ref[rows, :]
            dt = _softplus(dt_ref[0, 0, rows, :] + dtb)
            cum = _dot_select(tri[d], dt * a_neg)
            cb = lax.dot_general(cm.astype(BF16), bm.astype(BF16), (((1,), (1,)), ((), ())),
                                 preferred_element_type=F32)
            pre.append((dt, cum, cum.T, dt.T, cb, bm.T.astype(BF16), cm.astype(BF16)))
        for k in range(nblk):
            for (d, rows), (dt, cum, cum_t, dt_t, cb, bm_t, cm16) in zip(jobs, pre):
                end_row = SSD_CHUNK - 1 if d == 0 else 0
                xk = xs_ref[k, rows, :]
                ms = []
                for e in range(2):
                    idx = d * SSD_HPG + 2 * k + e
                    seg = jnp.where(valid[d], cum[:, idx:idx + 1] - cum_t[idx:idx + 1, :], -jnp.inf)
                    ms.append((cb * jnp.exp(seg) * dt_t[idx:idx + 1, :]).astype(BF16))
                x_lo = jnp.where(lo_half, xk, 0.0).astype(BF16)
                x_hi = jnp.where(lo_half, 0.0, xk).astype(BF16)
                yk = (jnp.dot(ms[0], x_lo, preferred_element_type=F32)
                      + jnp.dot(ms[1], x_hi, preferred_element_type=F32))
                i0 = d * SSD_HPG + 2 * k
                ecol = jnp.where(lo_half, cum[:, i0:i0 + 1], cum[:, i0 + 1:i0 + 2])
                dcol = jnp.where(lo_half, dt[:, i0:i0 + 1], dt[:, i0 + 1:i0 + 2])
                tot = jnp.where(lo_half, cum[end_row:end_row + 1, i0:i0 + 1],
                                cum[end_row:end_row + 1, i0 + 1:i0 + 2])
                sk = s_ref[d, k]
                yk = yk + jnp.dot(cm16, sk.astype(BF16), preferred_element_type=F32) * jnp.exp(ecol)
                xw = (xk * dcol * jnp.exp(tot - ecol)).astype(BF16)
                s_ref[d, k] = sk * jnp.exp(tot) + jnp.dot(bm_t, xw, preferred_element_type=F32)
                if finish:
                    ytot = y_ref[0, k, rows, :] + yk + dvec_ref[k] * xk
                    y_ref[0, k, rows, :] = ytot * _silu(z_ref[0, k, rows, :])
                else:
                    y_ref[0, k, rows, :] = yk
        return carry

    lax.fori_loop(0, nchunk // 2, functools.partial(pair, finish=False), 0)
    lax.fori_loop(nchunk // 2, nchunk, functools.partial(pair, finish=True), 0)
    for d in (0, 1):
        for k in range(nblk):
            ht_ref[0, d, 0, k] = s_ref[d, k]


def _ssd(p, h0, conv_w, conv_b, dtb, alog, dvec, *, z_blk0, xbc_blk0, dt_blk0):
    bsz, _, t, _ = p.shape
    assert (t // SSD_CHUNK) % 2 == 0
    g4 = lambda off: (lambda b, g: (b, off // 4 + g, 0, 0))
    g1 = lambda off: (lambda b, g: (b, off + g, 0, 0))
    big = pl.BlockSpec((1, 4, t, LANES), g4(xbc_blk0))
    one = lambda off: pl.BlockSpec((1, 1, t, LANES), g1(off))
    st_spec = pl.BlockSpec((1, 2, 1, 4, SSD_STATE, LANES), lambda b, g: (b, 0, g, 0, 0, 0))
    return pl.pallas_call(
        functools.partial(_ssd_kernel, t=t),
        grid=(bsz, SSD_GROUPS),
        in_specs=[big, one(xbc_blk0 + 8), one(xbc_blk0 + 10), one(dt_blk0),
                  pl.BlockSpec((1, 4, t, LANES), g4(z_blk0)),
                  pl.BlockSpec((4, 4, LANES), lambda b, g: (g, 0, 0)),
                  pl.BlockSpec((1, 4, LANES), lambda b, g: (8 + g, 0, 0)),
                  pl.BlockSpec((1, 4, LANES), lambda b, g: (10 + g, 0, 0)),
                  pl.BlockSpec((4, 1, LANES), lambda b, g: (g, 0, 0)),
                  pl.BlockSpec((1, 1, LANES), lambda b, g: (8 + g, 0, 0)),
                  pl.BlockSpec((1, 1, LANES), lambda b, g: (10 + g, 0, 0)),
                  pl.BlockSpec((1, 1, LANES), lambda b, g: (g, 0, 0)),
                  pl.BlockSpec((1, 1, LANES), lambda b, g: (g, 0, 0)),
                  pl.BlockSpec((4, 1, LANES), lambda b, g: (g, 0, 0)),
                  st_spec],
        out_specs=[pl.BlockSpec((1, 4, t, LANES), lambda b, g: (b, g, 0, 0)), st_spec],
        out_shape=[jax.ShapeDtypeStruct((bsz, 8, t, LANES), F32),
                   jax.ShapeDtypeStruct(h0.shape, F32)],
        scratch_shapes=[pltpu.VMEM((4, t, LANES), F32), pltpu.VMEM((t, LANES), F32),
                        pltpu.VMEM((t, LANES), F32), pltpu.VMEM((2, 4, SSD_STATE, LANES), F32)],
        compiler_params=_cparams(("parallel", "parallel")),
        name="ssd",
    )(p, p, p, p, p, conv_w, conv_w, conv_w, conv_b, conv_b, conv_b, dtb, alog, dvec, h0)


LRU_NB = 2
N_SEG = 8
SCAN_UNROLL = 8


def _seg_rows(t):
    seg = t // N_SEG
    return seg, seg + 8


def _conv4(v, w, b):
    return _conv4_silu(v, w, b)


def _seg_scan(a_ref, b_ref, acc_ref, h_ref, lead, seg, pitch, reverse):
    def step(i, carry):
        out = []
        for (h, acc), ld, rev in zip(carry, lead, reverse):
            j = (seg - 1 - i) if rev else i
            idx = ld + (pl.ds(j, N_SEG, stride=pitch), slice(None))
            a = a_ref[idx]
            h = a * h + b_ref[idx]
            acc = acc * a
            acc_ref[idx] = acc
            h_ref[idx] = h
            out.append((h, acc))
        return tuple(out)

    init = (jnp.zeros((N_SEG, LANES), F32), jnp.ones((N_SEG, LANES), F32))
    return lax.fori_loop(0, seg, step, tuple(init for _ in lead), unroll=SCAN_UNROLL)


def _seg_inputs(h0, end, tot, reverse):
    rows = [None] * N_SEG
    hin = h0
    for s in (range(N_SEG - 1, -1, -1) if reverse else range(N_SEG)):
        rows[s] = hin
        hin = tot[s:s + 1] * hin + end[s:s + 1]
    return rows, hin


def _lru_kernel(u_ref, gy_ref, cw_ref, cb_ref, wa_ref, wi_ref, ba_ref, bi_ref, lam_ref, h0_ref,
                r_ref, ht_ref, uc_ref, a_ref, b_ref, acc_ref, hl_ref, *, t):
    seg, pitch = _seg_rows(t)
    for k in range(LRU_NB):
        uc_ref[k] = _conv4(u_ref[0, k], cw_ref[k], cb_ref[k])
    for d in (1, 0):
        for k in range(LRU_NB):
            c = (-0.5 * LRU_C * math.log2(math.e)) * _softplus(-lam_ref[d, k])
            for s in range(N_SEG):
                u = uc_ref[k, s * seg:(s + 1) * seg, :]
                u16 = u.astype(BF16)
                ta = jnp.tanh(jnp.dot(u16, wa_ref[d, k], preferred_element_type=F32) + ba_ref[d, k])
                ti = jnp.tanh(jnp.dot(u16, wi_ref[d, k], preferred_element_type=F32) + bi_ref[d, k])
                a = jnp.exp2(c * ta + c)
                a_ref[d, k, s * pitch:s * pitch + seg, :] = a
                half_u = 0.5 * u
                gated_u = half_u * ti + half_u
                y = (1.0 - a) * (1.0 + a)
                root = jnp.where(y > 0.0, y * lax.rsqrt(y), 0.0)
                b_ref[d, k, s * pitch:s * pitch + seg, :] = root * gated_u
    chains = [(d, k) for d in (1, 0) for k in range(LRU_NB)]
    scanned = _seg_scan(a_ref, b_ref, acc_ref, hl_ref, chains, seg, pitch, [d == 1 for d, _ in chains])
    for (d, k), (end, tot) in zip(chains, scanned):
        rows, hfin = _seg_inputs(h0_ref[0, d, k], end, tot, reverse=(d == 1))
        ht_ref[0, d, k] = hfin
        for s in range(N_SEG):
            src = slice(s * pitch, s * pitch + seg)
            dst = slice(s * seg, (s + 1) * seg)
            h = hl_ref[d, k, src, :] + acc_ref[d, k, src, :] * rows[s]
            if d == 1:
                r_ref[0, k, dst, :] = h
            else:
                r_ref[0, k, dst, :] = (r_ref[0, k, dst, :] + h) * _gelu_tanh(gy_ref[0, k, dst, :])


def _lru(p, h0, conv_w, conv_b, wa, wi, ba, bi, lam, *, gy_blk0, u_blk0):
    bsz, _, t, _ = p.shape
    nb = LRU_NB
    seg, pitch = _seg_rows(t)
    blk = lambda off: pl.BlockSpec((1, nb, t, LANES), lambda b, g: (b, off // nb + g, 0, 0))
    par = lambda shape: pl.BlockSpec(shape, lambda b, g: (0, g) + (0,) * (len(shape) - 2))
    st_spec = pl.BlockSpec((1, 2, nb, 1, LANES), lambda b, g: (b, 0, g, 0, 0))
    return pl.pallas_call(
        functools.partial(_lru_kernel, t=t),
        grid=(bsz, LRU_BLOCKS // nb),
        in_specs=[blk(u_blk0), blk(gy_blk0),
                  pl.BlockSpec((nb, 4, LANES), lambda b, g: (g, 0, 0)),
                  pl.BlockSpec((nb, 1, LANES), lambda b, g: (g, 0, 0)),
                  par((2, nb, LANES, LANES)), par((2, nb, LANES, LANES)),
                  par((2, nb, 1, LANES)), par((2, nb, 1, LANES)), par((2, nb, 1, LANES)),
                  st_spec],
        out_specs=[pl.BlockSpec((1, nb, t, LANES), lambda b, g: (b, g, 0, 0)), st_spec],
        out_shape=[jax.ShapeDtypeStruct((bsz, LRU_BLOCKS, t, LANES), F32),
                   jax.ShapeDtypeStruct(h0.shape, F32)],
        scratch_shapes=[pltpu.VMEM((nb, t, LANES), F32)] + [pltpu.VMEM((2, nb, N_SEG * pitch, LANES), F32)] * 4,
        compiler_params=_cparams(("parallel", "parallel")),
        name="rglru",
    )(p, p, conv_w, conv_b, wa, wi, ba, bi, lam, h0)


EV_Z0, EV_XBC0, EV_DT0, EV_GY0, EV_U0, EV_NBLK = 0, 8, 20, 22, 30, 38


def _blocks(v, n):
    return v.reshape(n, 1, LANES)


def _per_group_heads(v):
    v = v.reshape(2, SSD_GROUPS, SSD_HPG).transpose(1, 0, 2).reshape(SSD_GROUPS, 2 * SSD_HPG)
    return jnp.pad(v, ((0, 0), (0, LANES - 2 * SSD_HPG))).reshape(SSD_GROUPS, 1, LANES)


def _even_weights(w_in, conv_w, conv_b, dt_bias, a_log, ssd_d, lru_conv_w, lru_conv_b, w_a, b_a, w_i, b_i, lam):
    d = w_in.shape[0]
    z, xbc, dt, gy, u = jnp.split(w_in, (1024, 2560, 2592, 3616), axis=1)
    dt = dt.reshape(d, 2, SSD_GROUPS, SSD_HPG).transpose(0, 2, 1, 3).reshape(d, SSD_GROUPS, 2 * SSD_HPG)
    dt = jnp.pad(dt, ((0, 0), (0, 0), (0, LANES - 2 * SSD_HPG))).reshape(d, SSD_GROUPS * LANES)
    return dict(
        w_in=jnp.concatenate([z, xbc, dt, gy, u], axis=1).astype(BF16),
        conv_w=conv_w.reshape(4, 12, LANES).transpose(1, 0, 2), conv_b=_blocks(conv_b, 12),
        dtb=_per_group_heads(dt_bias), alog=_per_group_heads(a_log),
        dvec=_blocks(jnp.repeat(ssd_d, SSD_HEAD_DIM), 8),
        lru_conv_w=lru_conv_w.reshape(4, LRU_BLOCKS, LANES).transpose(1, 0, 2),
        lru_conv_b=_blocks(lru_conv_b, LRU_BLOCKS),
        w_a=(0.5 * w_a).astype(BF16), w_i=(0.5 * w_i).astype(BF16),
        b_a=0.5 * b_a.reshape(2, LRU_BLOCKS, 1, LANES), b_i=0.5 * b_i.reshape(2, LRU_BLOCKS, 1, LANES),
        lam=lam.reshape(2, LRU_BLOCKS, 1, LANES))


def _even_scans(p, states, w):
    ssd_h, lru_h = states
    y, ssd_h = _ssd(p, ssd_h, w["conv_w"], w["conv_b"], w["dtb"], w["alog"], w["dvec"],
                    z_blk0=EV_Z0, xbc_blk0=EV_XBC0, dt_blk0=EV_DT0)
    r, lru_h = _lru(p, lru_h, w["lru_conv_w"], w["lru_conv_b"], w["w_a"], w["w_i"], w["b_a"], w["b_i"],
                    w["lam"], gy_blk0=EV_GY0, u_blk0=EV_U0)
    return y, r, (ssd_h, lru_h)


def _even_zero_states(bsz):
    return (jnp.zeros((bsz, 2, SSD_GROUPS, 4, SSD_STATE, LANES), F32),
            jnp.zeros((bsz, 2, LRU_BLOCKS, 1, LANES), F32))


HG_CHUNK = 128
HG_PAR = 3
SUBLANES = 8


def _group_boundary(cum, c, reverse):
    m = c // 2
    off = m if reverse else m - 1
    n = cum.shape[0]
    if c >= 2 * SUBLANES:
        r = cum.reshape(n // c, c, LANES)
        return jnp.broadcast_to(r[:, off:off + 1, :], r.shape).reshape(n, LANES)
    r = cum.reshape(n // SUBLANES, SUBLANES, LANES)
    sub = lax.broadcasted_iota(jnp.int32, r.shape, 1)
    p = None
    for g0 in range(0, SUBLANES, c):
        cand = jnp.broadcast_to(r[:, g0 + off:g0 + off + 1, :], r.shape)
        p = cand if p is None else jnp.where(sub >= g0, cand, p)
    return p.reshape(n, LANES)


def _hgrn_masks():
    l = np.arange(HG_CHUNK)[:, None]
    s = np.arange(HG_CHUNK)[None, :]
    fwd = []
    size = HG_CHUNK
    while size >= 2:
        half = size // 2
        fwd.append((l // size == s // size) & (l % size >= half) & (s % size < half))
        size = half
    fwd = np.stack(fwd).astype(np.float32)
    pairs = np.stack([fwd, fwd.transpose(0, 2, 1)])
    tri = np.stack([l >= s, l <= s]).astype(np.float32)
    sign = np.stack([np.where(pairs[d].any(axis=2, keepdims=True), 1.0, -1.0) for d in range(2)])
    sign = np.broadcast_to(sign, pairs.shape).astype(np.float32)
    return jnp.asarray(pairs), jnp.asarray(tri, dtype=BF16), jnp.asarray(sign)


def _hgrn_chunks(jobs, q_ref, ff_ref, fb_ref, v_ref, lb_ref, sf_ref, sb_ref, pairs_ref, tri_ref, sign_ref):
    n = len(jobs)
    qq, kk, vv, cum = [], [], [], []
    for hd, reverse, rows in jobs:
        lb = lb_ref[hd]
        fx = (fb_ref if reverse else ff_ref)[0, hd, rows, :]
        e = jnp.exp(-jnp.abs(fx))
        big = 1.0 / (1.0 + e)
        small = e * big
        pos = fx >= 0.0
        log2_f = jnp.log2(lb + (1.0 - lb) * jnp.where(pos, big, small))
        kk.append((1.0 - lb) * jnp.where(pos, small, big))
        cum.append(_dot_select(tri_ref[1 if reverse else 0], log2_f))
    for hd, reverse, rows in jobs:
        qq.append(_silu(q_ref[0, hd, rows, :]))
        vv.append(v_ref[0, hd, rows, :])
    att = [None] * n
    q16 = [v.astype(BF16) for v in qq]
    k16 = [v.astype(BF16) for v in kk]
    size = HG_CHUNK
    level = 0
    while size >= 2:
        for j, (hd, reverse, rows) in enumerate(jobs):
            dist = (cum[j] - _group_boundary(cum[j], size, reverse)) * sign_ref[1 if reverse else 0, level]
            fac = jnp.exp2(dist).astype(BF16)
            a_l = lax.dot_general(q16[j] * fac, k16[j] * fac, (((1,), (1,)), ((), ())),
                                  preferred_element_type=F32)
            a_l = a_l * pairs_ref[1 if reverse else 0, level]
            att[j] = a_l if att[j] is None else att[j] + a_l
        size //= 2
        level += 1
    outs = []
    for j, (hd, reverse, rows) in enumerate(jobs):
        s_ref = (sb_ref if reverse else sf_ref).at[hd]
        sv = s_ref[...]
        diag = jnp.sum(qq[j] * kk[j], axis=-1, keepdims=True)
        o = jnp.dot(att[j].astype(BF16), vv[j].astype(BF16), preferred_element_type=F32) + diag * vv[j]
        o = o + jnp.dot((qq[j] * jnp.exp2(cum[j])).astype(BF16), sv.astype(BF16), preferred_element_type=F32)
        end_row = 0 if reverse else HG_CHUNK - 1
        cum_end = cum[j][end_row:end_row + 1, :]
        kw = (kk[j] * jnp.exp2(cum_end - cum[j])).T.astype(BF16)
        keep = jnp.broadcast_to(jnp.exp2(cum_end), (HG_CHUNK, LANES)).T
        s_ref[...] = sv * keep + jnp.dot(kw, vv[j].astype(BF16), preferred_element_type=F32)
        outs.append(o)
    return outs


def _hgrn_kernel(q_ref, ff_ref, fb_ref, v_ref, g_ref, lb_ref, ng_ref, pairs_ref, tri_ref, sign_ref, h0_ref,
                 o_ref, ht_ref, sf_ref, sb_ref, *, t):
    nchunk = t // HG_CHUNK
    for hd in range(HG_PAR):
        sf_ref[hd] = h0_ref[0, 0, hd]
        sb_ref[hd] = h0_ref[0, 1, hd]

    def pair(i, carry, second):
        jobs = []
        for hd in range(HG_PAR):
            for reverse in (False, True):
                c = (nchunk - 1 - i) if reverse else i
                jobs.append((hd, reverse, pl.ds(pl.multiple_of(c * HG_CHUNK, HG_CHUNK), HG_CHUNK)))
        outs = _hgrn_chunks(jobs, q_ref, ff_ref, fb_ref, v_ref, lb_ref, sf_ref, sb_ref, pairs_ref, tri_ref,
                            sign_ref)
        for (hd, reverse, rows), o in zip(jobs, outs):
            o_ref[0, hd, rows, :] = (o_ref[0, hd, rows, :] + o) if second else o
        return carry

    def finish(c, carry):
        rows = pl.ds(pl.multiple_of(c * HG_CHUNK, HG_CHUNK), HG_CHUNK)
        for hd in range(HG_PAR):
            o_ref[0, hd, rows, :] = _rms(o_ref[0, hd, rows, :], ng_ref[hd]) * _silu(g_ref[0, hd, rows, :])
        return carry

    lax.fori_loop(0, nchunk // 2, functools.partial(pair, second=False), 0)
    lax.fori_loop(nchunk // 2, nchunk, functools.partial(pair, second=True), 0)
    lax.fori_loop(0, nchunk, finish, 0)
    for hd in range(HG_PAR):
        ht_ref[0, 0, hd] = sf_ref[hd]
        ht_ref[0, 1, hd] = sb_ref[hd]


def _hgrn(p, h0, lb, norm_g):
    bsz, _, t, _ = p.shape
    assert (t // HG_CHUNK) % 2 == 0
    hp = HG_PAR
    blk = lambda off: pl.BlockSpec((1, hp, t, LANES), lambda b, h: (b, off // hp + h, 0, 0))
    par = pl.BlockSpec((hp, 1, LANES), lambda b, h: (h, 0, 0))
    st_spec = pl.BlockSpec((1, 2, hp, LANES, LANES), lambda b, h: (b, 0, h, 0, 0))
    pairs, tri, sign = _hgrn_masks()
    return pl.pallas_call(
        functools.partial(_hgrn_kernel, t=t),
        grid=(bsz, HG_HEADS // hp),
        in_specs=[blk(0), blk(HG_HEADS), blk(2 * HG_HEADS), blk(3 * HG_HEADS), blk(4 * HG_HEADS),
                  par, par,
                  pl.BlockSpec(pairs.shape, lambda b, h: (0, 0, 0, 0)),
                  pl.BlockSpec(tri.shape, lambda b, h: (0, 0, 0)),
                  pl.BlockSpec(sign.shape, lambda b, h: (0, 0, 0, 0)),
                  st_spec],
        out_specs=[pl.BlockSpec((1, hp, t, LANES), lambda b, h: (b, h, 0, 0)), st_spec],
        out_shape=[jax.ShapeDtypeStruct((bsz, HG_HEADS, t, LANES), F32),
                   jax.ShapeDtypeStruct(h0.shape, F32)],
        scratch_shapes=[pltpu.VMEM((hp, LANES, LANES), F32), pltpu.VMEM((hp, LANES, LANES), F32)],
        compiler_params=_cparams(("parallel", "parallel")),
        name="hgrn2",
    )(p, p, p, p, p, lb.reshape(HG_HEADS, 1, LANES), norm_g.reshape(HG_HEADS, 1, LANES), pairs, tri, sign,
      h0)


S5_W = S5_GROUPS * S5_GROUP_CH
S5_NSTATE = S5_GROUPS * S5_STATE
S5_NB = S5_NSTATE // LANES
S5_PAR = 2


def _s5_kernel(u_ref, are_ref, aim_ref, pw_ref, bd_ref, cd_ref, dvec_ref, gw_ref, gb_ref, h0_ref,
               y_ref, ht_ref, wr_ref, wi_ref, hr_ref, hi_ref, acc_ref, *, t):
    seg, pitch = _seg_rows(t)
    u = jnp.concatenate([u_ref[0, 0], u_ref[0, 1]], axis=-1)
    u16 = u.astype(BF16)
    acc_ref[...] = dvec_ref[...] * u
    chains = [(d, k) for d in (0, 1) for k in range(S5_PAR)]
    for j0 in range(0, S5_NB, S5_PAR):
        cols = slice(j0 * LANES, (j0 + S5_PAR) * LANES)
        group = max(1, LANES // seg)
        for s0 in range(0, N_SEG, group):
            us = u16[s0 * seg:(s0 + group) * seg]
            wr = jnp.dot(us, bd_ref[0, :, cols], preferred_element_type=F32)
            wi = jnp.dot(us, bd_ref[1, :, cols], preferred_element_type=F32)
            for s in range(s0, s0 + group):
                part = slice((s - s0) * seg, (s - s0 + 1) * seg)
                for k in range(S5_PAR):
                    wr_ref[k, s * pitch:s * pitch + seg, :] = wr[part, k * LANES:(k + 1) * LANES]
                    wi_ref[k, s * pitch:s * pitch + seg, :] = wi[part, k * LANES:(k + 1) * LANES]
        ar = [jnp.broadcast_to(are_ref[d, j0 + k], (N_SEG, LANES)) for d, k in chains]
        ai = [jnp.broadcast_to(aim_ref[d, j0 + k], (N_SEG, LANES)) for d, k in chains]

        def step(i, carry, ar=ar, ai=ai):
            out = []
            for c, (d, k) in enumerate(chains):
                gr, gi = carry[c]
                j = (seg - 1 - i) if d == 1 else i
                src = (k, pl.ds(j, N_SEG, stride=pitch), slice(None))
                dst = (d, k, pl.ds(j, N_SEG, stride=pitch), slice(None))
                ngr = ar[c] * gr - ai[c] * gi + wr_ref[src]
                ngi = ar[c] * gi + ai[c] * gr + wi_ref[src]
                hr_ref[dst] = ngr
                hi_ref[dst] = ngi
                out.append((ngr, ngi))
            return tuple(out)

        zero = jnp.zeros((N_SEG, LANES), F32)
        fin = lax.fori_loop(0, seg, step, tuple((zero, zero) for _ in chains), unroll=SCAN_UNROLL)
        for d in (0, 1):
            ins = []
            last = 0 if d == 1 else seg - 1
            for k in range(S5_PAR):
                end_r, end_i = fin[d * S5_PAR + k]
                tot_r = pw_ref[d, 0, j0 + k, last:last + 1, :]
                tot_i = pw_ref[d, 1, j0 + k, last:last + 1, :]
                hin_r, hin_i = h0_ref[0, d, 0, j0 + k], h0_ref[0, d, 1, j0 + k]
                rows = [None] * N_SEG
                for s in (range(N_SEG - 1, -1, -1) if d == 1 else range(N_SEG)):
                    rows[s] = (hin_r, hin_i)
                    hin_r, hin_i = (tot_r[0:1] * hin_r - tot_i[0:1] * hin_i + end_r[s:s + 1],
                                    tot_r[0:1] * hin_i + tot_i[0:1] * hin_r + end_i[s:s + 1])
                ht_ref[0, d, 0, j0 + k] = hin_r
                ht_ref[0, d, 1, j0 + k] = hin_i
                ins.append(rows)
            gr_rows, gi_rows = [], []
            for s in range(N_SEG):
                src = slice(s * pitch, s * pitch + seg)
                gr_parts, gi_parts = [], []
                for k in range(S5_PAR):
                    in_r, in_i = ins[k][s]
                    pr, pi = pw_ref[d, 0, j0 + k], pw_ref[d, 1, j0 + k]
                    gr_parts.append((hr_ref[d, k, src, :] + pr * in_r - pi * in_i).astype(BF16))
                    gi_parts.append((hi_ref[d, k, src, :] + pr * in_i + pi * in_r).astype(BF16))
                gr_rows.append(jnp.concatenate(gr_parts, axis=-1))
                gi_rows.append(jnp.concatenate(gi_parts, axis=-1))
                if len(gr_rows) == group:
                    rows = slice((s + 1 - group) * seg, (s + 1) * seg)
                    acc_ref[rows, :] += (
                        jnp.dot(jnp.concatenate(gr_rows, axis=0), cd_ref[d, 0, cols, :],
                                preferred_element_type=F32)
                        + jnp.dot(jnp.concatenate(gi_rows, axis=0), cd_ref[d, 1, cols, :],
                                  preferred_element_type=F32))
                    gr_rows, gi_rows = [], []
    y = _gelu_tanh(acc_ref[...])
    y = y * _sigmoid(jnp.dot(y.astype(BF16), gw_ref[...], preferred_element_type=F32) + gb_ref[...])
    y_ref[0, 0] = y[:, :LANES]
    y_ref[0, 1] = y[:, LANES:]


def _s5(p, h0, log_a, bd, cd, dvec, glu_w, glu_b, *, u_blk0):
    bsz, _, t, _ = p.shape
    seg, pitch = _seg_rows(t)
    la_re, la_im = log_a
    pos = jnp.arange(seg, dtype=F32)
    n = jnp.stack([pos + 1.0, seg - pos]).reshape(2, 1, seg, 1)
    mag = jnp.exp(n * la_re)
    pw = jnp.stack([mag * jnp.cos(n * la_im), mag * jnp.sin(n * la_im)], axis=1)
    a_re, a_im = pw[0, 0, :, 0:1, :], pw[0, 1, :, 0:1, :]
    a_re = jnp.stack([a_re, pw[1, 0, :, seg - 1:seg, :]])
    a_im = jnp.stack([a_im, pw[1, 1, :, seg - 1:seg, :]])
    full = lambda a: pl.BlockSpec(a.shape, lambda b: (0,) * a.ndim)
    st_spec = pl.BlockSpec((1, 2, 2, S5_NB, 1, LANES), lambda b: (b, 0, 0, 0, 0, 0))
    return pl.pallas_call(
        functools.partial(_s5_kernel, t=t),
        grid=(bsz,),
        in_specs=[pl.BlockSpec((1, 2, t, LANES), lambda b: (b, u_blk0 // 2, 0, 0)),
                  full(a_re), full(a_im), full(pw), full(bd), full(cd), full(dvec), full(glu_w), full(glu_b),
                  st_spec],
        out_specs=[pl.BlockSpec((1, 2, t, LANES), lambda b: (b, 0, 0, 0)), st_spec],
        out_shape=[jax.ShapeDtypeStruct((bsz, 2, t, LANES), F32), jax.ShapeDtypeStruct(h0.shape, F32)],
        scratch_shapes=[pltpu.VMEM((S5_PAR, N_SEG * pitch, LANES), F32),
                        pltpu.VMEM((S5_PAR, N_SEG * pitch, LANES), F32),
                        pltpu.VMEM((2, S5_PAR, N_SEG * pitch, LANES), F32),
                        pltpu.VMEM((2, S5_PAR, N_SEG * pitch, LANES), F32),
                        pltpu.VMEM((t, S5_W), F32)],
        compiler_params=_cparams(("parallel",)),
        name="s5",
    )(p, a_re, a_im, pw, bd, cd, dvec, glu_w, glu_b, h0)


def _s5_params(lam_re, lam_im, log_step, b_re, b_im, c_re, c_im):
    step = jnp.exp(log_step)[..., None]
    mag = jnp.exp(lam_re * step)
    ar, ai = mag * jnp.cos(lam_im * step), mag * jnp.sin(lam_im * step)
    den = lam_re * lam_re + lam_im * lam_im
    zr = ((ar - 1) * lam_re + ai * lam_im) / den
    zi = (ai * lam_re - (ar - 1) * lam_im) / den
    czr = c_re * zr[:, :, None, :] - c_im * zi[:, :, None, :]
    czi = c_re * zi[:, :, None, :] + c_im * zr[:, :, None, :]
    eye = jnp.eye(S5_GROUPS, dtype=F32)

    def in_mat(m):
        return jnp.einsum("gpk,gh->gkhp", m, eye).reshape(S5_W, S5_NSTATE)

    def out_mat(m):
        return jnp.einsum("dgkp,gh->dgphk", m, eye).reshape(2, S5_NSTATE, S5_W)

    bd = jnp.stack([in_mat(b_re), in_mat(b_im)], axis=0).astype(BF16)
    cd = jnp.stack([out_mat(czr), -out_mat(czi)], axis=1).astype(BF16)
    shape = (2, S5_NB, 1, LANES)
    log_a = ((lam_re * step).reshape(shape), (lam_im * step).reshape(shape))
    return log_a, bd, cd


OD_U0 = 5 * HG_HEADS


def _odd_weights(w_in, lower_bound, hg_norm_g, lam_re, lam_im, log_step, b_re, b_im, c_re, c_im, s5_d,
                 glu_w, glu_b):
    log_a, bd, cd = _s5_params(lam_re, lam_im, log_step, b_re, b_im, c_re, c_im)
    return dict(w_in=w_in.astype(BF16), lb=lower_bound, norm_g=hg_norm_g, log_a=log_a, bd=bd, cd=cd,
                dvec=s5_d.reshape(1, S5_W), glu_w=glu_w.astype(BF16), glu_b=glu_b.reshape(1, S5_W))


def _odd_scans(p, states, w):
    hg_h, s5_h = states
    o, hg_h = _hgrn(p, hg_h, w["lb"], w["norm_g"])
    y, s5_h = _s5(p, s5_h, w["log_a"], w["bd"], w["cd"], w["dvec"], w["glu_w"], w["glu_b"], u_blk0=OD_U0)
    return o, y, (hg_h, s5_h)


def _odd_zero_states(bsz):
    return (jnp.zeros((bsz, 2, HG_HEADS, LANES, LANES), F32),
            jnp.zeros((bsz, 2, 2, S5_NB, 1, LANES), F32))


PROJ_TILE = 256
MIX_TILE = 512
FFN_TILE = 1024
FFN_COLS = 256


def kernel(x, c, ctx, c_ctx, w_mod, b_mod, norm_mix_g, norm_ffn_g, final_norm_g,
           ev_w_in, ev_w_out, ssd_conv_w, ssd_conv_b, ssd_dt_bias, ssd_a_log, ssd_d, ssd_norm_g,
           lru_conv_w, lru_conv_b, lru_w_a, lru_b_a, lru_w_i, lru_b_i, lru_lam,
           od_w_in, od_w_out, hg_lb_logits, hg_norm_g,
           s5_lam_re, s5_lam_im, s5_log_step, s5_b_re, s5_b_im, s5_c_re, s5_c_im, s5_d,
           s5_glu_w, s5_glu_b,
           ffn_w_gate, ffn_w_up, ffn_conv_w, ffn_conv_b, ffn_w_down):
    bsz, _, d = x.shape
    depth = w_mod.shape[0]
    prob = jax.nn.softmax(hg_lb_logits.astype(F32), axis=0)
    lower_bounds = (jnp.cumsum(prob, axis=0) - prob[0]).astype(hg_lb_logits.dtype)

    pad = (-(bsz + 1)) % SUBLANES
    cond = jnp.concatenate([c, c_ctx[None], jnp.zeros((pad, d), c.dtype)], axis=0)
    mods = _modulation(cond, w_mod, b_mod).transpose(0, 2, 1, 3)

    for layer in range(depth):
        last = layer == depth - 1
        j = layer // 2
        mod_x = mods[layer, :bsz]
        mod_c = mods[layer, bsz:bsz + 1]
        if layer % 2 == 0:
            w = _even_weights(ev_w_in[j], ssd_conv_w[j], ssd_conv_b[j], ssd_dt_bias[j], ssd_a_log[j], ssd_d[j],
                              lru_conv_w[j], lru_conv_b[j], lru_w_a[j], lru_b_a[j], lru_w_i[j], lru_b_i[j],
                              lru_lam[j])
            scans, zero_states = _even_scans, _even_zero_states
            w_out, norm_gain, norm_first = ev_w_out[j].astype(BF16), ssd_norm_g[j], True
        else:
            w = _odd_weights(od_w_in[j], lower_bounds[layer], hg_norm_g[j], s5_lam_re[j], s5_lam_im[j],
                             s5_log_step[j], s5_b_re[j], s5_b_im[j], s5_c_re[j], s5_c_im[j], s5_d[j],
                             s5_glu_w[j], s5_glu_b[j])
            scans, zero_states = _odd_scans, _odd_zero_states
            w_out, norm_gain, norm_first = od_w_out[j].astype(BF16), hg_norm_g[j], False
        ffn_w = _ffn_weights(ffn_w_gate[layer], ffn_w_up[layer], ffn_conv_w[layer], ffn_conv_b[layer],
                             ffn_w_down[layer], FFN_COLS)

        p_c = _project(ctx, mod_c, norm_mix_g[layer], w["w_in"], PROJ_TILE)
        a_c, b_c, states = scans(p_c, zero_states(bsz), w)
        p_x = _project(x, mod_x, norm_mix_g[layer], w["w_in"], PROJ_TILE)
        a_x, b_x, _ = scans(p_x, states, w)
        x = _mix_out(x, a_x, b_x, mod_x, norm_gain, w_out, norm_first, MIX_TILE)
        x = _conv_ffn(x, mod_x, norm_ffn_g[layer], final_norm_g, *ffn_w, grid_conv=True, final_norm=last,
                      tile=FFN_TILE, tf=FFN_COLS)
        if not last:
            ctx = _mix_out(ctx, a_c, b_c, mod_c, norm_gain, w_out, norm_first, MIX_TILE)
            ctx = _conv_ffn(ctx, mod_c, norm_ffn_g[layer], final_norm_g, *ffn_w, grid_conv=False,
                            final_norm=False, tile=FFN_TILE, tf=FFN_COLS)
    return x
```

```python
import functools
import math

import jax
import jax.numpy as jnp
import numpy as np
from jax import lax
from jax.experimental import pallas as pl
from jax.experimental.pallas import tpu as pltpu

LANES = 128
RMS_EPS = 1e-6
N_MOD = 6
GRID_W = 64
SSD_HEAD_DIM = 64
SSD_HEADS = 16
SSD_GROUPS = 2
SSD_HPG = SSD_HEADS // SSD_GROUPS
SSD_STATE = 128
SSD_CHUNK = 128
SSD_W = SSD_HEADS * SSD_HEAD_DIM
SSD_XBC = SSD_W + 2 * SSD_GROUPS * SSD_STATE
SSD_W_BLKS, SSD_XBC_BLKS = SSD_W // LANES, SSD_XBC // LANES
SSD_GBLK = SSD_W_BLKS // SSD_GROUPS
SSD_B0, SSD_C0 = SSD_W_BLKS, SSD_W_BLKS + SSD_GROUPS
LRU_BLOCKS = 8
LRU_C = 8.0
HG_HEADS = 6
S5_GROUPS = 16
S5_GROUP_CH = 16
S5_STATE = 64
VMEM_LIMIT = 56 * 1024 * 1024

BF16 = jnp.bfloat16
F32 = jnp.float32


def _cparams(sem):
    return pltpu.CompilerParams(dimension_semantics=sem, vmem_limit_bytes=VMEM_LIMIT)


def _dot(a, b):
    return jnp.dot(a.astype(BF16), b.astype(BF16), preferred_element_type=F32)


def _dot_select(sel, v):
    hi = v.astype(BF16)
    rest = v - hi.astype(F32)
    mid = rest.astype(BF16)
    lo = (rest - mid.astype(F32)).astype(BF16)
    return (jnp.dot(sel, hi, preferred_element_type=F32) + jnp.dot(sel, mid, preferred_element_type=F32)
            + jnp.dot(sel, lo, preferred_element_type=F32))


def _sigmoid(v):
    return 0.5 * jnp.tanh(0.5 * v) + 0.5


def _silu(v):
    return v * _sigmoid(v)


def _gelu_tanh(v):
    return 0.5 * v * (1.0 + jnp.tanh(math.sqrt(2.0 / math.pi) * (v + 0.044715 * (v * v * v))))


def _softplus(v):
    return jnp.maximum(v, 0.0) + jnp.log(1.0 + jnp.exp(-jnp.abs(v)))


def _rms(v, g):
    return v * lax.rsqrt(jnp.mean(v * v, axis=-1, keepdims=True) + RMS_EPS) * g


def _norm_mod(xv, g, shift, scale):
    return _rms(xv, g) * (1.0 + scale) + shift


def _mod_kernel(s_ref, w_ref, b_ref, o_ref):
    o_ref[0, 0] = _dot(_silu(s_ref[...]), w_ref[0]) + b_ref[0, 0]


def _modulation(s, w_mod, b_mod):
    depth, d, _ = w_mod.shape
    rows = s.shape[0]
    return pl.pallas_call(
        _mod_kernel,
        grid=(depth, N_MOD),
        in_specs=[pl.BlockSpec((rows, d), lambda l, j: (0, 0)),
                  pl.BlockSpec((1, d, d), lambda l, j: (l, 0, j)),
                  pl.BlockSpec((1, 1, 1, d), lambda l, j: (l, j, 0, 0))],
        out_specs=pl.BlockSpec((1, 1, rows, d), lambda l, j: (l, j, 0, 0)),
        out_shape=jax.ShapeDtypeStruct((depth, N_MOD, rows, d), F32),
        compiler_params=_cparams(("arbitrary", "arbitrary")),
        name="modulation",
    )(s, w_mod, b_mod.reshape(depth, N_MOD, 1, d))


def _proj_kernel(x_ref, mod_ref, g_ref, w_ref, o_ref, *, nblk):
    m = mod_ref[0]
    h = _norm_mod(x_ref[0], g_ref[...], m[0:1], m[1:2]).astype(BF16)
    group = 4
    for b0 in range(0, nblk, group):
        nb = min(group, nblk - b0)
        r = jnp.dot(h, w_ref[:, b0 * LANES:(b0 + nb) * LANES], preferred_element_type=F32)
        for k in range(nb):
            o_ref[0, b0 + k] = r[:, k * LANES:(k + 1) * LANES]


def _project(x, mod, gain, w, tile):
    bsz, t, d = x.shape
    nblk = w.shape[1] // LANES
    tile = min(tile, t)
    mod_map = (lambda b, i: (b, 0, 0)) if mod.shape[0] == bsz else (lambda b, i: (0, 0, 0))
    return pl.pallas_call(
        functools.partial(_proj_kernel, nblk=nblk),
        grid=(bsz, t // tile),
        in_specs=[pl.BlockSpec((1, tile, d), lambda b, i: (b, i, 0)),
                  pl.BlockSpec((1, N_MOD, d), mod_map),
                  pl.BlockSpec((1, d), lambda b, i: (0, 0)),
                  pl.BlockSpec((d, nblk * LANES), lambda b, i: (0, 0), pipeline_mode=pl.Buffered(1))],
        out_specs=pl.BlockSpec((1, nblk, tile, LANES), lambda b, i: (b, 0, i, 0)),
        out_shape=jax.ShapeDtypeStruct((bsz, nblk, t, LANES), F32),
        compiler_params=_cparams(("parallel", "parallel")),
        name="project",
    )(x, mod, gain.reshape(1, d), w)


def _mix_out_kernel(x_ref, ma_ref, mb_ref, mod_ref, ng_ref, w_ref, o_ref, *, norm_first):
    m = mod_ref[0]
    pa = [ma_ref[0, k] for k in range(ma_ref.shape[1])]
    pb = [mb_ref[0, k].astype(BF16) for k in range(mb_ref.shape[1])]
    if norm_first:
        pa = [_rms(jnp.concatenate(pa, axis=-1), ng_ref[...]).astype(BF16)]
    else:
        pa = [p.astype(BF16) for p in pa]
    v = jnp.concatenate(pa + pb, axis=-1)
    o_ref[0] = x_ref[0] + m[2:3] * jnp.dot(v, w_ref[...], preferred_element_type=F32)


def _mix_out(x, mix_a, mix_b, mod, norm_gain, w, norm_first, tile):
    bsz, t, d = x.shape
    tile = min(tile, t)
    mod_map = (lambda b, i: (b, 0, 0)) if mod.shape[0] == bsz else (lambda b, i: (0, 0, 0))
    ng = norm_gain.reshape(1, -1)
    mix_spec = lambda a: pl.BlockSpec((1, a.shape[1], tile, LANES), lambda b, i: (b, 0, i, 0))
    return pl.pallas_call(
        functools.partial(_mix_out_kernel, norm_first=norm_first),
        grid=(bsz, t // tile),
        in_specs=[pl.BlockSpec((1, tile, d), lambda b, i: (b, i, 0)),
                  mix_spec(mix_a), mix_spec(mix_b),
                  pl.BlockSpec((1, N_MOD, d), mod_map),
                  pl.BlockSpec(ng.shape, lambda b, i: (0, 0)),
                  pl.BlockSpec(w.shape, lambda b, i: (0, 0), pipeline_mode=pl.Buffered(1))],
        out_specs=pl.BlockSpec((1, tile, d), lambda b, i: (b, i, 0)),
        out_shape=jax.ShapeDtypeStruct((bsz, t, d), F32),
        compiler_params=_cparams(("parallel", "parallel")),
        name="mix_out",
    )(x, mix_a, mix_b, mod, ng, w)


def _ffn_kernel(x_ref, xp_ref, xn_ref, mod_ref, g_ref, fg_ref, wg_ref, wu_ref, cw_ref, cb_ref, wd_ref,
                o_ref, fx_ref, gt_ref, a0_ref, a1_ref, *, tile, halo, tf, grid_conv, period, final_norm):
    i = pl.program_id(1)
    nt = pl.num_programs(1)
    nf = wg_ref.shape[0]
    m = mod_ref[0]
    fx_ref[halo:halo + tile] = _norm_mod(x_ref[0], g_ref[...], m[3:4], m[4:5]).astype(BF16)
    if halo:
        keep_p = jnp.where(i > 0, 1.0, 0.0)
        keep_n = jnp.where(i < nt - 1, 1.0, 0.0)
        fx_ref[0:halo] = (keep_p * _norm_mod(xp_ref[0], g_ref[...], m[3:4], m[4:5])).astype(BF16)
        fx_ref[halo + tile:] = (keep_n * _norm_mod(xn_ref[0], g_ref[...], m[3:4], m[4:5])).astype(BF16)

    def gate(f, dst_ref):
        dst_ref[...] = jnp.dot(fx_ref[...], wg_ref[f], preferred_element_type=F32)

    def column(f, src_ref):
        a = src_ref[...]
        cw = cw_ref[f]
        rows = a.shape[0]
        pos = lax.broadcasted_iota(jnp.int32, a.shape, 0)
        if grid_conv:
            col = pos % GRID_W
            a_m1 = jnp.where(col == 0, 0.0, pltpu.roll(a, 1, axis=0))
            a_p1 = jnp.where(col == GRID_W - 1, 0.0, pltpu.roll(a, rows - 1, axis=0))
            conv = cb_ref[f]
            for dr in range(3):
                lo = dr * GRID_W
                conv = conv + (cw[3 * dr + 0:3 * dr + 1] * a_m1[lo:lo + tile]
                               + cw[3 * dr + 1:3 * dr + 2] * a[lo:lo + tile]
                               + cw[3 * dr + 2:3 * dr + 3] * a_p1[lo:lo + tile])
        else:
            col = pos % period
            a_m1 = jnp.where(col == 0, 0.0, pltpu.roll(a, 1, axis=0))
            a_p1 = jnp.where(col == period - 1, 0.0, pltpu.roll(a, rows - 1, axis=0))
            conv = cb_ref[f] + cw[3:4] * a_m1 + cw[4:5] * a + cw[5:6] * a_p1
        act = _silu(conv)
        up = jnp.dot(fx_ref[halo:halo + tile], wu_ref[f], preferred_element_type=F32)
        start = f * tf if isinstance(f, int) else pl.multiple_of(f * tf, tf)
        gt_ref[:, pl.ds(start, tf)] = (act * up).astype(BF16)

    def column_pair(p, carry):
        f = 2 * p
        gate(f + 1, a1_ref)
        column(f, a0_ref)
        gate(f + 2, a0_ref)
        column(f + 1, a1_ref)
        return carry

    gate(0, a0_ref)
    pairs = (nf - 1) // 2
    lax.fori_loop(0, pairs, column_pair, 0)
    if nf - 2 * pairs == 2:
        gate(nf - 1, a1_ref)
        column(nf - 2, a0_ref)
        column(nf - 1, a1_ref)
    else:
        column(nf - 1, a0_ref)
    y = x_ref[0] + m[5:6] * jnp.dot(gt_ref[...], wd_ref[...], preferred_element_type=F32)
    if final_norm:
        y = _rms(y, fg_ref[...])
    o_ref[0] = y


def _conv_ffn(x, mod, gain, final_gain, w_gate, w_up, conv_w, conv_b, w_down, grid_conv, final_norm,
              tile, tf):
    shape = x.shape
    period = x.shape[1]
    if not grid_conv and mod.shape[0] == 1 and tile % period == 0 and (x.shape[0] * period) % tile == 0:
        x = x.reshape(-1, tile, x.shape[2])
    bsz, t, d = x.shape
    nf = w_gate.shape[0]
    dff = nf * tf
    tile = min(tile, t)
    halo = GRID_W if grid_conv else 0
    nh = t // GRID_W
    per = tile // GRID_W
    mod_map = (lambda b, i: (b, 0, 0)) if mod.shape[0] == bsz else (lambda b, i: (0, 0, 0))
    resident = lambda a: pl.BlockSpec(a.shape, lambda b, i: (0,) * a.ndim, pipeline_mode=pl.Buffered(1))
    return pl.pallas_call(
        functools.partial(_ffn_kernel, tile=tile, halo=halo, tf=tf, grid_conv=grid_conv, period=period,
                          final_norm=final_norm),
        grid=(bsz, t // tile),
        in_specs=[pl.BlockSpec((1, tile, d), lambda b, i: (b, i, 0)),
                  pl.BlockSpec((1, GRID_W, d), lambda b, i: (b, jnp.maximum(i * per - 1, 0), 0)),
                  pl.BlockSpec((1, GRID_W, d), lambda b, i: (b, jnp.minimum((i + 1) * per, nh - 1), 0)),
                  pl.BlockSpec((1, N_MOD, d), mod_map),
                  pl.BlockSpec((1, d), lambda b, i: (0, 0)),
                  pl.BlockSpec((1, d), lambda b, i: (0, 0)),
                  resident(w_gate), resident(w_up), resident(conv_w), resident(conv_b), resident(w_down)],
        out_specs=pl.BlockSpec((1, tile, d), lambda b, i: (b, i, 0)),
        out_shape=jax.ShapeDtypeStruct((bsz, t, d), F32),
        scratch_shapes=[pltpu.VMEM((tile + 2 * halo, d), BF16), pltpu.VMEM((tile, dff), BF16),
                        pltpu.VMEM((tile + 2 * halo, tf), F32), pltpu.VMEM((tile + 2 * halo, tf), F32)],
        compiler_params=_cparams(("parallel", "parallel")),
        name="conv_ffn",
    )(x, x, x, mod, gain.reshape(1, d), final_gain.reshape(1, d), w_gate, w_up, conv_w, conv_b,
      w_down).reshape(shape)


def _ffn_weights(w_gate, w_up, conv_w, conv_b, w_down, tf):
    d, dff = w_gate.shape
    nf = dff // tf
    tiles = lambda w: w.reshape(w.shape[0], nf, tf).transpose(1, 0, 2)
    return (tiles(w_gate).astype(BF16), tiles(w_up).astype(BF16), tiles(conv_w.reshape(9, dff)),
            tiles(conv_b.reshape(1, dff)), w_down.astype(BF16))


def _conv4(v, w, b):
    t = v.shape[0]
    pos = lax.broadcasted_iota(jnp.int32, v.shape, 0)
    acc = b + w[1:2] * v
    acc = acc + w[0:1] * jnp.where(pos < 1, 0.0, pltpu.roll(v, 1, axis=0))
    acc = acc + w[2:3] * jnp.where(pos >= t - 1, 0.0, pltpu.roll(v, t - 1, axis=0))
    acc = acc + w[3:4] * jnp.where(pos >= t - 2, 0.0, pltpu.roll(v, t - 2, axis=0))
    return acc


def _ssd_kernel(xr_ref, br_ref, cr_ref, dt_ref, z_ref, cwx_ref, cwb_ref, cwc_ref, cbx_ref, cbb_ref, cbc_ref,
                dtb_ref, alog_ref, dvec_ref, h0_ref, y_ref, ht_ref, xs_ref, bs_ref, cs_ref, s_ref, *, t):
    nblk = SSD_GBLK
    nchunk = t // SSD_CHUNK
    for k in range(nblk):
        xs_ref[k] = _silu(_conv4(xr_ref[0, k], cwx_ref[k], cbx_ref[k]))
    bs_ref[...] = _silu(_conv4(br_ref[0, 0], cwb_ref[0], cbb_ref[0]))
    cs_ref[...] = _silu(_conv4(cr_ref[0, 0], cwc_ref[0], cbc_ref[0]))

    dtb = dtb_ref[0]
    a_neg = -jnp.exp(alog_ref[0])
    li = lax.broadcasted_iota(jnp.int32, (SSD_CHUNK, SSD_CHUNK), 0)
    si = lax.broadcasted_iota(jnp.int32, (SSD_CHUNK, SSD_CHUNK), 1)
    lane = lax.broadcasted_iota(jnp.int32, (1, LANES), 1)
    lo_half = lane < SSD_HEAD_DIM

    valid = [li >= si, li <= si]
    tri = [v.astype(F32).astype(BF16) for v in valid]
    for d in (0, 1):
        for k in range(nblk):
            s_ref[d, k] = h0_ref[0, d, 0, k]

    def pair(i, carry, finish):
        jobs = [(0, pl.ds(pl.multiple_of(i * SSD_CHUNK, SSD_CHUNK), SSD_CHUNK)),
                (1, pl.ds(pl.multiple_of((nchunk - 1 - i) * SSD_CHUNK, SSD_CHUNK), SSD_CHUNK))]
        pre = []
        for d, rows in jobs:
            bm = bs_ref[rows, :]
            cm = cs_ref[rows, :]
            dt = _softplus(dt_ref[0, 0, rows, :] + dtb)
            cum = _dot_select(tri[d], dt * a_neg)
            cb = lax.dot_general(cm.astype(BF16), bm.astype(BF16), (((1,), (1,)), ((), ())),
                                 preferred_element_type=F32)
            pre.append((dt, cum, cum.T, dt.T, cb, bm.T.astype(BF16), cm.astype(BF16)))
        for k in range(nblk):
            for (d, rows), (dt, cum, cum_t, dt_t, cb, bm_t, cm16) in zip(jobs, pre):
                end_row = SSD_CHUNK - 1 if d == 0 else 0
                xk = xs_ref[k, rows, :]
                ms = []
                for e in range(2):
                    idx = d * SSD_HPG + 2 * k + e
                    seg = jnp.where(valid[d], cum[:, idx:idx + 1] - cum_t[idx:idx + 1, :], -jnp.inf)
                    ms.append((cb * jnp.exp(seg) * dt_t[idx:idx + 1, :]).astype(BF16))
                x_lo = jnp.where(lo_half, xk, 0.0).astype(BF16)
                x_hi = jnp.where(lo_half, 0.0, xk).astype(BF16)
                yk = (jnp.dot(ms[0], x_lo, preferred_element_type=F32)
                      + jnp.dot(ms[1], x_hi, preferred_element_type=F32))
                i0 = d * SSD_HPG + 2 * k
                ecol = jnp.where(lo_half, cum[:, i0:i0 + 1], cum[:, i0 + 1:i0 + 2])
                dcol = jnp.where(lo_half, dt[:, i0:i0 + 1], dt[:, i0 + 1:i0 + 2])
                tot = jnp.where(lo_half, cum[end_row:end_row + 1, i0:i0 + 1],
                                cum[end_row:end_row + 1, i0 + 1:i0 + 2])
                sk = s_ref[d, k]
                yk = yk + jnp.dot(cm16, sk.astype(BF16), preferred_element_type=F32) * jnp.exp(ecol)
                xw = (xk * dcol * jnp.exp(tot - ecol)).astype(BF16)
                s_ref[d, k] = sk * jnp.exp(tot) + jnp.dot(bm_t, xw, preferred_element_type=F32)
                if finish:
                    ytot = y_ref[0, k, rows, :] + yk + dvec_ref[k] * xk
                    y_ref[0, k, rows, :] = ytot * _silu(z_ref[0, k, rows, :])
                else:
                    y_ref[0, k, rows, :] = yk
        return carry

    lax.fori_loop(0, nchunk // 2, functools.partial(pair, finish=False), 0)
    lax.fori_loop(nchunk // 2, nchunk, functools.partial(pair, finish=True), 0)
    for d in (0, 1):
        for k in range(nblk):
            ht_ref[0, d, 0, k] = s_ref[d, k]


def _ssd(p, h0, conv_w, conv_b, dtb, alog, dvec, *, z_blk0, xbc_blk0, dt_blk0):
    bsz, _, t, _ = p.shape
    assert (t // SSD_CHUNK) % 2 == 0
    nb = SSD_GBLK
    gx = lambda off: (lambda b, g: (b, off // nb + g, 0, 0))
    g1 = lambda off: (lambda b, g: (b, off + g, 0, 0))
    one = lambda off: pl.BlockSpec((1, 1, t, LANES), g1(off))
    st_spec = pl.BlockSpec((1, 2, 1, nb, SSD_STATE, LANES), lambda b, g: (b, 0, g, 0, 0, 0))
    return pl.pallas_call(
        functools.partial(_ssd_kernel, t=t),
        grid=(bsz, SSD_GROUPS),
        in_specs=[pl.BlockSpec((1, nb, t, LANES), gx(xbc_blk0)),
                  one(xbc_blk0 + SSD_B0), one(xbc_blk0 + SSD_C0), one(dt_blk0),
                  pl.BlockSpec((1, nb, t, LANES), gx(z_blk0)),
                  pl.BlockSpec((nb, 4, LANES), lambda b, g: (g, 0, 0)),
                  pl.BlockSpec((1, 4, LANES), lambda b, g: (SSD_B0 + g, 0, 0)),
                  pl.BlockSpec((1, 4, LANES), lambda b, g: (SSD_C0 + g, 0, 0)),
                  pl.BlockSpec((nb, 1, LANES), lambda b, g: (g, 0, 0)),
                  pl.BlockSpec((1, 1, LANES), lambda b, g: (SSD_B0 + g, 0, 0)),
                  pl.BlockSpec((1, 1, LANES), lambda b, g: (SSD_C0 + g, 0, 0)),
                  pl.BlockSpec((1, 1, LANES), lambda b, g: (g, 0, 0)),
                  pl.BlockSpec((1, 1, LANES), lambda b, g: (g, 0, 0)),
                  pl.BlockSpec((nb, 1, LANES), lambda b, g: (g, 0, 0)),
                  st_spec],
        out_specs=[pl.BlockSpec((1, nb, t, LANES), lambda b, g: (b, g, 0, 0)), st_spec],
        out_shape=[jax.ShapeDtypeStruct((bsz, SSD_W_BLKS, t, LANES), F32),
                   jax.ShapeDtypeStruct(h0.shape, F32)],
        scratch_shapes=[pltpu.VMEM((nb, t, LANES), F32), pltpu.VMEM((t, LANES), F32),
                        pltpu.VMEM((t, LANES), F32), pltpu.VMEM((2, nb, SSD_STATE, LANES), F32)],
        compiler_params=_cparams(("parallel", "parallel")),
        name="ssd",
    )(p, p, p, p, p, conv_w, conv_w, conv_w, conv_b, conv_b, conv_b, dtb, alog, dvec, h0)


LRU_NB = 2
N_SEG = 8
SCAN_UNROLL = 8


def _seg_rows(t):
    seg = t // N_SEG
    return seg, seg + 8


def _seg_scan(a_ref, b_ref, acc_ref, h_ref, lead, seg, pitch, reverse):
    def step(i, carry):
        out = []
        for (h, acc), ld, rev in zip(carry, lead, reverse):
            j = (seg - 1 - i) if rev else i
            idx = ld + (pl.ds(j, N_SEG, stride=pitch), slice(None))
            a = a_ref[idx]
            h = a * h + b_ref[idx]
            acc = acc * a
            acc_ref[idx] = acc
            h_ref[idx] = h
            out.append((h, acc))
        return tuple(out)

    init = (jnp.zeros((N_SEG, LANES), F32), jnp.ones((N_SEG, LANES), F32))
    return lax.fori_loop(0, seg, step, tuple(init for _ in lead), unroll=SCAN_UNROLL)


def _seg_inputs(h0, end, tot, reverse):
    rows = [None] * N_SEG
    hin = h0
    for s in (range(N_SEG - 1, -1, -1) if reverse else range(N_SEG)):
        rows[s] = hin
        hin = tot[s:s + 1] * hin + end[s:s + 1]
    return rows, hin


def _lru_kernel(u_ref, gy_ref, cw_ref, cb_ref, wa_ref, wi_ref, ba_ref, bi_ref, lam_ref, h0_ref,
                r_ref, ht_ref, uc_ref, a_ref, b_ref, acc_ref, hl_ref, *, t):
    seg, pitch = _seg_rows(t)
    for k in range(LRU_NB):
        uc_ref[k] = _conv4(u_ref[0, k], cw_ref[k], cb_ref[k])
    for d in (1, 0):
        for k in range(LRU_NB):
            c = (-0.5 * LRU_C * math.log2(math.e)) * _softplus(-lam_ref[d, k])
            for s in range(N_SEG):
                u = uc_ref[k, s * seg:(s + 1) * seg, :]
                u16 = u.astype(BF16)
                ta = jnp.tanh(jnp.dot(u16, wa_ref[d, k], preferred_element_type=F32) + ba_ref[d, k])
                ti = jnp.tanh(jnp.dot(u16, wi_ref[d, k], preferred_element_type=F32) + bi_ref[d, k])
                a = jnp.exp2(c * ta + c)
                a_ref[d, k, s * pitch:s * pitch + seg, :] = a
                half_u = 0.5 * u
                gated_u = half_u * ti + half_u
                y = (1.0 - a) * (1.0 + a)
                root = jnp.where(y > 0.0, y * lax.rsqrt(y), 0.0)
                b_ref[d, k, s * pitch:s * pitch + seg, :] = root * gated_u
    chains = [(d, k) for d in (1, 0) for k in range(LRU_NB)]
    scanned = _seg_scan(a_ref, b_ref, acc_ref, hl_ref, chains, seg, pitch, [d == 1 for d, _ in chains])
    for (d, k), (end, tot) in zip(chains, scanned):
        rows, hfin = _seg_inputs(h0_ref[0, d, k], end, tot, reverse=(d == 1))
        ht_ref[0, d, k] = hfin
        for s in range(N_SEG):
            src = slice(s * pitch, s * pitch + seg)
            dst = slice(s * seg, (s + 1) * seg)
            h = hl_ref[d, k, src, :] + acc_ref[d, k, src, :] * rows[s]
            if d == 1:
                r_ref[0, k, dst, :] = h
            else:
                r_ref[0, k, dst, :] = (r_ref[0, k, dst, :] + h) * _gelu_tanh(gy_ref[0, k, dst, :])


def _lru(p, h0, conv_w, conv_b, wa, wi, ba, bi, lam, *, gy_blk0, u_blk0):
    bsz, _, t, _ = p.shape
    nb = LRU_NB
    seg, pitch = _seg_rows(t)
    blk = lambda off: pl.BlockSpec((1, nb, t, LANES), lambda b, g: (b, off // nb + g, 0, 0))
    par = lambda shape: pl.BlockSpec(shape, lambda b, g: (0, g) + (0,) * (len(shape) - 2))
    st_spec = pl.BlockSpec((1, 2, nb, 1, LANES), lambda b, g: (b, 0, g, 0, 0))
    return pl.pallas_call(
        functools.partial(_lru_kernel, t=t),
        grid=(bsz, LRU_BLOCKS // nb),
        in_specs=[blk(u_blk0), blk(gy_blk0),
                  pl.BlockSpec((nb, 4, LANES), lambda b, g: (g, 0, 0)),
                  pl.BlockSpec((nb, 1, LANES), lambda b, g: (g, 0, 0)),
                  par((2, nb, LANES, LANES)), par((2, nb, LANES, LANES)),
                  par((2, nb, 1, LANES)), par((2, nb, 1, LANES)), par((2, nb, 1, LANES)),
                  st_spec],
        out_specs=[pl.BlockSpec((1, nb, t, LANES), lambda b, g: (b, g, 0, 0)), st_spec],
        out_shape=[jax.ShapeDtypeStruct((bsz, LRU_BLOCKS, t, LANES), F32),
                   jax.ShapeDtypeStruct(h0.shape, F32)],
        scratch_shapes=[pltpu.VMEM((nb, t, LANES), F32)] + [pltpu.VMEM((2, nb, N_SEG * pitch, LANES), F32)] * 4,
        compiler_params=_cparams(("parallel", "parallel")),
        name="rglru",
    )(p, p, conv_w, conv_b, wa, wi, ba, bi, lam, h0)


LRU_W = LRU_BLOCKS * LANES
EV_Z0 = 0
EV_XBC0 = EV_Z0 + SSD_W_BLKS
EV_DT0 = EV_XBC0 + SSD_XBC_BLKS
EV_GY0 = EV_DT0 + SSD_GROUPS
EV_U0 = EV_GY0 + LRU_BLOCKS


def _blocks(v, n):
    return v.reshape(n, 1, LANES)


def _per_group_heads(v):
    v = v.reshape(2, SSD_GROUPS, SSD_HPG).transpose(1, 0, 2).reshape(SSD_GROUPS, 2 * SSD_HPG)
    return jnp.pad(v, ((0, 0), (0, LANES - 2 * SSD_HPG))).reshape(SSD_GROUPS, 1, LANES)


def _even_weights(w_in, conv_w, conv_b, dt_bias, a_log, ssd_d, lru_conv_w, lru_conv_b, w_a, b_a, w_i, b_i, lam):
    d = w_in.shape[0]
    splits = np.cumsum([SSD_W, SSD_XBC, 2 * SSD_HEADS, LRU_W])
    z, xbc, dt, gy, u = jnp.split(w_in, [int(v) for v in splits], axis=1)
    dt = dt.reshape(d, 2, SSD_GROUPS, SSD_HPG).transpose(0, 2, 1, 3).reshape(d, SSD_GROUPS, 2 * SSD_HPG)
    dt = jnp.pad(dt, ((0, 0), (0, 0), (0, LANES - 2 * SSD_HPG))).reshape(d, SSD_GROUPS * LANES)
    return dict(
        w_in=jnp.concatenate([z, xbc, dt, gy, u], axis=1).astype(BF16),
        conv_w=conv_w.reshape(4, SSD_XBC_BLKS, LANES).transpose(1, 0, 2), conv_b=_blocks(conv_b, SSD_XBC_BLKS),
        dtb=_per_group_heads(dt_bias), alog=_per_group_heads(a_log),
        dvec=_blocks(jnp.repeat(ssd_d, SSD_HEAD_DIM), SSD_W_BLKS),
        lru_conv_w=lru_conv_w.reshape(4, LRU_BLOCKS, LANES).transpose(1, 0, 2),
        lru_conv_b=_blocks(lru_conv_b, LRU_BLOCKS),
        w_a=(0.5 * w_a).astype(BF16), w_i=(0.5 * w_i).astype(BF16),
        b_a=0.5 * b_a.reshape(2, LRU_BLOCKS, 1, LANES), b_i=0.5 * b_i.reshape(2, LRU_BLOCKS, 1, LANES),
        lam=lam.reshape(2, LRU_BLOCKS, 1, LANES))


def _even_scans(p, states, w, need_out=True):
    del need_out
    ssd_h, lru_h = states
    y, ssd_h = _ssd(p, ssd_h, w["conv_w"], w["conv_b"], w["dtb"], w["alog"], w["dvec"],
                    z_blk0=EV_Z0, xbc_blk0=EV_XBC0, dt_blk0=EV_DT0)
    r, lru_h = _lru(p, lru_h, w["lru_conv_w"], w["lru_conv_b"], w["w_a"], w["w_i"], w["b_a"], w["b_i"],
                    w["lam"], gy_blk0=EV_GY0, u_blk0=EV_U0)
    return y, r, (ssd_h, lru_h)


def _even_zero_states(bsz):
    return (jnp.zeros((bsz, 2, SSD_GROUPS, SSD_GBLK, SSD_STATE, LANES), F32),
            jnp.zeros((bsz, 2, LRU_BLOCKS, 1, LANES), F32))


HG_CHUNK = 128
HG_PAR = 3
SUBLANES = 8


def _group_boundary(cum, c, reverse):
    m = c // 2
    off = m if reverse else m - 1
    n = cum.shape[0]
    if c >= 2 * SUBLANES:
        r = cum.reshape(n // c, c, LANES)
        return jnp.broadcast_to(r[:, off:off + 1, :], r.shape).reshape(n, LANES)
    r = cum.reshape(n // SUBLANES, SUBLANES, LANES)
    sub = lax.broadcasted_iota(jnp.int32, r.shape, 1)
    p = None
    for g0 in range(0, SUBLANES, c):
        cand = jnp.broadcast_to(r[:, g0 + off:g0 + off + 1, :], r.shape)
        p = cand if p is None else jnp.where(sub >= g0, cand, p)
    return p.reshape(n, LANES)


def _hgrn_masks():
    l = np.arange(HG_CHUNK)[:, None]
    s = np.arange(HG_CHUNK)[None, :]
    fwd = []
    size = HG_CHUNK
    while size >= 2:
        half = size // 2
        fwd.append((l // size == s // size) & (l % size >= half) & (s % size < half))
        size = half
    fwd = np.stack(fwd).astype(np.float32)
    pairs = np.stack([fwd, fwd.transpose(0, 2, 1)])
    tri = np.stack([l >= s, l <= s]).astype(np.float32)
    sign = np.stack([np.where(pairs[d].any(axis=2, keepdims=True), 1.0, -1.0) for d in range(2)])
    sign = np.broadcast_to(sign, pairs.shape).astype(np.float32)
    return jnp.asarray(pairs), jnp.asarray(tri, dtype=BF16), jnp.asarray(sign)


def _hgrn_chunks(jobs, q_ref, ff_ref, fb_ref, v_ref, lb_ref, sf_ref, sb_ref, pairs_ref, tri_ref, sign_ref,
                 need_out):
    n = len(jobs)
    qq, kk, vv, cum = [], [], [], []
    for hd, reverse, rows in jobs:
        lb = lb_ref[hd]
        fx = (fb_ref if reverse else ff_ref)[0, hd, rows, :]
        e = jnp.exp(-jnp.abs(fx))
        big = 1.0 / (1.0 + e)
        small = e * big
        pos = fx >= 0.0
        log2_f = jnp.log2(lb + (1.0 - lb) * jnp.where(pos, big, small))
        kk.append((1.0 - lb) * jnp.where(pos, small, big))
        cum.append(_dot_select(tri_ref[1 if reverse else 0], log2_f))
    for hd, reverse, rows in jobs:
        qq.append(_silu(q_ref[0, hd, rows, :]) if need_out else None)
        vv.append(v_ref[0, hd, rows, :])
    att = [None] * n
    q16 = [v.astype(BF16) for v in qq] if need_out else None
    k16 = [v.astype(BF16) for v in kk]
    size = HG_CHUNK
    level = 0
    while need_out and size >= 2:
        for j, (hd, reverse, rows) in enumerate(jobs):
            dist = (cum[j] - _group_boundary(cum[j], size, reverse)) * sign_ref[1 if reverse else 0, level]
            fac = jnp.exp2(dist).astype(BF16)
            a_l = lax.dot_general(q16[j] * fac, k16[j] * fac, (((1,), (1,)), ((), ())),
                                  preferred_element_type=F32)
            a_l = a_l * pairs_ref[1 if reverse else 0, level]
            att[j] = a_l if att[j] is None else att[j] + a_l
        size //= 2
        level += 1
    outs = []
    for j, (hd, reverse, rows) in enumerate(jobs):
        s_ref = (sb_ref if reverse else sf_ref).at[hd]
        sv = s_ref[...]
        o = None
        if need_out:
            diag = jnp.sum(qq[j] * kk[j], axis=-1, keepdims=True)
            o = jnp.dot(att[j].astype(BF16), vv[j].astype(BF16), preferred_element_type=F32) + diag * vv[j]
            o = o + jnp.dot((qq[j] * jnp.exp2(cum[j])).astype(BF16), sv.astype(BF16),
                            preferred_element_type=F32)
        end_row = 0 if reverse else HG_CHUNK - 1
        cum_end = cum[j][end_row:end_row + 1, :]
        kw = (kk[j] * jnp.exp2(cum_end - cum[j])).T.astype(BF16)
        keep = jnp.broadcast_to(jnp.exp2(cum_end), (HG_CHUNK, LANES)).T
        s_ref[...] = sv * keep + jnp.dot(kw, vv[j].astype(BF16), preferred_element_type=F32)
        outs.append(o)
    return outs


def _hgrn_kernel(q_ref, ff_ref, fb_ref, v_ref, g_ref, lb_ref, ng_ref, pairs_ref, tri_ref, sign_ref, h0_ref,
                 *rest, t, need_out):
    if need_out:
        o_ref, ht_ref, sf_ref, sb_ref = rest
    else:
        ht_ref, sf_ref, sb_ref = rest
    nchunk = t // HG_CHUNK
    for hd in range(HG_PAR):
        sf_ref[hd] = h0_ref[0, 0, hd]
        sb_ref[hd] = h0_ref[0, 1, hd]

    def pair(i, carry, second):
        jobs = []
        for hd in range(HG_PAR):
            for reverse in (False, True):
                c = (nchunk - 1 - i) if reverse else i
                jobs.append((hd, reverse, pl.ds(pl.multiple_of(c * HG_CHUNK, HG_CHUNK), HG_CHUNK)))
        outs = _hgrn_chunks(jobs, q_ref, ff_ref, fb_ref, v_ref, lb_ref, sf_ref, sb_ref, pairs_ref, tri_ref,
                            sign_ref, need_out)
        if need_out:
            for (hd, reverse, rows), o in zip(jobs, outs):
                o_ref[0, hd, rows, :] = (o_ref[0, hd, rows, :] + o) if second else o
        return carry

    def finish(c, carry):
        rows = pl.ds(pl.multiple_of(c * HG_CHUNK, HG_CHUNK), HG_CHUNK)
        for hd in range(HG_PAR):
            o_ref[0, hd, rows, :] = _rms(o_ref[0, hd, rows, :], ng_ref[hd]) * _silu(g_ref[0, hd, rows, :])
        return carry

    lax.fori_loop(0, nchunk // 2, functools.partial(pair, second=False), 0)
    lax.fori_loop(nchunk // 2, nchunk, functools.partial(pair, second=True), 0)
    if need_out:
        lax.fori_loop(0, nchunk, finish, 0)
    for hd in range(HG_PAR):
        ht_ref[0, 0, hd] = sf_ref[hd]
        ht_ref[0, 1, hd] = sb_ref[hd]


def _hgrn(p, h0, lb, norm_g, need_out=True):
    bsz, _, t, _ = p.shape
    assert (t // HG_CHUNK) % 2 == 0
    hp = HG_PAR
    blk = lambda off: pl.BlockSpec((1, hp, t, LANES), lambda b, h: (b, off // hp + h, 0, 0))
    par = pl.BlockSpec((hp, 1, LANES), lambda b, h: (h, 0, 0))
    st_spec = pl.BlockSpec((1, 2, hp, LANES, LANES), lambda b, h: (b, 0, h, 0, 0))
    pairs, tri, sign = _hgrn_masks()
    o_spec = [pl.BlockSpec((1, hp, t, LANES), lambda b, h: (b, h, 0, 0))] if need_out else []
    o_shape = [jax.ShapeDtypeStruct((bsz, HG_HEADS, t, LANES), F32)] if need_out else []
    outs = pl.pallas_call(
        functools.partial(_hgrn_kernel, t=t, need_out=need_out),
        grid=(bsz, HG_HEADS // hp),
        in_specs=[blk(0), blk(HG_HEADS), blk(2 * HG_HEADS), blk(3 * HG_HEADS), blk(4 * HG_HEADS),
                  par, par,
                  pl.BlockSpec(pairs.shape, lambda b, h: (0, 0, 0, 0)),
                  pl.BlockSpec(tri.shape, lambda b, h: (0, 0, 0)),
                  pl.BlockSpec(sign.shape, lambda b, h: (0, 0, 0, 0)),
                  st_spec],
        out_specs=o_spec + [st_spec],
        out_shape=o_shape + [jax.ShapeDtypeStruct(h0.shape, F32)],
        scratch_shapes=[pltpu.VMEM((hp, LANES, LANES), F32), pltpu.VMEM((hp, LANES, LANES), F32)],
        compiler_params=_cparams(("parallel", "parallel")),
        name="hgrn2",
    )(p, p, p, p, p, lb.reshape(HG_HEADS, 1, LANES), norm_g.reshape(HG_HEADS, 1, LANES), pairs, tri, sign,
      h0)
    return tuple(outs) if need_out else (None, outs[0])


S5_W = S5_GROUPS * S5_GROUP_CH
S5_NSTATE = S5_GROUPS * S5_STATE
S5_NB = S5_NSTATE // LANES
S5_PAR = 2


def _s5_kernel(u_ref, are_ref, aim_ref, pw_ref, bd_ref, cd_ref, dvec_ref, gw_ref, gb_ref, h0_ref,
               y_ref, ht_ref, wr_ref, wi_ref, hr_ref, hi_ref, acc_ref, *, t):
    seg, pitch = _seg_rows(t)
    u = jnp.concatenate([u_ref[0, 0], u_ref[0, 1]], axis=-1)
    u16 = u.astype(BF16)
    acc_ref[...] = dvec_ref[...] * u
    chains = [(d, k) for d in (0, 1) for k in range(S5_PAR)]
    for j0 in range(0, S5_NB, S5_PAR):
        cols = slice(j0 * LANES, (j0 + S5_PAR) * LANES)
        group = max(1, LANES // seg)
        for s0 in range(0, N_SEG, group):
            us = u16[s0 * seg:(s0 + group) * seg]
            wr = jnp.dot(us, bd_ref[0, :, cols], preferred_element_type=F32)
            wi = jnp.dot(us, bd_ref[1, :, cols], preferred_element_type=F32)
            for s in range(s0, s0 + group):
                part = slice((s - s0) * seg, (s - s0 + 1) * seg)
                for k in range(S5_PAR):
                    wr_ref[k, s * pitch:s * pitch + seg, :] = wr[part, k * LANES:(k + 1) * LANES]
                    wi_ref[k, s * pitch:s * pitch + seg, :] = wi[part, k * LANES:(k + 1) * LANES]
        ar = [jnp.broadcast_to(are_ref[d, j0 + k], (N_SEG, LANES)) for d, k in chains]
        ai = [jnp.broadcast_to(aim_ref[d, j0 + k], (N_SEG, LANES)) for d, k in chains]

        def step(i, carry, ar=ar, ai=ai):
            out = []
            for c, (d, k) in enumerate(chains):
                gr, gi = carry[c]
                j = (seg - 1 - i) if d == 1 else i
                src = (k, pl.ds(j, N_SEG, stride=pitch), slice(None))
                dst = (d, k, pl.ds(j, N_SEG, stride=pitch), slice(None))
                ngr = ar[c] * gr - ai[c] * gi + wr_ref[src]
                ngi = ar[c] * gi + ai[c] * gr + wi_ref[src]
                hr_ref[dst] = ngr
                hi_ref[dst] = ngi
                out.append((ngr, ngi))
            return tuple(out)

        zero = jnp.zeros((N_SEG, LANES), F32)
        fin = lax.fori_loop(0, seg, step, tuple((zero, zero) for _ in chains), unroll=SCAN_UNROLL)
        for d in (0, 1):
            ins = []
            last = 0 if d == 1 else seg - 1
            for k in range(S5_PAR):
                end_r, end_i = fin[d * S5_PAR + k]
                tot_r = pw_ref[d, 0, j0 + k, last:last + 1, :]
                tot_i = pw_ref[d, 1, j0 + k, last:last + 1, :]
                hin_r, hin_i = h0_ref[0, d, 0, j0 + k], h0_ref[0, d, 1, j0 + k]
                rows = [None] * N_SEG
                for s in (range(N_SEG - 1, -1, -1) if d == 1 else range(N_SEG)):
                    rows[s] = (hin_r, hin_i)
                    hin_r, hin_i = (tot_r[0:1] * hin_r - tot_i[0:1] * hin_i + end_r[s:s + 1],
                                    tot_r[0:1] * hin_i + tot_i[0:1] * hin_r + end_i[s:s + 1])
                ht_ref[0, d, 0, j0 + k] = hin_r
                ht_ref[0, d, 1, j0 + k] = hin_i
                ins.append(rows)
            gr_rows, gi_rows = [], []
            for s in range(N_SEG):
                src = slice(s * pitch, s * pitch + seg)
                gr_parts, gi_parts = [], []
                for k in range(S5_PAR):
                    in_r, in_i = ins[k][s]
                    pr, pi = pw_ref[d, 0, j0 + k], pw_ref[d, 1, j0 + k]
                    gr_parts.append((hr_ref[d, k, src, :] + pr * in_r - pi * in_i).astype(BF16))
                    gi_parts.append((hi_ref[d, k, src, :] + pr * in_i + pi * in_r).astype(BF16))
                gr_rows.append(jnp.concatenate(gr_parts, axis=-1))
                gi_rows.append(jnp.concatenate(gi_parts, axis=-1))
                if len(gr_rows) == group:
                    rows = slice((s + 1 - group) * seg, (s + 1) * seg)
                    acc_ref[rows, :] += (
                        jnp.dot(jnp.concatenate(gr_rows, axis=0), cd_ref[d, 0, cols, :],
                                preferred_element_type=F32)
                        + jnp.dot(jnp.concatenate(gi_rows, axis=0), cd_ref[d, 1, cols, :],
                                  preferred_element_type=F32))
                    gr_rows, gi_rows = [], []
    y = _gelu_tanh(acc_ref[...])
    y = y * _sigmoid(jnp.dot(y.astype(BF16), gw_ref[...], preferred_element_type=F32) + gb_ref[...])
    y_ref[0, 0] = y[:, :LANES]
    y_ref[0, 1] = y[:, LANES:]


def _s5(p, h0, log_a, bd, cd, dvec, glu_w, glu_b, *, u_blk0):
    bsz, _, t, _ = p.shape
    seg, pitch = _seg_rows(t)
    la_re, la_im = log_a
    pos = jnp.arange(seg, dtype=F32)
    n = jnp.stack([pos + 1.0, seg - pos]).reshape(2, 1, seg, 1)
    mag = jnp.exp(n * la_re)
    pw = jnp.stack([mag * jnp.cos(n * la_im), mag * jnp.sin(n * la_im)], axis=1)
    a_re, a_im = pw[0, 0, :, 0:1, :], pw[0, 1, :, 0:1, :]
    a_re = jnp.stack([a_re, pw[1, 0, :, seg - 1:seg, :]])
    a_im = jnp.stack([a_im, pw[1, 1, :, seg - 1:seg, :]])
    full = lambda a: pl.BlockSpec(a.shape, lambda b: (0,) * a.ndim)
    st_spec = pl.BlockSpec((1, 2, 2, S5_NB, 1, LANES), lambda b: (b, 0, 0, 0, 0, 0))
    return pl.pallas_call(
        functools.partial(_s5_kernel, t=t),
        grid=(bsz,),
        in_specs=[pl.BlockSpec((1, 2, t, LANES), lambda b: (b, u_blk0 // 2, 0, 0)),
                  full(a_re), full(a_im), full(pw), full(bd), full(cd), full(dvec), full(glu_w), full(glu_b),
                  st_spec],
        out_specs=[pl.BlockSpec((1, 2, t, LANES), lambda b: (b, 0, 0, 0)), st_spec],
        out_shape=[jax.ShapeDtypeStruct((bsz, 2, t, LANES), F32), jax.ShapeDtypeStruct(h0.shape, F32)],
        scratch_shapes=[pltpu.VMEM((S5_PAR, N_SEG * pitch, LANES), F32),
                        pltpu.VMEM((S5_PAR, N_SEG * pitch, LANES), F32),
                        pltpu.VMEM((2, S5_PAR, N_SEG * pitch, LANES), F32),
                        pltpu.VMEM((2, S5_PAR, N_SEG * pitch, LANES), F32),
                        pltpu.VMEM((t, S5_W), F32)],
        compiler_params=_cparams(("parallel",)),
        name="s5",
    )(p, a_re, a_im, pw, bd, cd, dvec, glu_w, glu_b, h0)


def _s5_params(lam_re, lam_im, log_step, b_re, b_im, c_re, c_im):
    step = jnp.exp(log_step)[..., None]
    mag = jnp.exp(lam_re * step)
    ar, ai = mag * jnp.cos(lam_im * step), mag * jnp.sin(lam_im * step)
    den = lam_re * lam_re + lam_im * lam_im
    zr = ((ar - 1) * lam_re + ai * lam_im) / den
    zi = (ai * lam_re - (ar - 1) * lam_im) / den
    czr = c_re * zr[:, :, None, :] - c_im * zi[:, :, None, :]
    czi = c_re * zi[:, :, None, :] + c_im * zr[:, :, None, :]
    eye = jnp.eye(S5_GROUPS, dtype=F32)

    def in_mat(m):
        return jnp.einsum("gpk,gh->gkhp", m, eye).reshape(S5_W, S5_NSTATE)

    def out_mat(m):
        return jnp.einsum("dgkp,gh->dgphk", m, eye).reshape(2, S5_NSTATE, S5_W)

    bd = jnp.stack([in_mat(b_re), in_mat(b_im)], axis=0).astype(BF16)
    cd = jnp.stack([out_mat(czr), -out_mat(czi)], axis=1).astype(BF16)
    shape = (2, S5_NB, 1, LANES)
    log_a = ((lam_re * step).reshape(shape), (lam_im * step).reshape(shape))
    return log_a, bd, cd


OD_U0 = 5 * HG_HEADS


def _odd_weights(w_in, lower_bound, hg_norm_g, lam_re, lam_im, log_step, b_re, b_im, c_re, c_im, s5_d,
                 glu_w, glu_b):
    log_a, bd, cd = _s5_params(lam_re, lam_im, log_step, b_re, b_im, c_re, c_im)
    return dict(w_in=w_in.astype(BF16), lb=lower_bound, norm_g=hg_norm_g, log_a=log_a, bd=bd, cd=cd,
                dvec=s5_d.reshape(1, S5_W), glu_w=glu_w.astype(BF16), glu_b=glu_b.reshape(1, S5_W))


def _odd_scans(p, states, w, need_out=True):
    hg_h, s5_h = states
    o, hg_h = _hgrn(p, hg_h, w["lb"], w["norm_g"], need_out)
    y, s5_h = _s5(p, s5_h, w["log_a"], w["bd"], w["cd"], w["dvec"], w["glu_w"], w["glu_b"], u_blk0=OD_U0)
    return o, y, (hg_h, s5_h)


def _odd_zero_states(bsz):
    return (jnp.zeros((bsz, 2, HG_HEADS, LANES, LANES), F32),
            jnp.zeros((bsz, 2, 2, S5_NB, 1, LANES), F32))


PROJ_TILE = 256
MIX_TILE = 512
FFN_TILE = 1024
FFN_COLS = 256


def kernel(x, c, ctx, c_ctx, w_mod, b_mod, norm_mix_g, norm_ffn_g, final_norm_g,
           ev_w_in, ev_w_out, ssd_conv_w, ssd_conv_b, ssd_dt_bias, ssd_a_log, ssd_d, ssd_norm_g,
           lru_conv_w, lru_conv_b, lru_w_a, lru_b_a, lru_w_i, lru_b_i, lru_lam,
           od_w_in, od_w_out, hg_lb_logits, hg_norm_g,
           s5_lam_re, s5_lam_im, s5_log_step, s5_b_re, s5_b_im, s5_c_re, s5_c_im, s5_d,
           s5_glu_w, s5_glu_b,
           ffn_w_gate, ffn_w_up, ffn_conv_w, ffn_conv_b, ffn_w_down):
    bsz, _, d = x.shape
    depth = w_mod.shape[0]
    prob = jax.nn.softmax(hg_lb_logits.astype(F32), axis=0)
    lower_bounds = (jnp.cumsum(prob, axis=0) - prob[0]).astype(hg_lb_logits.dtype)

    pad = (-(bsz + 1)) % SUBLANES
    cond = jnp.concatenate([c, c_ctx[None], jnp.zeros((pad, d), c.dtype)], axis=0)
    mods = _modulation(cond, w_mod, b_mod).transpose(0, 2, 1, 3)

    for layer in range(depth):
        last = layer == depth - 1
        j = layer // 2
        mod_x = mods[layer, :bsz]
        mod_c = mods[layer, bsz:bsz + 1]
        if layer % 2 == 0:
            w = _even_weights(ev_w_in[j], ssd_conv_w[j], ssd_conv_b[j], ssd_dt_bias[j], ssd_a_log[j], ssd_d[j],
                              lru_conv_w[j], lru_conv_b[j], lru_w_a[j], lru_b_a[j], lru_w_i[j], lru_b_i[j],
                              lru_lam[j])
            scans, zero_states = _even_scans, _even_zero_states
            w_out, norm_gain, norm_first = ev_w_out[j].astype(BF16), ssd_norm_g[j], True
        else:
            w = _odd_weights(od_w_in[j], lower_bounds[layer], hg_norm_g[j], s5_lam_re[j], s5_lam_im[j],
                             s5_log_step[j], s5_b_re[j], s5_b_im[j], s5_c_re[j], s5_c_im[j], s5_d[j],
                             s5_glu_w[j], s5_glu_b[j])
            scans, zero_states = _odd_scans, _odd_zero_states
            w_out, norm_gain, norm_first = od_w_out[j].astype(BF16), hg_norm_g[j], False
        ffn_w = _ffn_weights(ffn_w_gate[layer], ffn_w_up[layer], ffn_conv_w[layer], ffn_conv_b[layer],
                             ffn_w_down[layer], FFN_COLS)

        p_c = _project(ctx, mod_c, norm_mix_g[layer], w["w_in"], PROJ_TILE)
        a_c, b_c, states = scans(p_c, zero_states(bsz), w, need_out=not last)
        p_x = _project(x, mod_x, norm_mix_g[layer], w["w_in"], PROJ_TILE)
        a_x, b_x, _ = scans(p_x, states, w)
        x = _mix_out(x, a_x, b_x, mod_x, norm_gain, w_out, norm_first, MIX_TILE)
        x = _conv_ffn(x, mod_x, norm_ffn_g[layer], final_norm_g, *ffn_w, grid_conv=True, final_norm=last,
                      tile=FFN_TILE, tf=FFN_COLS)
        if not last:
            ctx = _mix_out(ctx, a_c, b_c, mod_c, norm_gain, w_out, norm_first, MIX_TILE)
            ctx = _conv_ffn(ctx, mod_c, norm_ffn_g[layer], final_norm_g, *ffn_w, grid_conv=False,
                            final_norm=False, tile=FFN_TILE, tf=FFN_COLS)
    return x
```

```python
import functools
import math

import jax
import jax.numpy as jnp
import numpy as np
from jax import lax
from jax.experimental import pallas as pl
from jax.experimental.pallas import tpu as pltpu

LANES = 128
RMS_EPS = 1e-6
N_MOD = 6
GRID_W = 64
SSD_HEAD_DIM = 64
SSD_HEADS = 16
SSD_GROUPS = 2
SSD_HPG = SSD_HEADS // SSD_GROUPS
SSD_STATE = 128
SSD_CHUNK = 128
SSD_W = SSD_HEADS * SSD_HEAD_DIM
SSD_XBC = SSD_W + 2 * SSD_GROUPS * SSD_STATE
SSD_W_BLKS, SSD_XBC_BLKS = SSD_W // LANES, SSD_XBC // LANES
SSD_GBLK = SSD_W_BLKS // SSD_GROUPS
SSD_B0, SSD_C0 = SSD_W_BLKS, SSD_W_BLKS + SSD_GROUPS
LRU_BLOCKS = 8
LRU_C = 8.0
HG_HEADS = 6
S5_GROUPS = 16
S5_GROUP_CH = 16
S5_STATE = 64
VMEM_LIMIT = 56 * 1024 * 1024

BF16 = jnp.bfloat16
F32 = jnp.float32


def _cparams(sem):
    return pltpu.CompilerParams(dimension_semantics=sem, vmem_limit_bytes=VMEM_LIMIT)


def _dot(a, b):
    return jnp.dot(a.astype(BF16), b.astype(BF16), preferred_element_type=F32)


def _dot_select(sel, v):
    hi = v.astype(BF16)
    rest = v - hi.astype(F32)
    mid = rest.astype(BF16)
    lo = (rest - mid.astype(F32)).astype(BF16)
    return (jnp.dot(sel, hi, preferred_element_type=F32) + jnp.dot(sel, mid, preferred_element_type=F32)
            + jnp.dot(sel, lo, preferred_element_type=F32))


def _sigmoid(v):
    return 0.5 * jnp.tanh(0.5 * v) + 0.5


def _silu(v):
    h = 0.5 * v
    return h * jnp.tanh(h) + h


def _gelu_tanh(v):
    return 0.5 * v * (1.0 + jnp.tanh(math.sqrt(2.0 / math.pi) * (v + 0.044715 * (v * v * v))))


def _softplus(v):
    return jnp.maximum(v, 0.0) + jnp.log(1.0 + jnp.exp(-jnp.abs(v)))


def _rms(v, g):
    return v * lax.rsqrt(jnp.mean(v * v, axis=-1, keepdims=True) + RMS_EPS) * g


def _norm_mod(xv, g, shift, scale):
    return _rms(xv, g) * (1.0 + scale) + shift


def _mod_kernel(s_ref, w_ref, b_ref, o_ref):
    o_ref[0, 0] = _dot(_silu(s_ref[...]), w_ref[0]) + b_ref[0, 0]


def _modulation(s, w_mod, b_mod):
    depth, d, _ = w_mod.shape
    rows = s.shape[0]
    return pl.pallas_call(
        _mod_kernel,
        grid=(depth, N_MOD),
        in_specs=[pl.BlockSpec((rows, d), lambda l, j: (0, 0)),
                  pl.BlockSpec((1, d, d), lambda l, j: (l, 0, j)),
                  pl.BlockSpec((1, 1, 1, d), lambda l, j: (l, j, 0, 0))],
        out_specs=pl.BlockSpec((1, 1, rows, d), lambda l, j: (l, j, 0, 0)),
        out_shape=jax.ShapeDtypeStruct((depth, N_MOD, rows, d), F32),
        compiler_params=_cparams(("arbitrary", "arbitrary")),
        name="modulation",
    )(s, w_mod, b_mod.reshape(depth, N_MOD, 1, d))


def _proj_kernel(x_ref, mod_ref, g_ref, w_ref, o_ref, *, nblk):
    m = mod_ref[0]
    h = _norm_mod(x_ref[0], g_ref[...], m[0:1], m[1:2]).astype(BF16)
    group = 4
    for b0 in range(0, nblk, group):
        nb = min(group, nblk - b0)
        r = jnp.dot(h, w_ref[:, b0 * LANES:(b0 + nb) * LANES], preferred_element_type=F32)
        for k in range(nb):
            o_ref[0, b0 + k] = r[:, k * LANES:(k + 1) * LANES]


def _project(x, mod, gain, w, tile):
    bsz, t, d = x.shape
    nblk = w.shape[1] // LANES
    tile = min(tile, t)
    mod_map = (lambda b, i: (b, 0, 0)) if mod.shape[0] == bsz else (lambda b, i: (0, 0, 0))
    return pl.pallas_call(
        functools.partial(_proj_kernel, nblk=nblk),
        grid=(bsz, t // tile),
        in_specs=[pl.BlockSpec((1, tile, d), lambda b, i: (b, i, 0)),
                  pl.BlockSpec((1, N_MOD, d), mod_map),
                  pl.BlockSpec((1, d), lambda b, i: (0, 0)),
                  pl.BlockSpec((d, nblk * LANES), lambda b, i: (0, 0), pipeline_mode=pl.Buffered(1))],
        out_specs=pl.BlockSpec((1, nblk, tile, LANES), lambda b, i: (b, 0, i, 0)),
        out_shape=jax.ShapeDtypeStruct((bsz, nblk, t, LANES), F32),
        compiler_params=_cparams(("parallel", "parallel")),
        name="project",
    )(x, mod, gain.reshape(1, d), w)


def _mix_out_kernel(x_ref, ma_ref, mb_ref, mod_ref, ng_ref, w_ref, o_ref, *, norm_first):
    m = mod_ref[0]
    pa = [ma_ref[0, k] for k in range(ma_ref.shape[1])]
    pb = [mb_ref[0, k].astype(BF16) for k in range(mb_ref.shape[1])]
    if norm_first:
        pa = [_rms(jnp.concatenate(pa, axis=-1), ng_ref[...]).astype(BF16)]
    else:
        pa = [p.astype(BF16) for p in pa]
    v = jnp.concatenate(pa + pb, axis=-1)
    o_ref[0] = x_ref[0] + m[2:3] * jnp.dot(v, w_ref[...], preferred_element_type=F32)


def _mix_out(x, mix_a, mix_b, mod, norm_gain, w, norm_first, tile):
    bsz, t, d = x.shape
    tile = min(tile, t)
    mod_map = (lambda b, i: (b, 0, 0)) if mod.shape[0] == bsz else (lambda b, i: (0, 0, 0))
    ng = norm_gain.reshape(1, -1)
    mix_spec = lambda a: pl.BlockSpec((1, a.shape[1], tile, LANES), lambda b, i: (b, 0, i, 0))
    return pl.pallas_call(
        functools.partial(_mix_out_kernel, norm_first=norm_first),
        grid=(bsz, t // tile),
        in_specs=[pl.BlockSpec((1, tile, d), lambda b, i: (b, i, 0)),
                  mix_spec(mix_a), mix_spec(mix_b),
                  pl.BlockSpec((1, N_MOD, d), mod_map),
                  pl.BlockSpec(ng.shape, lambda b, i: (0, 0)),
                  pl.BlockSpec(w.shape, lambda b, i: (0, 0), pipeline_mode=pl.Buffered(1))],
        out_specs=pl.BlockSpec((1, tile, d), lambda b, i: (b, i, 0)),
        out_shape=jax.ShapeDtypeStruct((bsz, t, d), F32),
        compiler_params=_cparams(("parallel", "parallel")),
        name="mix_out",
    )(x, mix_a, mix_b, mod, ng, w)


def _ffn_kernel(x_ref, xp_ref, xn_ref, mod_ref, g_ref, fg_ref, wg_ref, wu_ref, cw_ref, cb_ref, wd_ref,
                o_ref, fx_ref, gt_ref, a0_ref, a1_ref, *, tile, halo, tf, grid_conv, period, final_norm):
    i = pl.program_id(1)
    nt = pl.num_programs(1)
    nf = wg_ref.shape[0]
    m = mod_ref[0]
    fx_ref[halo:halo + tile] = _norm_mod(x_ref[0], g_ref[...], m[3:4], m[4:5]).astype(BF16)
    if halo:
        keep_p = jnp.where(i > 0, 1.0, 0.0)
        keep_n = jnp.where(i < nt - 1, 1.0, 0.0)
        fx_ref[0:halo] = (keep_p * _norm_mod(xp_ref[0], g_ref[...], m[3:4], m[4:5])).astype(BF16)
        fx_ref[halo + tile:] = (keep_n * _norm_mod(xn_ref[0], g_ref[...], m[3:4], m[4:5])).astype(BF16)

    def gate(f, dst_ref):
        dst_ref[...] = jnp.dot(fx_ref[...], wg_ref[f], preferred_element_type=F32)

    def column(f, src_ref):
        a = src_ref[...]
        cw = cw_ref[f]
        rows = a.shape[0]
        pos = lax.broadcasted_iota(jnp.int32, a.shape, 0)
        if grid_conv:
            col = pos % GRID_W
            a_m1 = jnp.where(col == 0, 0.0, pltpu.roll(a, 1, axis=0))
            a_p1 = jnp.where(col == GRID_W - 1, 0.0, pltpu.roll(a, rows - 1, axis=0))
            conv = cb_ref[f]
            for dr in range(3):
                lo = dr * GRID_W
                conv = conv + (cw[3 * dr + 0:3 * dr + 1] * a_m1[lo:lo + tile]
                               + cw[3 * dr + 1:3 * dr + 2] * a[lo:lo + tile]
                               + cw[3 * dr + 2:3 * dr + 3] * a_p1[lo:lo + tile])
        else:
            col = pos % period
            a_m1 = jnp.where(col == 0, 0.0, pltpu.roll(a, 1, axis=0))
            a_p1 = jnp.where(col == period - 1, 0.0, pltpu.roll(a, rows - 1, axis=0))
            conv = cb_ref[f] + cw[3:4] * a_m1 + cw[4:5] * a + cw[5:6] * a_p1
        act = _silu(conv)
        up = jnp.dot(fx_ref[halo:halo + tile], wu_ref[f], preferred_element_type=F32)
        start = f * tf if isinstance(f, int) else pl.multiple_of(f * tf, tf)
        gt_ref[:, pl.ds(start, tf)] = (act * up).astype(BF16)

    def column_pair(p, carry):
        f = 2 * p
        gate(f + 1, a1_ref)
        column(f, a0_ref)
        gate(f + 2, a0_ref)
        column(f + 1, a1_ref)
        return carry

    gate(0, a0_ref)
    pairs = (nf - 1) // 2
    lax.fori_loop(0, pairs, column_pair, 0)
    if nf - 2 * pairs == 2:
        gate(nf - 1, a1_ref)
        column(nf - 2, a0_ref)
        column(nf - 1, a1_ref)
    else:
        column(nf - 1, a0_ref)
    y = x_ref[0] + m[5:6] * jnp.dot(gt_ref[...], wd_ref[...], preferred_element_type=F32)
    if final_norm:
        y = _rms(y, fg_ref[...])
    o_ref[0] = y


def _conv_ffn(x, mod, gain, final_gain, w_gate, w_up, conv_w, conv_b, w_down, grid_conv, final_norm,
              tile, tf):
    shape = x.shape
    period = x.shape[1]
    if not grid_conv and mod.shape[0] == 1 and tile % period == 0 and (x.shape[0] * period) % tile == 0:
        x = x.reshape(-1, tile, x.shape[2])
    bsz, t, d = x.shape
    nf = w_gate.shape[0]
    dff = nf * tf
    tile = min(tile, t)
    halo = GRID_W if grid_conv else 0
    nh = t // GRID_W
    per = tile // GRID_W
    mod_map = (lambda b, i: (b, 0, 0)) if mod.shape[0] == bsz else (lambda b, i: (0, 0, 0))
    resident = lambda a: pl.BlockSpec(a.shape, lambda b, i: (0,) * a.ndim, pipeline_mode=pl.Buffered(1))
    return pl.pallas_call(
        functools.partial(_ffn_kernel, tile=tile, halo=halo, tf=tf, grid_conv=grid_conv, period=period,
                          final_norm=final_norm),
        grid=(bsz, t // tile),
        in_specs=[pl.BlockSpec((1, tile, d), lambda b, i: (b, i, 0)),
                  pl.BlockSpec((1, GRID_W, d), lambda b, i: (b, jnp.maximum(i * per - 1, 0), 0)),
                  pl.BlockSpec((1, GRID_W, d), lambda b, i: (b, jnp.minimum((i + 1) * per, nh - 1), 0)),
                  pl.BlockSpec((1, N_MOD, d), mod_map),
                  pl.BlockSpec((1, d), lambda b, i: (0, 0)),
                  pl.BlockSpec((1, d), lambda b, i: (0, 0)),
                  resident(w_gate), resident(w_up), resident(conv_w), resident(conv_b), resident(w_down)],
        out_specs=pl.BlockSpec((1, tile, d), lambda b, i: (b, i, 0)),
        out_shape=jax.ShapeDtypeStruct((bsz, t, d), F32),
        scratch_shapes=[pltpu.VMEM((tile + 2 * halo, d), BF16), pltpu.VMEM((tile, dff), BF16),
                        pltpu.VMEM((tile + 2 * halo, tf), F32), pltpu.VMEM((tile + 2 * halo, tf), F32)],
        compiler_params=_cparams(("parallel", "parallel")),
        name="conv_ffn",
    )(x, x, x, mod, gain.reshape(1, d), final_gain.reshape(1, d), w_gate, w_up, conv_w, conv_b,
      w_down).reshape(shape)


def _ffn_weights(w_gate, w_up, conv_w, conv_b, w_down, tf):
    d, dff = w_gate.shape
    nf = dff // tf
    tiles = lambda w: w.reshape(w.shape[0], nf, tf).transpose(1, 0, 2)
    return (tiles(w_gate).astype(BF16), tiles(w_up).astype(BF16), tiles(conv_w.reshape(9, dff)),
            tiles(conv_b.reshape(1, dff)), w_down.astype(BF16))


def _conv4(v, w, b):
    t = v.shape[0]
    pos = lax.broadcasted_iota(jnp.int32, v.shape, 0)
    acc = b + w[1:2] * v
    acc = acc + w[0:1] * jnp.where(pos < 1, 0.0, pltpu.roll(v, 1, axis=0))
    acc = acc + w[2:3] * jnp.where(pos >= t - 1, 0.0, pltpu.roll(v, t - 1, axis=0))
    acc = acc + w[3:4] * jnp.where(pos >= t - 2, 0.0, pltpu.roll(v, t - 2, axis=0))
    return acc


def _ssd_kernel(xr_ref, br_ref, cr_ref, dt_ref, z_ref, cwx_ref, cwb_ref, cwc_ref, cbx_ref, cbb_ref, cbc_ref,
                dtb_ref, alog_ref, dvec_ref, h0_ref, y_ref, ht_ref, xs_ref, bs_ref, cs_ref, s_ref, *, t):
    nblk = SSD_GBLK
    nchunk = t // SSD_CHUNK
    for k in range(nblk):
        xs_ref[k] = _silu(_conv4(xr_ref[0, k], cwx_ref[k], cbx_ref[k]))
    bs_ref[...] = _silu(_conv4(br_ref[0, 0], cwb_ref[0], cbb_ref[0]))
    cs_ref[...] = _silu(_conv4(cr_ref[0, 0], cwc_ref[0], cbc_ref[0]))

    dtb = dtb_ref[0]
    a_neg = -jnp.exp(alog_ref[0])
    li = lax.broadcasted_iota(jnp.int32, (SSD_CHUNK, SSD_CHUNK), 0)
    si = lax.broadcasted_iota(jnp.int32, (SSD_CHUNK, SSD_CHUNK), 1)
    lane = lax.broadcasted_iota(jnp.int32, (1, LANES), 1)
    lo_half = lane < SSD_HEAD_DIM

    valid = [li >= si, li <= si]
    tri = [v.astype(F32).astype(BF16) for v in valid]
    for d in (0, 1):
        for k in range(nblk):
            s_ref[d, k] = h0_ref[0, d, 0, k]

    def pair(i, carry, finish):
        jobs = [(0, pl.ds(pl.multiple_of(i * SSD_CHUNK, SSD_CHUNK), SSD_CHUNK)),
                (1, pl.ds(pl.multiple_of((nchunk - 1 - i) * SSD_CHUNK, SSD_CHUNK), SSD_CHUNK))]
        pre = []
        for d, rows in jobs:
            bm = bs_ref[rows, :]
            cm = cs_ref[rows, :]
            dt = _softplus(dt_ref[0, 0, rows, :] + dtb)
            cum = _dot_select(tri[d], dt * a_neg)
            cb = lax.dot_general(cm.astype(BF16), bm.astype(BF16), (((1,), (1,)), ((), ())),
                                 preferred_element_type=F32)
            pre.append((dt, cum, cum.T, dt.T, cb, bm.T.astype(BF16), cm.astype(BF16)))
        for k in range(nblk):
            for (d, rows), (dt, cum, cum_t, dt_t, cb, bm_t, cm16) in zip(jobs, pre):
                end_row = SSD_CHUNK - 1 if d == 0 else 0
                xk = xs_ref[k, rows, :]
                ms = []
                for e in range(2):
                    idx = d * SSD_HPG + 2 * k + e
                    seg = jnp.where(valid[d], cum[:, idx:idx + 1] - cum_t[idx:idx + 1, :], -jnp.inf)
                    ms.append((cb * jnp.exp(seg) * dt_t[idx:idx + 1, :]).astype(BF16))
                x_lo = jnp.where(lo_half, xk, 0.0).astype(BF16)
                x_hi = jnp.where(lo_half, 0.0, xk).astype(BF16)
                yk = (jnp.dot(ms[0], x_lo, preferred_element_type=F32)
                      + jnp.dot(ms[1], x_hi, preferred_element_type=F32))
                i0 = d * SSD_HPG + 2 * k
                ecol = jnp.where(lo_half, cum[:, i0:i0 + 1], cum[:, i0 + 1:i0 + 2])
                dcol = jnp.where(lo_half, dt[:, i0:i0 + 1], dt[:, i0 + 1:i0 + 2])
                tot = jnp.where(lo_half, cum[end_row:end_row + 1, i0:i0 + 1],
                                cum[end_row:end_row + 1, i0 + 1:i0 + 2])
                sk = s_ref[d, k]
                yk = yk + jnp.dot(cm16, sk.astype(BF16), preferred_element_type=F32) * jnp.exp(ecol)
                xw = (xk * dcol * jnp.exp(tot - ecol)).astype(BF16)
                s_ref[d, k] = sk * jnp.exp(tot) + jnp.dot(bm_t, xw, preferred_element_type=F32)
                if finish:
                    ytot = y_ref[0, k, rows, :] + yk + dvec_ref[k] * xk
                    y_ref[0, k, rows, :] = ytot * _silu(z_ref[0, k, rows, :])
                else:
                    y_ref[0, k, rows, :] = yk
        return carry

    lax.fori_loop(0, nchunk // 2, functools.partial(pair, finish=False), 0)
    lax.fori_loop(nchunk // 2, nchunk, functools.partial(pair, finish=True), 0)
    for d in (0, 1):
        for k in range(nblk):
            ht_ref[0, d, 0, k] = s_ref[d, k]


def _ssd(p, h0, conv_w, conv_b, dtb, alog, dvec, *, z_blk0, xbc_blk0, dt_blk0):
    bsz, _, t, _ = p.shape
    assert (t // SSD_CHUNK) % 2 == 0
    nb = SSD_GBLK
    gx = lambda off: (lambda b, g: (b, off // nb + g, 0, 0))
    g1 = lambda off: (lambda b, g: (b, off + g, 0, 0))
    one = lambda off: pl.BlockSpec((1, 1, t, LANES), g1(off))
    st_spec = pl.BlockSpec((1, 2, 1, nb, SSD_STATE, LANES), lambda b, g: (b, 0, g, 0, 0, 0))
    return pl.pallas_call(
        functools.partial(_ssd_kernel, t=t),
        grid=(bsz, SSD_GROUPS),
        in_specs=[pl.BlockSpec((1, nb, t, LANES), gx(xbc_blk0)),
                  one(xbc_blk0 + SSD_B0), one(xbc_blk0 + SSD_C0), one(dt_blk0),
                  pl.BlockSpec((1, nb, t, LANES), gx(z_blk0)),
                  pl.BlockSpec((nb, 4, LANES), lambda b, g: (g, 0, 0)),
                  pl.BlockSpec((1, 4, LANES), lambda b, g: (SSD_B0 + g, 0, 0)),
                  pl.BlockSpec((1, 4, LANES), lambda b, g: (SSD_C0 + g, 0, 0)),
                  pl.BlockSpec((nb, 1, LANES), lambda b, g: (g, 0, 0)),
                  pl.BlockSpec((1, 1, LANES), lambda b, g: (SSD_B0 + g, 0, 0)),
                  pl.BlockSpec((1, 1, LANES), lambda b, g: (SSD_C0 + g, 0, 0)),
                  pl.BlockSpec((1, 1, LANES), lambda b, g: (g, 0, 0)),
                  pl.BlockSpec((1, 1, LANES), lambda b, g: (g, 0, 0)),
                  pl.BlockSpec((nb, 1, LANES), lambda b, g: (g, 0, 0)),
                  st_spec],
        out_specs=[pl.BlockSpec((1, nb, t, LANES), lambda b, g: (b, g, 0, 0)), st_spec],
        out_shape=[jax.ShapeDtypeStruct((bsz, SSD_W_BLKS, t, LANES), F32),
                   jax.ShapeDtypeStruct(h0.shape, F32)],
        scratch_shapes=[pltpu.VMEM((nb, t, LANES), F32), pltpu.VMEM((t, LANES), F32),
                        pltpu.VMEM((t, LANES), F32), pltpu.VMEM((2, nb, SSD_STATE, LANES), F32)],
        compiler_params=_cparams(("parallel", "parallel")),
        name="ssd",
    )(p, p, p, p, p, conv_w, conv_w, conv_w, conv_b, conv_b, conv_b, dtb, alog, dvec, h0)


LRU_NB = 2
N_SEG = 8
SCAN_UNROLL = 8


def _seg_rows(t):
    seg = t // N_SEG
    return seg, seg + 8


def _seg_scan(a_ref, b_ref, acc_ref, h_ref, lead, seg, pitch, reverse):
    def step(i, carry):
        out = []
        for (h, acc), ld, rev in zip(carry, lead, reverse):
            j = (seg - 1 - i) if rev else i
            idx = ld + (pl.ds(j, N_SEG, stride=pitch), slice(None))
            a = a_ref[idx]
            h = a * h + b_ref[idx]
            acc = acc * a
            acc_ref[idx] = acc
            h_ref[idx] = h
            out.append((h, acc))
        return tuple(out)

    init = (jnp.zeros((N_SEG, LANES), F32), jnp.ones((N_SEG, LANES), F32))
    return lax.fori_loop(0, seg, step, tuple(init for _ in lead), unroll=SCAN_UNROLL)


def _seg_inputs(h0, end, tot, reverse):
    rows = [None] * N_SEG
    hin = h0
    for s in (range(N_SEG - 1, -1, -1) if reverse else range(N_SEG)):
        rows[s] = hin
        hin = tot[s:s + 1] * hin + end[s:s + 1]
    return rows, hin


def _lru_kernel(u_ref, gy_ref, cw_ref, cb_ref, wa_ref, wi_ref, ba_ref, bi_ref, lam_ref, h0_ref,
                r_ref, ht_ref, uc_ref, a_ref, b_ref, acc_ref, hl_ref, *, t):
    seg, pitch = _seg_rows(t)
    for k in range(LRU_NB):
        uc_ref[k] = _conv4(u_ref[0, k], cw_ref[k], cb_ref[k])
    for d in (1, 0):
        for k in range(LRU_NB):
            c = (-0.5 * LRU_C * math.log2(math.e)) * _softplus(-lam_ref[d, k])
            for s in range(N_SEG):
                u = uc_ref[k, s * seg:(s + 1) * seg, :]
                u16 = u.astype(BF16)
                ta = jnp.tanh(jnp.dot(u16, wa_ref[d, k], preferred_element_type=F32) + ba_ref[d, k])
                ti = jnp.tanh(jnp.dot(u16, wi_ref[d, k], preferred_element_type=F32) + bi_ref[d, k])
                a = jnp.exp2(c * ta + c)
                a_ref[d, k, s * pitch:s * pitch + seg, :] = a
                half_u = 0.5 * u
                gated_u = half_u * ti + half_u
                y = (1.0 - a) * (1.0 + a)
                root = jnp.where(y > 0.0, y * lax.rsqrt(y), 0.0)
                b_ref[d, k, s * pitch:s * pitch + seg, :] = root * gated_u
    chains = [(d, k) for d in (1, 0) for k in range(LRU_NB)]
    scanned = _seg_scan(a_ref, b_ref, acc_ref, hl_ref, chains, seg, pitch, [d == 1 for d, _ in chains])
    for (d, k), (end, tot) in zip(chains, scanned):
        rows, hfin = _seg_inputs(h0_ref[0, d, k], end, tot, reverse=(d == 1))
        ht_ref[0, d, k] = hfin
        for s in range(N_SEG):
            src = slice(s * pitch, s * pitch + seg)
            dst = slice(s * seg, (s + 1) * seg)
            h = hl_ref[d, k, src, :] + acc_ref[d, k, src, :] * rows[s]
            if d == 1:
                r_ref[0, k, dst, :] = h
            else:
                r_ref[0, k, dst, :] = (r_ref[0, k, dst, :] + h) * _gelu_tanh(gy_ref[0, k, dst, :])


def _lru(p, h0, conv_w, conv_b, wa, wi, ba, bi, lam, *, gy_blk0, u_blk0):
    bsz, _, t, _ = p.shape
    nb = LRU_NB
    seg, pitch = _seg_rows(t)
    blk = lambda off: pl.BlockSpec((1, nb, t, LANES), lambda b, g: (b, off // nb + g, 0, 0))
    par = lambda shape: pl.BlockSpec(shape, lambda b, g: (0, g) + (0,) * (len(shape) - 2))
    st_spec = pl.BlockSpec((1, 2, nb, 1, LANES), lambda b, g: (b, 0, g, 0, 0))
    return pl.pallas_call(
        functools.partial(_lru_kernel, t=t),
        grid=(bsz, LRU_BLOCKS // nb),
        in_specs=[blk(u_blk0), blk(gy_blk0),
                  pl.BlockSpec((nb, 4, LANES), lambda b, g: (g, 0, 0)),
                  pl.BlockSpec((nb, 1, LANES), lambda b, g: (g, 0, 0)),
                  par((2, nb, LANES, LANES)), par((2, nb, LANES, LANES)),
                  par((2, nb, 1, LANES)), par((2, nb, 1, LANES)), par((2, nb, 1, LANES)),
                  st_spec],
        out_specs=[pl.BlockSpec((1, nb, t, LANES), lambda b, g: (b, g, 0, 0)), st_spec],
        out_shape=[jax.ShapeDtypeStruct((bsz, LRU_BLOCKS, t, LANES), F32),
                   jax.ShapeDtypeStruct(h0.shape, F32)],
        scratch_shapes=[pltpu.VMEM((nb, t, LANES), F32)] + [pltpu.VMEM((2, nb, N_SEG * pitch, LANES), F32)] * 4,
        compiler_params=_cparams(("parallel", "parallel")),
        name="rglru",
    )(p, p, conv_w, conv_b, wa, wi, ba, bi, lam, h0)


LRU_W = LRU_BLOCKS * LANES
EV_Z0 = 0
EV_XBC0 = EV_Z0 + SSD_W_BLKS
EV_DT0 = EV_XBC0 + SSD_XBC_BLKS
EV_GY0 = EV_DT0 + SSD_GROUPS
EV_U0 = EV_GY0 + LRU_BLOCKS


def _blocks(v, n):
    return v.reshape(n, 1, LANES)


def _per_group_heads(v):
    v = v.reshape(2, SSD_GROUPS, SSD_HPG).transpose(1, 0, 2).reshape(SSD_GROUPS, 2 * SSD_HPG)
    return jnp.pad(v, ((0, 0), (0, LANES - 2 * SSD_HPG))).reshape(SSD_GROUPS, 1, LANES)


def _even_weights(w_in, conv_w, conv_b, dt_bias, a_log, ssd_d, lru_conv_w, lru_conv_b, w_a, b_a, w_i, b_i, lam):
    d = w_in.shape[0]
    splits = np.cumsum([SSD_W, SSD_XBC, 2 * SSD_HEADS, LRU_W])
    z, xbc, dt, gy, u = jnp.split(w_in, [int(v) for v in splits], axis=1)
    dt = dt.reshape(d, 2, SSD_GROUPS, SSD_HPG).transpose(0, 2, 1, 3).reshape(d, SSD_GROUPS, 2 * SSD_HPG)
    dt = jnp.pad(dt, ((0, 0), (0, 0), (0, LANES - 2 * SSD_HPG))).reshape(d, SSD_GROUPS * LANES)
    return dict(
        w_in=jnp.concatenate([z, xbc, dt, gy, u], axis=1).astype(BF16),
        conv_w=conv_w.reshape(4, SSD_XBC_BLKS, LANES).transpose(1, 0, 2), conv_b=_blocks(conv_b, SSD_XBC_BLKS),
        dtb=_per_group_heads(dt_bias), alog=_per_group_heads(a_log),
        dvec=_blocks(jnp.repeat(ssd_d, SSD_HEAD_DIM), SSD_W_BLKS),
        lru_conv_w=lru_conv_w.reshape(4, LRU_BLOCKS, LANES).transpose(1, 0, 2),
        lru_conv_b=_blocks(lru_conv_b, LRU_BLOCKS),
        w_a=(0.5 * w_a).astype(BF16), w_i=(0.5 * w_i).astype(BF16),
        b_a=0.5 * b_a.reshape(2, LRU_BLOCKS, 1, LANES), b_i=0.5 * b_i.reshape(2, LRU_BLOCKS, 1, LANES),
        lam=lam.reshape(2, LRU_BLOCKS, 1, LANES))


def _even_scans(p, states, w, need_out=True):
    del need_out
    ssd_h, lru_h = states
    y, ssd_h = _ssd(p, ssd_h, w["conv_w"], w["conv_b"], w["dtb"], w["alog"], w["dvec"],
                    z_blk0=EV_Z0, xbc_blk0=EV_XBC0, dt_blk0=EV_DT0)
    r, lru_h = _lru(p, lru_h, w["lru_conv_w"], w["lru_conv_b"], w["w_a"], w["w_i"], w["b_a"], w["b_i"],
                    w["lam"], gy_blk0=EV_GY0, u_blk0=EV_U0)
    return y, r, (ssd_h, lru_h)


def _even_zero_states(bsz):
    return (jnp.zeros((bsz, 2, SSD_GROUPS, SSD_GBLK, SSD_STATE, LANES), F32),
            jnp.zeros((bsz, 2, LRU_BLOCKS, 1, LANES), F32))


HG_CHUNK = 128
HG_PAR = 3
SUBLANES = 8


def _group_boundary(cum, c, reverse):
    m = c // 2
    off = m if reverse else m - 1
    n = cum.shape[0]
    if c >= 2 * SUBLANES:
        r = cum.reshape(n // c, c, LANES)
        return jnp.broadcast_to(r[:, off:off + 1, :], r.shape).reshape(n, LANES)
    r = cum.reshape(n // SUBLANES, SUBLANES, LANES)
    sub = lax.broadcasted_iota(jnp.int32, r.shape, 1)
    p = None
    for g0 in range(0, SUBLANES, c):
        cand = jnp.broadcast_to(r[:, g0 + off:g0 + off + 1, :], r.shape)
        p = cand if p is None else jnp.where(sub >= g0, cand, p)
    return p.reshape(n, LANES)


def _hgrn_masks():
    l = np.arange(HG_CHUNK)[:, None]
    s = np.arange(HG_CHUNK)[None, :]
    fwd = []
    size = HG_CHUNK
    while size >= 2:
        half = size // 2
        fwd.append((l // size == s // size) & (l % size >= half) & (s % size < half))
        size = half
    fwd = np.stack(fwd).astype(np.float32)
    pairs = np.stack([fwd, fwd.transpose(0, 2, 1)])
    tri = np.stack([l >= s, l <= s]).astype(np.float32)
    sign = np.stack([np.where(pairs[d].any(axis=2, keepdims=True), 1.0, -1.0) for d in range(2)])
    sign = np.broadcast_to(sign, pairs.shape).astype(np.float32)
    return jnp.asarray(pairs), jnp.asarray(tri, dtype=BF16), jnp.asarray(sign)


def _hgrn_chunks(jobs, q_ref, ff_ref, fb_ref, v_ref, lb_ref, sf_ref, sb_ref, pairs_ref, tri_ref, sign_ref,
                 need_out):
    n = len(jobs)
    qq, kk, vv, cum = [], [], [], []
    for hd, reverse, rows in jobs:
        lb = lb_ref[hd]
        fx = (fb_ref if reverse else ff_ref)[0, hd, rows, :]
        e = jnp.exp(-jnp.abs(fx))
        big = 1.0 / (1.0 + e)
        small = e * big
        pos = fx >= 0.0
        log2_f = jnp.log2(lb + (1.0 - lb) * jnp.where(pos, big, small))
        kk.append((1.0 - lb) * jnp.where(pos, small, big))
        cum.append(_dot_select(tri_ref[1 if reverse else 0], log2_f))
    for hd, reverse, rows in jobs:
        qq.append(_silu(q_ref[0, hd, rows, :]) if need_out else None)
        vv.append(v_ref[0, hd, rows, :])
    att = [None] * n
    q16 = [v.astype(BF16) for v in qq] if need_out else None
    k16 = [v.astype(BF16) for v in kk]
    size = HG_CHUNK
    level = 0
    while need_out and size >= 2:
        for j, (hd, reverse, rows) in enumerate(jobs):
            dist = (cum[j] - _group_boundary(cum[j], size, reverse)) * sign_ref[1 if reverse else 0, level]
            fac = jnp.exp2(dist).astype(BF16)
            a_l = lax.dot_general(q16[j] * fac, k16[j] * fac, (((1,), (1,)), ((), ())),
                                  preferred_element_type=F32)
            a_l = a_l * pairs_ref[1 if reverse else 0, level]
            att[j] = a_l if att[j] is None else att[j] + a_l
        size //= 2
        level += 1
    outs = []
    for j, (hd, reverse, rows) in enumerate(jobs):
        s_ref = (sb_ref if reverse else sf_ref).at[hd]
        sv = s_ref[...]
        o = None
        if need_out:
            diag = jnp.sum(qq[j] * kk[j], axis=-1, keepdims=True)
            o = jnp.dot(att[j].astype(BF16), vv[j].astype(BF16), preferred_element_type=F32) + diag * vv[j]
            o = o + jnp.dot((qq[j] * jnp.exp2(cum[j])).astype(BF16), sv.astype(BF16),
                            preferred_element_type=F32)
        end_row = 0 if reverse else HG_CHUNK - 1
        cum_end = cum[j][end_row:end_row + 1, :]
        kw = (kk[j] * jnp.exp2(cum_end - cum[j])).T.astype(BF16)
        keep = jnp.broadcast_to(jnp.exp2(cum_end), (HG_CHUNK, LANES)).T
        s_ref[...] = sv * keep + jnp.dot(kw, vv[j].astype(BF16), preferred_element_type=F32)
        outs.append(o)
    return outs


def _hgrn_kernel(q_ref, ff_ref, fb_ref, v_ref, g_ref, lb_ref, ng_ref, pairs_ref, tri_ref, sign_ref, h0_ref,
                 *rest, t, need_out):
    if need_out:
        o_ref, ht_ref, sf_ref, sb_ref = rest
    else:
        ht_ref, sf_ref, sb_ref = rest
    nchunk = t // HG_CHUNK
    for hd in range(HG_PAR):
        sf_ref[hd] = h0_ref[0, 0, hd]
        sb_ref[hd] = h0_ref[0, 1, hd]

    def pair(i, carry, second):
        jobs = []
        for hd in range(HG_PAR):
            for reverse in (False, True):
                c = (nchunk - 1 - i) if reverse else i
                jobs.append((hd, reverse, pl.ds(pl.multiple_of(c * HG_CHUNK, HG_CHUNK), HG_CHUNK)))
        outs = _hgrn_chunks(jobs, q_ref, ff_ref, fb_ref, v_ref, lb_ref, sf_ref, sb_ref, pairs_ref, tri_ref,
                            sign_ref, need_out)
        if need_out:
            for (hd, reverse, rows), o in zip(jobs, outs):
                o_ref[0, hd, rows, :] = (o_ref[0, hd, rows, :] + o) if second else o
        return carry

    def finish(c, carry):
        rows = pl.ds(pl.multiple_of(c * HG_CHUNK, HG_CHUNK), HG_CHUNK)
        for hd in range(HG_PAR):
            o_ref[0, hd, rows, :] = _rms(o_ref[0, hd, rows, :], ng_ref[hd]) * _silu(g_ref[0, hd, rows, :])
        return carry

    lax.fori_loop(0, nchunk // 2, functools.partial(pair, second=False), 0)
    lax.fori_loop(nchunk // 2, nchunk, functools.partial(pair, second=True), 0)
    if need_out:
        lax.fori_loop(0, nchunk, finish, 0)
    for hd in range(HG_PAR):
        ht_ref[0, 0, hd] = sf_ref[hd]
        ht_ref[0, 1, hd] = sb_ref[hd]


def _hgrn(p, h0, lb, norm_g, need_out=True):
    bsz, _, t, _ = p.shape
    assert (t // HG_CHUNK) % 2 == 0
    hp = HG_PAR
    blk = lambda off: pl.BlockSpec((1, hp, t, LANES), lambda b, h: (b, off // hp + h, 0, 0))
    par = pl.BlockSpec((hp, 1, LANES), lambda b, h: (h, 0, 0))
    st_spec = pl.BlockSpec((1, 2, hp, LANES, LANES), lambda b, h: (b, 0, h, 0, 0))
    pairs, tri, sign = _hgrn_masks()
    o_spec = [pl.BlockSpec((1, hp, t, LANES), lambda b, h: (b, h, 0, 0))] if need_out else []
    o_shape = [jax.ShapeDtypeStruct((bsz, HG_HEADS, t, LANES), F32)] if need_out else []
    outs = pl.pallas_call(
        functools.partial(_hgrn_kernel, t=t, need_out=need_out),
        grid=(bsz, HG_HEADS // hp),
        in_specs=[blk(0), blk(HG_HEADS), blk(2 * HG_HEADS), blk(3 * HG_HEADS), blk(4 * HG_HEADS),
                  par, par,
                  pl.BlockSpec(pairs.shape, lambda b, h: (0, 0, 0, 0)),
                  pl.BlockSpec(tri.shape, lambda b, h: (0, 0, 0)),
                  pl.BlockSpec(sign.shape, lambda b, h: (0, 0, 0, 0)),
                  st_spec],
        out_specs=o_spec + [st_spec],
        out_shape=o_shape + [jax.ShapeDtypeStruct(h0.shape, F32)],
        scratch_shapes=[pltpu.VMEM((hp, LANES, LANES), F32), pltpu.VMEM((hp, LANES, LANES), F32)],
        compiler_params=_cparams(("parallel", "parallel")),
        name="hgrn2",
    )(p, p, p, p, p, lb.reshape(HG_HEADS, 1, LANES), norm_g.reshape(HG_HEADS, 1, LANES), pairs, tri, sign,
      h0)
    return tuple(outs) if need_out else (None, outs[0])


S5_W = S5_GROUPS * S5_GROUP_CH
S5_NSTATE = S5_GROUPS * S5_STATE
S5_NB = S5_NSTATE // LANES
S5_PAR = 2


def _s5_kernel(u_ref, are_ref, aim_ref, pw_ref, bd_ref, cd_ref, dvec_ref, gw_ref, gb_ref, h0_ref,
               y_ref, ht_ref, wr_ref, wi_ref, hr_ref, hi_ref, acc_ref, *, t):
    seg, pitch = _seg_rows(t)
    u = jnp.concatenate([u_ref[0, 0], u_ref[0, 1]], axis=-1)
    u16 = u.astype(BF16)
    acc_ref[...] = dvec_ref[...] * u
    chains = [(d, k) for d in (0, 1) for k in range(S5_PAR)]
    for j0 in range(0, S5_NB, S5_PAR):
        cols = slice(j0 * LANES, (j0 + S5_PAR) * LANES)
        group = max(1, LANES // seg)
        for s0 in range(0, N_SEG, group):
            us = u16[s0 * seg:(s0 + group) * seg]
            wr = jnp.dot(us, bd_ref[0, :, cols], preferred_element_type=F32)
            wi = jnp.dot(us, bd_ref[1, :, cols], preferred_element_type=F32)
            for s in range(s0, s0 + group):
                part = slice((s - s0) * seg, (s - s0 + 1) * seg)
                for k in range(S5_PAR):
                    wr_ref[k, s * pitch:s * pitch + seg, :] = wr[part, k * LANES:(k + 1) * LANES]
                    wi_ref[k, s * pitch:s * pitch + seg, :] = wi[part, k * LANES:(k + 1) * LANES]
        ar = [jnp.broadcast_to(are_ref[d, j0 + k], (N_SEG, LANES)) for d, k in chains]
        ai = [jnp.broadcast_to(aim_ref[d, j0 + k], (N_SEG, LANES)) for d, k in chains]

        def step(i, carry, ar=ar, ai=ai):
            out = []
            for c, (d, k) in enumerate(chains):
                gr, gi = carry[c]
                j = (seg - 1 - i) if d == 1 else i
                src = (k, pl.ds(j, N_SEG, stride=pitch), slice(None))
                dst = (d, k, pl.ds(j, N_SEG, stride=pitch), slice(None))
                ngr = ar[c] * gr - ai[c] * gi + wr_ref[src]
                ngi = ar[c] * gi + ai[c] * gr + wi_ref[src]
                hr_ref[dst] = ngr
                hi_ref[dst] = ngi
                out.append((ngr, ngi))
            return tuple(out)

        zero = jnp.zeros((N_SEG, LANES), F32)
        fin = lax.fori_loop(0, seg, step, tuple((zero, zero) for _ in chains), unroll=SCAN_UNROLL)
        for d in (0, 1):
            ins = []
            last = 0 if d == 1 else seg - 1
            for k in range(S5_PAR):
                end_r, end_i = fin[d * S5_PAR + k]
                tot_r = pw_ref[d, 0, j0 + k, last:last + 1, :]
                tot_i = pw_ref[d, 1, j0 + k, last:last + 1, :]
                hin_r, hin_i = h0_ref[0, d, 0, j0 + k], h0_ref[0, d, 1, j0 + k]
                rows = [None] * N_SEG
                for s in (range(N_SEG - 1, -1, -1) if d == 1 else range(N_SEG)):
                    rows[s] = (hin_r, hin_i)
                    hin_r, hin_i = (tot_r[0:1] * hin_r - tot_i[0:1] * hin_i + end_r[s:s + 1],
                                    tot_r[0:1] * hin_i + tot_i[0:1] * hin_r + end_i[s:s + 1])
                ht_ref[0, d, 0, j0 + k] = hin_r
                ht_ref[0, d, 1, j0 + k] = hin_i
                ins.append(rows)
            gr_rows, gi_rows = [], []
            for s in range(N_SEG):
                src = slice(s * pitch, s * pitch + seg)
                gr_parts, gi_parts = [], []
                for k in range(S5_PAR):
                    in_r, in_i = ins[k][s]
                    pr, pi = pw_ref[d, 0, j0 + k], pw_ref[d, 1, j0 + k]
                    gr_parts.append((hr_ref[d, k, src, :] + pr * in_r - pi * in_i).astype(BF16))
                    gi_parts.append((hi_ref[d, k, src, :] + pr * in_i + pi * in_r).astype(BF16))
                gr_rows.append(jnp.concatenate(gr_parts, axis=-1))
                gi_rows.append(jnp.concatenate(gi_parts, axis=-1))
                if len(gr_rows) == group:
                    rows = slice((s + 1 - group) * seg, (s + 1) * seg)
                    acc_ref[rows, :] += (
                        jnp.dot(jnp.concatenate(gr_rows, axis=0), cd_ref[d, 0, cols, :],
                                preferred_element_type=F32)
                        + jnp.dot(jnp.concatenate(gi_rows, axis=0), cd_ref[d, 1, cols, :],
                                  preferred_element_type=F32))
                    gr_rows, gi_rows = [], []
    y = _gelu_tanh(acc_ref[...])
    y = y * _sigmoid(jnp.dot(y.astype(BF16), gw_ref[...], preferred_element_type=F32) + gb_ref[...])
    y_ref[0, 0] = y[:, :LANES]
    y_ref[0, 1] = y[:, LANES:]


def _s5(p, h0, log_a, bd, cd, dvec, glu_w, glu_b, *, u_blk0):
    bsz, _, t, _ = p.shape
    seg, pitch = _seg_rows(t)
    la_re, la_im = log_a
    pos = jnp.arange(seg, dtype=F32)
    n = jnp.stack([pos + 1.0, seg - pos]).reshape(2, 1, seg, 1)
    mag = jnp.exp(n * la_re)
    pw = jnp.stack([mag * jnp.cos(n * la_im), mag * jnp.sin(n * la_im)], axis=1)
    a_re, a_im = pw[0, 0, :, 0:1, :], pw[0, 1, :, 0:1, :]
    a_re = jnp.stack([a_re, pw[1, 0, :, seg - 1:seg, :]])
    a_im = jnp.stack([a_im, pw[1, 1, :, seg - 1:seg, :]])
    full = lambda a: pl.BlockSpec(a.shape, lambda b: (0,) * a.ndim)
    st_spec = pl.BlockSpec((1, 2, 2, S5_NB, 1, LANES), lambda b: (b, 0, 0, 0, 0, 0))
    return pl.pallas_call(
        functools.partial(_s5_kernel, t=t),
        grid=(bsz,),
        in_specs=[pl.BlockSpec((1, 2, t, LANES), lambda b: (b, u_blk0 // 2, 0, 0)),
                  full(a_re), full(a_im), full(pw), full(bd), full(cd), full(dvec), full(glu_w), full(glu_b),
                  st_spec],
        out_specs=[pl.BlockSpec((1, 2, t, LANES), lambda b: (b, 0, 0, 0)), st_spec],
        out_shape=[jax.ShapeDtypeStruct((bsz, 2, t, LANES), F32), jax.ShapeDtypeStruct(h0.shape, F32)],
        scratch_shapes=[pltpu.VMEM((S5_PAR, N_SEG * pitch, LANES), F32),
                        pltpu.VMEM((S5_PAR, N_SEG * pitch, LANES), F32),
                        pltpu.VMEM((2, S5_PAR, N_SEG * pitch, LANES), F32),
                        pltpu.VMEM((2, S5_PAR, N_SEG * pitch, LANES), F32),
                        pltpu.VMEM((t, S5_W), F32)],
        compiler_params=_cparams(("parallel",)),
        name="s5",
    )(p, a_re, a_im, pw, bd, cd, dvec, glu_w, glu_b, h0)


def _s5_params(lam_re, lam_im, log_step, b_re, b_im, c_re, c_im):
    step = jnp.exp(log_step)[..., None]
    mag = jnp.exp(lam_re * step)
    ar, ai = mag * jnp.cos(lam_im * step), mag * jnp.sin(lam_im * step)
    den = lam_re * lam_re + lam_im * lam_im
    zr = ((ar - 1) * lam_re + ai * lam_im) / den
    zi = (ai * lam_re - (ar - 1) * lam_im) / den
    czr = c_re * zr[:, :, None, :] - c_im * zi[:, :, None, :]
    czi = c_re * zi[:, :, None, :] + c_im * zr[:, :, None, :]
    eye = jnp.eye(S5_GROUPS, dtype=F32)

    def in_mat(m):
        return jnp.einsum("gpk,gh->gkhp", m, eye).reshape(S5_W, S5_NSTATE)

    def out_mat(m):
        return jnp.einsum("dgkp,gh->dgphk", m, eye).reshape(2, S5_NSTATE, S5_W)

    bd = jnp.stack([in_mat(b_re), in_mat(b_im)], axis=0).astype(BF16)
    cd = jnp.stack([out_mat(czr), -out_mat(czi)], axis=1).astype(BF16)
    shape = (2, S5_NB, 1, LANES)
    log_a = ((lam_re * step).reshape(shape), (lam_im * step).reshape(shape))
    return log_a, bd, cd


OD_U0 = 5 * HG_HEADS


def _odd_weights(w_in, lower_bound, hg_norm_g, lam_re, lam_im, log_step, b_re, b_im, c_re, c_im, s5_d,
                 glu_w, glu_b):
    log_a, bd, cd = _s5_params(lam_re, lam_im, log_step, b_re, b_im, c_re, c_im)
    return dict(w_in=w_in.astype(BF16), lb=lower_bound, norm_g=hg_norm_g, log_a=log_a, bd=bd, cd=cd,
                dvec=s5_d.reshape(1, S5_W), glu_w=glu_w.astype(BF16), glu_b=glu_b.reshape(1, S5_W))


def _odd_scans(p, states, w, need_out=True):
    hg_h, s5_h = states
    o, hg_h = _hgrn(p, hg_h, w["lb"], w["norm_g"], need_out)
    y, s5_h = _s5(p, s5_h, w["log_a"], w["bd"], w["cd"], w["dvec"], w["glu_w"], w["glu_b"], u_blk0=OD_U0)
    return o, y, (hg_h, s5_h)


def _odd_zero_states(bsz):
    return (jnp.zeros((bsz, 2, HG_HEADS, LANES, LANES), F32),
            jnp.zeros((bsz, 2, 2, S5_NB, 1, LANES), F32))


PROJ_TILE = 512
MIX_TILE = 1024
FFN_TILE = 1024
FFN_COLS = 256


def kernel(x, c, ctx, c_ctx, w_mod, b_mod, norm_mix_g, norm_ffn_g, final_norm_g,
           ev_w_in, ev_w_out, ssd_conv_w, ssd_conv_b, ssd_dt_bias, ssd_a_log, ssd_d, ssd_norm_g,
           lru_conv_w, lru_conv_b, lru_w_a, lru_b_a, lru_w_i, lru_b_i, lru_lam,
           od_w_in, od_w_out, hg_lb_logits, hg_norm_g,
           s5_lam_re, s5_lam_im, s5_log_step, s5_b_re, s5_b_im, s5_c_re, s5_c_im, s5_d,
           s5_glu_w, s5_glu_b,
           ffn_w_gate, ffn_w_up, ffn_conv_w, ffn_conv_b, ffn_w_down):
    bsz, _, d = x.shape
    depth = w_mod.shape[0]
    prob = jax.nn.softmax(hg_lb_logits.astype(F32), axis=0)
    lower_bounds = (jnp.cumsum(prob, axis=0) - prob[0]).astype(hg_lb_logits.dtype)

    pad = (-(bsz + 1)) % SUBLANES
    cond = jnp.concatenate([c, c_ctx[None], jnp.zeros((pad, d), c.dtype)], axis=0)
    mods = _modulation(cond, w_mod, b_mod).transpose(0, 2, 1, 3)

    for layer in range(depth):
        last = layer == depth - 1
        j = layer // 2
        mod_x = mods[layer, :bsz]
        mod_c = mods[layer, bsz:bsz + 1]
        if layer % 2 == 0:
            w = _even_weights(ev_w_in[j], ssd_conv_w[j], ssd_conv_b[j], ssd_dt_bias[j], ssd_a_log[j], ssd_d[j],
                              lru_conv_w[j], lru_conv_b[j], lru_w_a[j], lru_b_a[j], lru_w_i[j], lru_b_i[j],
                              lru_lam[j])
            scans, zero_states = _even_scans, _even_zero_states
            w_out, norm_gain, norm_first = ev_w_out[j].astype(BF16), ssd_norm_g[j], True
        else:
            w = _odd_weights(od_w_in[j], lower_bounds[layer], hg_norm_g[j], s5_lam_re[j], s5_lam_im[j],
                             s5_log_step[j], s5_b_re[j], s5_b_im[j], s5_c_re[j], s5_c_im[j], s5_d[j],
                             s5_glu_w[j], s5_glu_b[j])
            scans, zero_states = _odd_scans, _odd_zero_states
            w_out, norm_gain, norm_first = od_w_out[j].astype(BF16), hg_norm_g[j], False
        ffn_w = _ffn_weights(ffn_w_gate[layer], ffn_w_up[layer], ffn_conv_w[layer], ffn_conv_b[layer],
                             ffn_w_down[layer], FFN_COLS)

        p_c = _project(ctx, mod_c, norm_mix_g[layer], w["w_in"], PROJ_TILE)
        a_c, b_c, states = scans(p_c, zero_states(bsz), w, need_out=not last)
        p_x = _project(x, mod_x, norm_mix_g[layer], w["w_in"], PROJ_TILE)
        a_x, b_x, _ = scans(p_x, states, w)
        x = _mix_out(x, a_x, b_x, mod_x, norm_gain, w_out, norm_first, MIX_TILE)
        x = _conv_ffn(x, mod_x, norm_ffn_g[layer], final_norm_g, *ffn_w, grid_conv=True, final_norm=last,
                      tile=FFN_TILE, tf=FFN_COLS)
        if not last:
            ctx = _mix_out(ctx, a_c, b_c, mod_c, norm_gain, w_out, norm_first, MIX_TILE)
            ctx = _conv_ffn(ctx, mod_c, norm_ffn_g[layer], final_norm_g, *ffn_w, grid_conv=False,
                            final_norm=False, tile=FFN_TILE, tf=FFN_COLS)
    return x
```

```python
import functools
import math

import jax
import jax.numpy as jnp
import numpy as np
from jax import lax
from jax.experimental import pallas as pl
from jax.experimental.pallas import tpu as pltpu

LANES = 128
SUBLANES = 8
RMS_EPS = 1e-6
N_MOD = 6
GRID_W = 64
SSD_HEAD_DIM = 64
SSD_HEADS = 16
SSD_GROUPS = 2
SSD_HPG = SSD_HEADS // SSD_GROUPS
SSD_STATE = 128
SSD_CHUNK = 128
SSD_W = SSD_HEADS * SSD_HEAD_DIM
SSD_XBC = SSD_W + 2 * SSD_GROUPS * SSD_STATE
SSD_W_BLKS, SSD_XBC_BLKS = SSD_W // LANES, SSD_XBC // LANES
SSD_GBLK = SSD_W_BLKS // SSD_GROUPS
SSD_B0, SSD_C0 = SSD_W_BLKS, SSD_W_BLKS + SSD_GROUPS
LRU_BLOCKS = 8
LRU_C = 8.0
HG_HEADS = 6
S5_GROUPS = 16
S5_GROUP_CH = 16
S5_STATE = 64
VMEM_LIMIT = 56 * 1024 * 1024

BF16 = jnp.bfloat16
F32 = jnp.float32


def _cparams(sem):
    return pltpu.CompilerParams(dimension_semantics=sem, vmem_limit_bytes=VMEM_LIMIT)


def _dot(a, b):
    return jnp.dot(a.astype(BF16), b.astype(BF16), preferred_element_type=F32)


def _dot_select(sel, v):
    hi = v.astype(BF16)
    rest = v - hi.astype(F32)
    mid = rest.astype(BF16)
    lo = (rest - mid.astype(F32)).astype(BF16)
    return (jnp.dot(sel, hi, preferred_element_type=F32) + jnp.dot(sel, mid, preferred_element_type=F32)
            + jnp.dot(sel, lo, preferred_element_type=F32))


def _sigmoid(v):
    return 0.5 * jnp.tanh(0.5 * v) + 0.5


def _silu(v):
    h = 0.5 * v
    return h * jnp.tanh(h) + h


def _gelu_tanh(v):
    return 0.5 * v * (1.0 + jnp.tanh(math.sqrt(2.0 / math.pi) * (v + 0.044715 * (v * v * v))))


def _softplus(v):
    return jnp.maximum(v, 0.0) + jnp.log(1.0 + jnp.exp(-jnp.abs(v)))


def _rms(v, g):
    return v * lax.rsqrt(jnp.mean(v * v, axis=-1, keepdims=True) + RMS_EPS) * g


def _norm_mod(xv, g, shift, scale):
    return _rms(xv, g) * (1.0 + scale) + shift


def _mod_kernel(s_ref, w_ref, b_ref, o_ref):
    o_ref[0, 0] = _dot(_silu(s_ref[...]), w_ref[0]) + b_ref[0, 0]


def _modulation(s, w_mod, b_mod):
    depth, d, _ = w_mod.shape
    rows = s.shape[0]
    return pl.pallas_call(
        _mod_kernel,
        grid=(depth, N_MOD),
        in_specs=[pl.BlockSpec((rows, d), lambda l, j: (0, 0)),
                  pl.BlockSpec((1, d, d), lambda l, j: (l, 0, j)),
                  pl.BlockSpec((1, 1, 1, d), lambda l, j: (l, j, 0, 0))],
        out_specs=pl.BlockSpec((1, 1, rows, d), lambda l, j: (l, j, 0, 0)),
        out_shape=jax.ShapeDtypeStruct((depth, N_MOD, rows, d), F32),
        compiler_params=_cparams(("arbitrary", "arbitrary")),
        name="modulation",
    )(s, w_mod, b_mod.reshape(depth, N_MOD, 1, d))


def _proj_kernel(x_ref, mod_ref, g_ref, w_ref, o_ref, *, nblk):
    m = mod_ref[0]
    h = _norm_mod(x_ref[0], g_ref[...], m[0:1], m[1:2]).astype(BF16)
    group = 4
    for b0 in range(0, nblk, group):
        nb = min(group, nblk - b0)
        r = jnp.dot(h, w_ref[:, b0 * LANES:(b0 + nb) * LANES], preferred_element_type=F32)
        for k in range(nb):
            o_ref[0, b0 + k] = r[:, k * LANES:(k + 1) * LANES]


def _project(x, mod, gain, w, tile):
    bsz, t, d = x.shape
    nblk = w.shape[1] // LANES
    tile = min(tile, t)
    mod_map = (lambda b, i: (b, 0, 0)) if mod.shape[0] == bsz else (lambda b, i: (0, 0, 0))
    return pl.pallas_call(
        functools.partial(_proj_kernel, nblk=nblk),
        grid=(bsz, t // tile),
        in_specs=[pl.BlockSpec((1, tile, d), lambda b, i: (b, i, 0)),
                  pl.BlockSpec((1, N_MOD, d), mod_map),
                  pl.BlockSpec((1, d), lambda b, i: (0, 0)),
                  pl.BlockSpec((d, nblk * LANES), lambda b, i: (0, 0), pipeline_mode=pl.Buffered(1))],
        out_specs=pl.BlockSpec((1, nblk, tile, LANES), lambda b, i: (b, 0, i, 0)),
        out_shape=jax.ShapeDtypeStruct((bsz, nblk, t, LANES), F32),
        compiler_params=_cparams(("parallel", "parallel")),
        name="project",
    )(x, mod, gain.reshape(1, d), w)


def _mix_out_kernel(x_ref, ma_ref, mb_ref, mod_ref, ng_ref, w_ref, o_ref, *, norm_first):
    m = mod_ref[0]
    pa = [ma_ref[0, k] for k in range(ma_ref.shape[1])]
    pb = [mb_ref[0, k].astype(BF16) for k in range(mb_ref.shape[1])]
    if norm_first:
        pa = [_rms(jnp.concatenate(pa, axis=-1), ng_ref[...]).astype(BF16)]
    else:
        pa = [p.astype(BF16) for p in pa]
    v = jnp.concatenate(pa + pb, axis=-1)
    o_ref[0] = x_ref[0] + m[2:3] * jnp.dot(v, w_ref[...], preferred_element_type=F32)


def _mix_out(x, mix_a, mix_b, mod, norm_gain, w, norm_first, tile):
    bsz, t, d = x.shape
    tile = min(tile, t)
    mod_map = (lambda b, i: (b, 0, 0)) if mod.shape[0] == bsz else (lambda b, i: (0, 0, 0))
    ng = norm_gain.reshape(1, -1)
    mix_spec = lambda a: pl.BlockSpec((1, a.shape[1], tile, LANES), lambda b, i: (b, 0, i, 0))
    return pl.pallas_call(
        functools.partial(_mix_out_kernel, norm_first=norm_first),
        grid=(bsz, t // tile),
        in_specs=[pl.BlockSpec((1, tile, d), lambda b, i: (b, i, 0)),
                  mix_spec(mix_a), mix_spec(mix_b),
                  pl.BlockSpec((1, N_MOD, d), mod_map),
                  pl.BlockSpec(ng.shape, lambda b, i: (0, 0)),
                  pl.BlockSpec(w.shape, lambda b, i: (0, 0), pipeline_mode=pl.Buffered(1))],
        out_specs=pl.BlockSpec((1, tile, d), lambda b, i: (b, i, 0)),
        out_shape=jax.ShapeDtypeStruct((bsz, t, d), F32),
        compiler_params=_cparams(("parallel", "parallel")),
        name="mix_out",
    )(x, mix_a, mix_b, mod, ng, w)


def _ffn_kernel(x_ref, xp_ref, xn_ref, mod_ref, g_ref, fg_ref, wg_ref, wu_ref, cw_ref, cb_ref, wd_ref,
                o_ref, fx_ref, gt_ref, a0_ref, a1_ref, *, tile, halo, tf, grid_conv, period, final_norm):
    i = pl.program_id(1)
    nt = pl.num_programs(1)
    nf = wg_ref.shape[0]
    m = mod_ref[0]
    fx_ref[halo:halo + tile] = _norm_mod(x_ref[0], g_ref[...], m[3:4], m[4:5]).astype(BF16)
    if halo:
        keep_p = jnp.where(i > 0, 1.0, 0.0)
        keep_n = jnp.where(i < nt - 1, 1.0, 0.0)
        fx_ref[0:halo] = (keep_p * _norm_mod(xp_ref[0], g_ref[...], m[3:4], m[4:5])).astype(BF16)
        fx_ref[halo + tile:] = (keep_n * _norm_mod(xn_ref[0], g_ref[...], m[3:4], m[4:5])).astype(BF16)

    def gate(f, dst_ref):
        dst_ref[...] = jnp.dot(fx_ref[...], wg_ref[f], preferred_element_type=F32)

    def column(f, src_ref):
        a = src_ref[...]
        cw = cw_ref[f]
        rows = a.shape[0]
        pos = lax.broadcasted_iota(jnp.int32, a.shape, 0)
        if grid_conv:
            col = pos % GRID_W
            a_m1 = jnp.where(col == 0, 0.0, pltpu.roll(a, 1, axis=0))
            a_p1 = jnp.where(col == GRID_W - 1, 0.0, pltpu.roll(a, rows - 1, axis=0))
            conv = cb_ref[f]
            for dr in range(3):
                lo = dr * GRID_W
                conv = conv + (cw[3 * dr + 0:3 * dr + 1] * a_m1[lo:lo + tile]
                               + cw[3 * dr + 1:3 * dr + 2] * a[lo:lo + tile]
                               + cw[3 * dr + 2:3 * dr + 3] * a_p1[lo:lo + tile])
        else:
            col = pos % period
            a_m1 = jnp.where(col == 0, 0.0, pltpu.roll(a, 1, axis=0))
            a_p1 = jnp.where(col == period - 1, 0.0, pltpu.roll(a, rows - 1, axis=0))
            conv = cb_ref[f] + cw[3:4] * a_m1 + cw[4:5] * a + cw[5:6] * a_p1
        act = _silu(conv)
        up = jnp.dot(fx_ref[halo:halo + tile], wu_ref[f], preferred_element_type=F32)
        start = f * tf if isinstance(f, int) else pl.multiple_of(f * tf, tf)
        gt_ref[:, pl.ds(start, tf)] = (act * up).astype(BF16)

    def column_pair(p, carry):
        f = 2 * p
        gate(f + 1, a1_ref)
        column(f, a0_ref)
        gate(f + 2, a0_ref)
        column(f + 1, a1_ref)
        return carry

    gate(0, a0_ref)
    pairs = (nf - 1) // 2
    lax.fori_loop(0, pairs, column_pair, 0)
    if nf - 2 * pairs == 2:
        gate(nf - 1, a1_ref)
        column(nf - 2, a0_ref)
        column(nf - 1, a1_ref)
    else:
        column(nf - 1, a0_ref)
    y = x_ref[0] + m[5:6] * jnp.dot(gt_ref[...], wd_ref[...], preferred_element_type=F32)
    if final_norm:
        y = _rms(y, fg_ref[...])
    o_ref[0] = y


def _conv_ffn(x, mod, gain, final_gain, w_gate, w_up, conv_w, conv_b, w_down, grid_conv, final_norm,
              tile, tf):
    shape = x.shape
    period = x.shape[1]
    if not grid_conv and mod.shape[0] == 1 and tile % period == 0 and (x.shape[0] * period) % tile == 0:
        x = x.reshape(-1, tile, x.shape[2])
    bsz, t, d = x.shape
    nf = w_gate.shape[0]
    dff = nf * tf
    tile = min(tile, t)
    halo = GRID_W if grid_conv else 0
    nh = t // GRID_W
    per = tile // GRID_W
    mod_map = (lambda b, i: (b, 0, 0)) if mod.shape[0] == bsz else (lambda b, i: (0, 0, 0))
    resident = lambda a: pl.BlockSpec(a.shape, lambda b, i: (0,) * a.ndim, pipeline_mode=pl.Buffered(1))
    return pl.pallas_call(
        functools.partial(_ffn_kernel, tile=tile, halo=halo, tf=tf, grid_conv=grid_conv, period=period,
                          final_norm=final_norm),
        grid=(bsz, t // tile),
        in_specs=[pl.BlockSpec((1, tile, d), lambda b, i: (b, i, 0)),
                  pl.BlockSpec((1, GRID_W, d), lambda b, i: (b, jnp.maximum(i * per - 1, 0), 0)),
                  pl.BlockSpec((1, GRID_W, d), lambda b, i: (b, jnp.minimum((i + 1) * per, nh - 1), 0)),
                  pl.BlockSpec((1, N_MOD, d), mod_map),
                  pl.BlockSpec((1, d), lambda b, i: (0, 0)),
                  pl.BlockSpec((1, d), lambda b, i: (0, 0)),
                  resident(w_gate), resident(w_up), resident(conv_w), resident(conv_b), resident(w_down)],
        out_specs=pl.BlockSpec((1, tile, d), lambda b, i: (b, i, 0)),
        out_shape=jax.ShapeDtypeStruct((bsz, t, d), F32),
        scratch_shapes=[pltpu.VMEM((tile + 2 * halo, d), BF16), pltpu.VMEM((tile, dff), BF16),
                        pltpu.VMEM((tile + 2 * halo, tf), F32), pltpu.VMEM((tile + 2 * halo, tf), F32)],
        compiler_params=_cparams(("parallel", "parallel")),
        name="conv_ffn",
    )(x, x, x, mod, gain.reshape(1, d), final_gain.reshape(1, d), w_gate, w_up, conv_w, conv_b,
      w_down).reshape(shape)


def _ffn_weights(w_gate, w_up, conv_w, conv_b, w_down, tf):
    d, dff = w_gate.shape
    nf = dff // tf
    tiles = lambda w: w.reshape(w.shape[0], nf, tf).transpose(1, 0, 2)
    return (tiles(w_gate).astype(BF16), tiles(w_up).astype(BF16), tiles(conv_w.reshape(9, dff)),
            tiles(conv_b.reshape(1, dff)), w_down.astype(BF16))


def _conv4(v, w, b):
    t = v.shape[0]
    pos = lax.broadcasted_iota(jnp.int32, v.shape, 0)
    acc = b + w[1:2] * v
    acc = acc + w[0:1] * jnp.where(pos < 1, 0.0, pltpu.roll(v, 1, axis=0))
    acc = acc + w[2:3] * jnp.where(pos >= t - 1, 0.0, pltpu.roll(v, t - 1, axis=0))
    acc = acc + w[3:4] * jnp.where(pos >= t - 2, 0.0, pltpu.roll(v, t - 2, axis=0))
    return acc


def _ssd_kernel(xr_ref, br_ref, cr_ref, dt_ref, z_ref, cwx_ref, cwb_ref, cwc_ref, cbx_ref, cbb_ref, cbc_ref,
                dtb_ref, alog_ref, dvec_ref, h0_ref, y_ref, ht_ref, xs_ref, bs_ref, cs_ref, s_ref, *, t):
    nblk = SSD_GBLK
    nchunk = t // SSD_CHUNK
    for k in range(nblk):
        xs_ref[k] = _silu(_conv4(xr_ref[0, k], cwx_ref[k], cbx_ref[k]))
    bs_ref[...] = _silu(_conv4(br_ref[0, 0], cwb_ref[0], cbb_ref[0]))
    cs_ref[...] = _silu(_conv4(cr_ref[0, 0], cwc_ref[0], cbc_ref[0]))

    dtb = dtb_ref[0]
    a_neg = -jnp.exp(alog_ref[0])
    li = lax.broadcasted_iota(jnp.int32, (SSD_CHUNK, SSD_CHUNK), 0)
    si = lax.broadcasted_iota(jnp.int32, (SSD_CHUNK, SSD_CHUNK), 1)
    lane = lax.broadcasted_iota(jnp.int32, (1, LANES), 1)
    lo_half = lane < SSD_HEAD_DIM

    valid = [li >= si, li <= si]
    tri = [v.astype(F32).astype(BF16) for v in valid]
    for d in (0, 1):
        for k in range(nblk):
            s_ref[d, k] = h0_ref[0, d, 0, k]

    def pair(i, carry, finish):
        jobs = [(0, pl.ds(pl.multiple_of(i * SSD_CHUNK, SSD_CHUNK), SSD_CHUNK)),
                (1, pl.ds(pl.multiple_of((nchunk - 1 - i) * SSD_CHUNK, SSD_CHUNK), SSD_CHUNK))]
        pre = []
        for d, rows in jobs:
            bm = bs_ref[rows, :]
            cm = cs_ref[rows, :]
            dt = _softplus(dt_ref[0, 0, rows, :] + dtb)
            cum = _dot_select(tri[d], dt * a_neg)
            cb = lax.dot_general(cm.astype(BF16), bm.astype(BF16), (((1,), (1,)), ((), ())),
                                 preferred_element_type=F32)
            pre.append((dt, cum, cum.T, dt.T, cb, bm.T.astype(BF16), cm.astype(BF16)))
        for k in range(nblk):
            for (d, rows), (dt, cum, cum_t, dt_t, cb, bm_t, cm16) in zip(jobs, pre):
                end_row = SSD_CHUNK - 1 if d == 0 else 0
                xk = xs_ref[k, rows, :]
                ms = []
                for e in range(2):
                    idx = d * SSD_HPG + 2 * k + e
                    seg = jnp.where(valid[d], cum[:, idx:idx + 1] - cum_t[idx:idx + 1, :], -jnp.inf)
                    ms.append((cb * jnp.exp(seg) * dt_t[idx:idx + 1, :]).astype(BF16))
                x_lo = jnp.where(lo_half, xk, 0.0).astype(BF16)
                x_hi = jnp.where(lo_half, 0.0, xk).astype(BF16)
                yk = (jnp.dot(ms[0], x_lo, preferred_element_type=F32)
                      + jnp.dot(ms[1], x_hi, preferred_element_type=F32))
                i0 = d * SSD_HPG + 2 * k
                ecol = jnp.where(lo_half, cum[:, i0:i0 + 1], cum[:, i0 + 1:i0 + 2])
                dcol = jnp.where(lo_half, dt[:, i0:i0 + 1], dt[:, i0 + 1:i0 + 2])
                tot = jnp.where(lo_half, cum[end_row:end_row + 1, i0:i0 + 1],
                                cum[end_row:end_row + 1, i0 + 1:i0 + 2])
                sk = s_ref[d, k]
                yk = yk + jnp.dot(cm16, sk.astype(BF16), preferred_element_type=F32) * jnp.exp(ecol)
                xw = (xk * dcol * jnp.exp(tot - ecol)).astype(BF16)
                s_ref[d, k] = sk * jnp.exp(tot) + jnp.dot(bm_t, xw, preferred_element_type=F32)
                if finish:
                    ytot = y_ref[0, k, rows, :] + yk + dvec_ref[k] * xk
                    y_ref[0, k, rows, :] = ytot * _silu(z_ref[0, k, rows, :])
                else:
                    y_ref[0, k, rows, :] = yk
        return carry

    lax.fori_loop(0, nchunk // 2, functools.partial(pair, finish=False), 0)
    lax.fori_loop(nchunk // 2, nchunk, functools.partial(pair, finish=True), 0)
    for d in (0, 1):
        for k in range(nblk):
            ht_ref[0, d, 0, k] = s_ref[d, k]


def _ssd(p, h0, conv_w, conv_b, dtb, alog, dvec, *, z_blk0, xbc_blk0, dt_blk0):
    bsz, _, t, _ = p.shape
    assert (t // SSD_CHUNK) % 2 == 0
    nb = SSD_GBLK
    gx = lambda off: (lambda b, g: (b, off // nb + g, 0, 0))
    g1 = lambda off: (lambda b, g: (b, off + g, 0, 0))
    one = lambda off: pl.BlockSpec((1, 1, t, LANES), g1(off))
    st_spec = pl.BlockSpec((1, 2, 1, nb, SSD_STATE, LANES), lambda b, g: (b, 0, g, 0, 0, 0))
    return pl.pallas_call(
        functools.partial(_ssd_kernel, t=t),
        grid=(bsz, SSD_GROUPS),
        in_specs=[pl.BlockSpec((1, nb, t, LANES), gx(xbc_blk0)),
                  one(xbc_blk0 + SSD_B0), one(xbc_blk0 + SSD_C0), one(dt_blk0),
                  pl.BlockSpec((1, nb, t, LANES), gx(z_blk0)),
                  pl.BlockSpec((nb, 4, LANES), lambda b, g: (g, 0, 0)),
                  pl.BlockSpec((1, 4, LANES), lambda b, g: (SSD_B0 + g, 0, 0)),
                  pl.BlockSpec((1, 4, LANES), lambda b, g: (SSD_C0 + g, 0, 0)),
                  pl.BlockSpec((nb, 1, LANES), lambda b, g: (g, 0, 0)),
                  pl.BlockSpec((1, 1, LANES), lambda b, g: (SSD_B0 + g, 0, 0)),
                  pl.BlockSpec((1, 1, LANES), lambda b, g: (SSD_C0 + g, 0, 0)),
                  pl.BlockSpec((1, 1, LANES), lambda b, g: (g, 0, 0)),
                  pl.BlockSpec((1, 1, LANES), lambda b, g: (g, 0, 0)),
                  pl.BlockSpec((nb, 1, LANES), lambda b, g: (g, 0, 0)),
                  st_spec],
        out_specs=[pl.BlockSpec((1, nb, t, LANES), lambda b, g: (b, g, 0, 0)), st_spec],
        out_shape=[jax.ShapeDtypeStruct((bsz, SSD_W_BLKS, t, LANES), F32),
                   jax.ShapeDtypeStruct(h0.shape, F32)],
        scratch_shapes=[pltpu.VMEM((nb, t, LANES), F32), pltpu.VMEM((t, LANES), F32),
                        pltpu.VMEM((t, LANES), F32), pltpu.VMEM((2, nb, SSD_STATE, LANES), F32)],
        compiler_params=_cparams(("parallel", "parallel")),
        name="ssd",
    )(p, p, p, p, p, conv_w, conv_w, conv_w, conv_b, conv_b, conv_b, dtb, alog, dvec, h0)


LRU_NB = 2
N_SEG = 8
SCAN_UNROLL = 8


def _seg_rows(t):
    seg = t // N_SEG
    return seg, seg + SUBLANES


def _seg_scan(a_ref, b_ref, acc_ref, h_ref, lead, seg, pitch, reverse):
    def step(i, carry):
        out = []
        for (h, acc), ld, rev in zip(carry, lead, reverse):
            j = (seg - 1 - i) if rev else i
            idx = ld + (pl.ds(j, N_SEG, stride=pitch), slice(None))
            a = a_ref[idx]
            h = a * h + b_ref[idx]
            acc = acc * a
            acc_ref[idx] = acc
            h_ref[idx] = h
            out.append((h, acc))
        return tuple(out)

    init = (jnp.zeros((N_SEG, LANES), F32), jnp.ones((N_SEG, LANES), F32))
    return lax.fori_loop(0, seg, step, tuple(init for _ in lead), unroll=SCAN_UNROLL)


def _seg_inputs(h0, end, tot, reverse):
    rows = [None] * N_SEG
    hin = h0
    for s in (range(N_SEG - 1, -1, -1) if reverse else range(N_SEG)):
        rows[s] = hin
        hin = tot[s:s + 1] * hin + end[s:s + 1]
    return rows, hin


def _lru_kernel(u_ref, gy_ref, cw_ref, cb_ref, wa_ref, wi_ref, ba_ref, bi_ref, lam_ref, h0_ref,
                r_ref, ht_ref, uc_ref, a_ref, b_ref, acc_ref, hl_ref, *, t):
    seg, pitch = _seg_rows(t)
    for k in range(LRU_NB):
        uc_ref[k] = _conv4(u_ref[0, k], cw_ref[k], cb_ref[k])
    for d in (1, 0):
        for k in range(LRU_NB):
            c = (-0.5 * LRU_C * math.log2(math.e)) * _softplus(-lam_ref[d, k])
            for s in range(N_SEG):
                u = uc_ref[k, s * seg:(s + 1) * seg, :]
                u16 = u.astype(BF16)
                ta = jnp.tanh(jnp.dot(u16, wa_ref[d, k], preferred_element_type=F32) + ba_ref[d, k])
                ti = jnp.tanh(jnp.dot(u16, wi_ref[d, k], preferred_element_type=F32) + bi_ref[d, k])
                a = jnp.exp2(c * ta + c)
                a_ref[d, k, s * pitch:s * pitch + seg, :] = a
                half_u = 0.5 * u
                gated_u = half_u * ti + half_u
                y = (1.0 - a) * (1.0 + a)
                root = jnp.where(y > 0.0, y * lax.rsqrt(y), 0.0)
                b_ref[d, k, s * pitch:s * pitch + seg, :] = root * gated_u
    chains = [(d, k) for d in (1, 0) for k in range(LRU_NB)]
    scanned = _seg_scan(a_ref, b_ref, acc_ref, hl_ref, chains, seg, pitch, [d == 1 for d, _ in chains])
    for (d, k), (end, tot) in zip(chains, scanned):
        rows, hfin = _seg_inputs(h0_ref[0, d, k], end, tot, reverse=(d == 1))
        ht_ref[0, d, k] = hfin
        for s in range(N_SEG):
            src = slice(s * pitch, s * pitch + seg)
            dst = slice(s * seg, (s + 1) * seg)
            h = hl_ref[d, k, src, :] + acc_ref[d, k, src, :] * rows[s]
            if d == 1:
                r_ref[0, k, dst, :] = h
            else:
                r_ref[0, k, dst, :] = (r_ref[0, k, dst, :] + h) * _gelu_tanh(gy_ref[0, k, dst, :])


def _lru(p, h0, conv_w, conv_b, wa, wi, ba, bi, lam, *, gy_blk0, u_blk0):
    bsz, _, t, _ = p.shape
    nb = LRU_NB
    seg, pitch = _seg_rows(t)
    blk = lambda off: pl.BlockSpec((1, nb, t, LANES), lambda b, g: (b, off // nb + g, 0, 0))
    par = lambda shape: pl.BlockSpec(shape, lambda b, g: (0, g) + (0,) * (len(shape) - 2))
    st_spec = pl.BlockSpec((1, 2, nb, 1, LANES), lambda b, g: (b, 0, g, 0, 0))
    return pl.pallas_call(
        functools.partial(_lru_kernel, t=t),
        grid=(bsz, LRU_BLOCKS // nb),
        in_specs=[blk(u_blk0), blk(gy_blk0),
                  pl.BlockSpec((nb, 4, LANES), lambda b, g: (g, 0, 0)),
                  pl.BlockSpec((nb, 1, LANES), lambda b, g: (g, 0, 0)),
                  par((2, nb, LANES, LANES)), par((2, nb, LANES, LANES)),
                  par((2, nb, 1, LANES)), par((2, nb, 1, LANES)), par((2, nb, 1, LANES)),
                  st_spec],
        out_specs=[pl.BlockSpec((1, nb, t, LANES), lambda b, g: (b, g, 0, 0)), st_spec],
        out_shape=[jax.ShapeDtypeStruct((bsz, LRU_BLOCKS, t, LANES), F32),
                   jax.ShapeDtypeStruct(h0.shape, F32)],
        scratch_shapes=[pltpu.VMEM((nb, t, LANES), F32)] + [pltpu.VMEM((2, nb, N_SEG * pitch, LANES), F32)] * 4,
        compiler_params=_cparams(("parallel", "parallel")),
        name="rglru",
    )(p, p, conv_w, conv_b, wa, wi, ba, bi, lam, h0)


LRU_W = LRU_BLOCKS * LANES
EV_Z0 = 0
EV_XBC0 = EV_Z0 + SSD_W_BLKS
EV_DT0 = EV_XBC0 + SSD_XBC_BLKS
EV_GY0 = EV_DT0 + SSD_GROUPS
EV_U0 = EV_GY0 + LRU_BLOCKS


def _blocks(v, n):
    return v.reshape(n, 1, LANES)


def _per_group_heads(v):
    v = v.reshape(2, SSD_GROUPS, SSD_HPG).transpose(1, 0, 2).reshape(SSD_GROUPS, 2 * SSD_HPG)
    return jnp.pad(v, ((0, 0), (0, LANES - 2 * SSD_HPG))).reshape(SSD_GROUPS, 1, LANES)


def _even_weights(w_in, conv_w, conv_b, dt_bias, a_log, ssd_d, lru_conv_w, lru_conv_b, w_a, b_a, w_i, b_i, lam):
    d = w_in.shape[0]
    splits = np.cumsum([SSD_W, SSD_XBC, 2 * SSD_HEADS, LRU_W])
    z, xbc, dt, gy, u = jnp.split(w_in, [int(v) for v in splits], axis=1)
    dt = dt.reshape(d, 2, SSD_GROUPS, SSD_HPG).transpose(0, 2, 1, 3).reshape(d, SSD_GROUPS, 2 * SSD_HPG)
    dt = jnp.pad(dt, ((0, 0), (0, 0), (0, LANES - 2 * SSD_HPG))).reshape(d, SSD_GROUPS * LANES)
    return dict(
        w_in=jnp.concatenate([z, xbc, dt, gy, u], axis=1).astype(BF16),
        conv_w=conv_w.reshape(4, SSD_XBC_BLKS, LANES).transpose(1, 0, 2), conv_b=_blocks(conv_b, SSD_XBC_BLKS),
        dtb=_per_group_heads(dt_bias), alog=_per_group_heads(a_log),
        dvec=_blocks(jnp.repeat(ssd_d, SSD_HEAD_DIM), SSD_W_BLKS),
        lru_conv_w=lru_conv_w.reshape(4, LRU_BLOCKS, LANES).transpose(1, 0, 2),
        lru_conv_b=_blocks(lru_conv_b, LRU_BLOCKS),
        w_a=(0.5 * w_a).astype(BF16), w_i=(0.5 * w_i).astype(BF16),
        b_a=0.5 * b_a.reshape(2, LRU_BLOCKS, 1, LANES), b_i=0.5 * b_i.reshape(2, LRU_BLOCKS, 1, LANES),
        lam=lam.reshape(2, LRU_BLOCKS, 1, LANES))


def _even_scans(p, states, w, need_out=True):
    del need_out
    ssd_h, lru_h = states
    y, ssd_h = _ssd(p, ssd_h, w["conv_w"], w["conv_b"], w["dtb"], w["alog"], w["dvec"],
                    z_blk0=EV_Z0, xbc_blk0=EV_XBC0, dt_blk0=EV_DT0)
    r, lru_h = _lru(p, lru_h, w["lru_conv_w"], w["lru_conv_b"], w["w_a"], w["w_i"], w["b_a"], w["b_i"],
                    w["lam"], gy_blk0=EV_GY0, u_blk0=EV_U0)
    return y, r, (ssd_h, lru_h)


def _even_zero_states(bsz):
    return (jnp.zeros((bsz, 2, SSD_GROUPS, SSD_GBLK, SSD_STATE, LANES), F32),
            jnp.zeros((bsz, 2, LRU_BLOCKS, 1, LANES), F32))


HG_CHUNK = 128
HG_PAR = 3


def _group_boundary(cum, c, reverse):
    m = c // 2
    off = m if reverse else m - 1
    n = cum.shape[0]
    if c >= 2 * SUBLANES:
        r = cum.reshape(n // c, c, LANES)
        return jnp.broadcast_to(r[:, off:off + 1, :], r.shape).reshape(n, LANES)
    r = cum.reshape(n // SUBLANES, SUBLANES, LANES)
    sub = lax.broadcasted_iota(jnp.int32, r.shape, 1)
    p = None
    for g0 in range(0, SUBLANES, c):
        cand = jnp.broadcast_to(r[:, g0 + off:g0 + off + 1, :], r.shape)
        p = cand if p is None else jnp.where(sub >= g0, cand, p)
    return p.reshape(n, LANES)


def _hgrn_masks():
    l = np.arange(HG_CHUNK)[:, None]
    s = np.arange(HG_CHUNK)[None, :]
    fwd = []
    size = HG_CHUNK
    while size >= 2:
        half = size // 2
        fwd.append((l // size == s // size) & (l % size >= half) & (s % size < half))
        size = half
    fwd = np.stack(fwd).astype(np.float32)
    pairs = np.stack([fwd, fwd.transpose(0, 2, 1)])
    tri = np.stack([l >= s, l <= s]).astype(np.float32)
    sign = np.stack([np.where(pairs[d].any(axis=2, keepdims=True), 1.0, -1.0) for d in range(2)])
    sign = np.broadcast_to(sign, pairs.shape).astype(np.float32)
    return jnp.asarray(pairs), jnp.asarray(tri, dtype=BF16), jnp.asarray(sign)


def _hgrn_chunks(jobs, q_ref, ff_ref, fb_ref, v_ref, lb_ref, sf_ref, sb_ref, pairs_ref, tri_ref, sign_ref,
                 need_out):
    n = len(jobs)
    qq, kk, vv, cum = [], [], [], []
    for hd, reverse, rows in jobs:
        lb = lb_ref[hd]
        fx = (fb_ref if reverse else ff_ref)[0, hd, rows, :]
        e = jnp.exp(-jnp.abs(fx))
        big = 1.0 / (1.0 + e)
        small = e * big
        pos = fx >= 0.0
        log2_f = jnp.log2(lb + (1.0 - lb) * jnp.where(pos, big, small))
        kk.append((1.0 - lb) * jnp.where(pos, small, big))
        cum.append(_dot_select(tri_ref[1 if reverse else 0], log2_f))
    for hd, reverse, rows in jobs:
        qq.append(_silu(q_ref[0, hd, rows, :]) if need_out else None)
        vv.append(v_ref[0, hd, rows, :])
    att = [None] * n
    q16 = [v.astype(BF16) for v in qq] if need_out else None
    k16 = [v.astype(BF16) for v in kk]
    size = HG_CHUNK
    level = 0
    while need_out and size >= 2:
        for j, (hd, reverse, rows) in enumerate(jobs):
            dist = (cum[j] - _group_boundary(cum[j], size, reverse)) * sign_ref[1 if reverse else 0, level]
            fac = jnp.exp2(dist).astype(BF16)
            a_l = lax.dot_general(q16[j] * fac, k16[j] * fac, (((1,), (1,)), ((), ())),
                                  preferred_element_type=F32)
            a_l = a_l * pairs_ref[1 if reverse else 0, level]
            att[j] = a_l if att[j] is None else att[j] + a_l
        size //= 2
        level += 1
    outs = []
    for j, (hd, reverse, rows) in enumerate(jobs):
        s_ref = (sb_ref if reverse else sf_ref).at[hd]
        sv = s_ref[...]
        o = None
        if need_out:
            diag = jnp.sum(qq[j] * kk[j], axis=-1, keepdims=True)
            o = jnp.dot(att[j].astype(BF16), vv[j].astype(BF16), preferred_element_type=F32) + diag * vv[j]
            o = o + jnp.dot((qq[j] * jnp.exp2(cum[j])).astype(BF16), sv.astype(BF16),
                            preferred_element_type=F32)
        end_row = 0 if reverse else HG_CHUNK - 1
        cum_end = cum[j][end_row:end_row + 1, :]
        kw = (kk[j] * jnp.exp2(cum_end - cum[j])).T.astype(BF16)
        keep = jnp.broadcast_to(jnp.exp2(cum_end), (HG_CHUNK, LANES)).T
        s_ref[...] = sv * keep + jnp.dot(kw, vv[j].astype(BF16), preferred_element_type=F32)
        outs.append(o)
    return outs


def _hgrn_kernel(q_ref, ff_ref, fb_ref, v_ref, g_ref, lb_ref, ng_ref, pairs_ref, tri_ref, sign_ref, h0_ref,
                 *rest, t, need_out):
    if need_out:
        o_ref, ht_ref, sf_ref, sb_ref = rest
    else:
        ht_ref, sf_ref, sb_ref = rest
    nchunk = t // HG_CHUNK
    for hd in range(HG_PAR):
        sf_ref[hd] = h0_ref[0, 0, hd]
        sb_ref[hd] = h0_ref[0, 1, hd]

    def pair(i, carry, second):
        jobs = []
        for hd in range(HG_PAR):
            for reverse in (False, True):
                c = (nchunk - 1 - i) if reverse else i
                jobs.append((hd, reverse, pl.ds(pl.multiple_of(c * HG_CHUNK, HG_CHUNK), HG_CHUNK)))
        outs = _hgrn_chunks(jobs, q_ref, ff_ref, fb_ref, v_ref, lb_ref, sf_ref, sb_ref, pairs_ref, tri_ref,
                            sign_ref, need_out)
        if need_out:
            for (hd, reverse, rows), o in zip(jobs, outs):
                o_ref[0, hd, rows, :] = (o_ref[0, hd, rows, :] + o) if second else o
        return carry

    lax.fori_loop(0, nchunk // 2, functools.partial(pair, second=False), 0)
    lax.fori_loop(nchunk // 2, nchunk, functools.partial(pair, second=True), 0)
    if need_out:
        def finish(c, carry):
            rows = pl.ds(pl.multiple_of(c * HG_CHUNK, HG_CHUNK), HG_CHUNK)
            for hd in range(HG_PAR):
                o_ref[0, hd, rows, :] = _rms(o_ref[0, hd, rows, :], ng_ref[hd]) * _silu(g_ref[0, hd, rows, :])
            return carry

        lax.fori_loop(0, nchunk, finish, 0)
    for hd in range(HG_PAR):
        ht_ref[0, 0, hd] = sf_ref[hd]
        ht_ref[0, 1, hd] = sb_ref[hd]


def _hgrn(p, h0, lb, norm_g, need_out=True):
    bsz, _, t, _ = p.shape
    assert (t // HG_CHUNK) % 2 == 0
    hp = HG_PAR
    blk = lambda off: pl.BlockSpec((1, hp, t, LANES), lambda b, h: (b, off // hp + h, 0, 0))
    par = pl.BlockSpec((hp, 1, LANES), lambda b, h: (h, 0, 0))
    st_spec = pl.BlockSpec((1, 2, hp, LANES, LANES), lambda b, h: (b, 0, h, 0, 0))
    pairs, tri, sign = _hgrn_masks()
    o_spec = [pl.BlockSpec((1, hp, t, LANES), lambda b, h: (b, h, 0, 0))] if need_out else []
    o_shape = [jax.ShapeDtypeStruct((bsz, HG_HEADS, t, LANES), F32)] if need_out else []
    outs = pl.pallas_call(
        functools.partial(_hgrn_kernel, t=t, need_out=need_out),
        grid=(bsz, HG_HEADS // hp),
        in_specs=[blk(0), blk(HG_HEADS), blk(2 * HG_HEADS), blk(3 * HG_HEADS), blk(4 * HG_HEADS),
                  par, par,
                  pl.BlockSpec(pairs.shape, lambda b, h: (0, 0, 0, 0)),
                  pl.BlockSpec(tri.shape, lambda b, h: (0, 0, 0)),
                  pl.BlockSpec(sign.shape, lambda b, h: (0, 0, 0, 0)),
                  st_spec],
        out_specs=o_spec + [st_spec],
        out_shape=o_shape + [jax.ShapeDtypeStruct(h0.shape, F32)],
        scratch_shapes=[pltpu.VMEM((hp, LANES, LANES), F32), pltpu.VMEM((hp, LANES, LANES), F32)],
        compiler_params=_cparams(("parallel", "parallel")),
        name="hgrn2",
    )(p, p, p, p, p, lb.reshape(HG_HEADS, 1, LANES), norm_g.reshape(HG_HEADS, 1, LANES), pairs, tri, sign,
      h0)
    return tuple(outs) if need_out else (None, outs[0])


S5_W = S5_GROUPS * S5_GROUP_CH
S5_NSTATE = S5_GROUPS * S5_STATE
S5_NB = S5_NSTATE // LANES
S5_PAR = 2


def _s5_kernel(u_ref, are_ref, aim_ref, pw_ref, bd_ref, cd_ref, dvec_ref, gw_ref, gb_ref, h0_ref,
               y_ref, ht_ref, wr_ref, wi_ref, hr_ref, hi_ref, acc_ref, *, t):
    seg, pitch = _seg_rows(t)
    u = jnp.concatenate([u_ref[0, 0], u_ref[0, 1]], axis=-1)
    u16 = u.astype(BF16)
    acc_ref[...] = dvec_ref[...] * u
    chains = [(d, k) for d in (0, 1) for k in range(S5_PAR)]
    for j0 in range(0, S5_NB, S5_PAR):
        cols = slice(j0 * LANES, (j0 + S5_PAR) * LANES)
        group = max(1, LANES // seg)
        for s0 in range(0, N_SEG, group):
            us = u16[s0 * seg:(s0 + group) * seg]
            wr = jnp.dot(us, bd_ref[0, :, cols], preferred_element_type=F32)
            wi = jnp.dot(us, bd_ref[1, :, cols], preferred_element_type=F32)
            for s in range(s0, s0 + group):
                part = slice((s - s0) * seg, (s - s0 + 1) * seg)
                for k in range(S5_PAR):
                    wr_ref[k, s * pitch:s * pitch + seg, :] = wr[part, k * LANES:(k + 1) * LANES]
                    wi_ref[k, s * pitch:s * pitch + seg, :] = wi[part, k * LANES:(k + 1) * LANES]
        ar = [jnp.broadcast_to(are_ref[d, j0 + k], (N_SEG, LANES)) for d, k in chains]
        ai = [jnp.broadcast_to(aim_ref[d, j0 + k], (N_SEG, LANES)) for d, k in chains]

        def step(i, carry, ar=ar, ai=ai):
            out = []
            for c, (d, k) in enumerate(chains):
                gr, gi = carry[c]
                j = (seg - 1 - i) if d == 1 else i
                src = (k, pl.ds(j, N_SEG, stride=pitch), slice(None))
                dst = (d, k, pl.ds(j, N_SEG, stride=pitch), slice(None))
                ngr = ar[c] * gr - ai[c] * gi + wr_ref[src]
                ngi = ar[c] * gi + ai[c] * gr + wi_ref[src]
                hr_ref[dst] = ngr
                hi_ref[dst] = ngi
                out.append((ngr, ngi))
            return tuple(out)

        zero = jnp.zeros((N_SEG, LANES), F32)
        fin = lax.fori_loop(0, seg, step, tuple((zero, zero) for _ in chains), unroll=SCAN_UNROLL)
        for d in (0, 1):
            ins = []
            last = 0 if d == 1 else seg - 1
            for k in range(S5_PAR):
                end_r, end_i = fin[d * S5_PAR + k]
                tot_r = pw_ref[d, 0, j0 + k, last:last + 1, :]
                tot_i = pw_ref[d, 1, j0 + k, last:last + 1, :]
                hin_r, hin_i = h0_ref[0, d, 0, j0 + k], h0_ref[0, d, 1, j0 + k]
                rows = [None] * N_SEG
                for s in (range(N_SEG - 1, -1, -1) if d == 1 else range(N_SEG)):
                    rows[s] = (hin_r, hin_i)
                    hin_r, hin_i = (tot_r[0:1] * hin_r - tot_i[0:1] * hin_i + end_r[s:s + 1],
                                    tot_r[0:1] * hin_i + tot_i[0:1] * hin_r + end_i[s:s + 1])
                ht_ref[0, d, 0, j0 + k] = hin_r
                ht_ref[0, d, 1, j0 + k] = hin_i
                ins.append(rows)
            gr_rows, gi_rows = [], []
            for s in range(N_SEG):
                src = slice(s * pitch, s * pitch + seg)
                gr_parts, gi_parts = [], []
                for k in range(S5_PAR):
                    in_r, in_i = ins[k][s]
                    pr, pi = pw_ref[d, 0, j0 + k], pw_ref[d, 1, j0 + k]
                    gr_parts.append((hr_ref[d, k, src, :] + pr * in_r - pi * in_i).astype(BF16))
                    gi_parts.append((hi_ref[d, k, src, :] + pr * in_i + pi * in_r).astype(BF16))
                gr_rows.append(jnp.concatenate(gr_parts, axis=-1))
                gi_rows.append(jnp.concatenate(gi_parts, axis=-1))
                if len(gr_rows) == group:
                    rows = slice((s + 1 - group) * seg, (s + 1) * seg)
                    acc_ref[rows, :] += (
                        jnp.dot(jnp.concatenate(gr_rows, axis=0), cd_ref[d, 0, cols, :],
                                preferred_element_type=F32)
                        + jnp.dot(jnp.concatenate(gi_rows, axis=0), cd_ref[d, 1, cols, :],
                                  preferred_element_type=F32))
                    gr_rows, gi_rows = [], []
    y = _gelu_tanh(acc_ref[...])
    y = y * _sigmoid(jnp.dot(y.astype(BF16), gw_ref[...], preferred_element_type=F32) + gb_ref[...])
    y_ref[0, 0] = y[:, :LANES]
    y_ref[0, 1] = y[:, LANES:]


def _s5(p, h0, log_a, bd, cd, dvec, glu_w, glu_b, *, u_blk0):
    bsz, _, t, _ = p.shape
    seg, pitch = _seg_rows(t)
    la_re, la_im = log_a
    pos = jnp.arange(seg, dtype=F32)
    n = jnp.stack([pos + 1.0, seg - pos]).reshape(2, 1, seg, 1)
    mag = jnp.exp(n * la_re)
    pw = jnp.stack([mag * jnp.cos(n * la_im), mag * jnp.sin(n * la_im)], axis=1)
    a_re, a_im = pw[0, 0, :, 0:1, :], pw[0, 1, :, 0:1, :]
    a_re = jnp.stack([a_re, pw[1, 0, :, seg - 1:seg, :]])
    a_im = jnp.stack([a_im, pw[1, 1, :, seg - 1:seg, :]])
    full = lambda a: pl.BlockSpec(a.shape, lambda b: (0,) * a.ndim)
    st_spec = pl.BlockSpec((1, 2, 2, S5_NB, 1, LANES), lambda b: (b, 0, 0, 0, 0, 0))
    return pl.pallas_call(
        functools.partial(_s5_kernel, t=t),
        grid=(bsz,),
        in_specs=[pl.BlockSpec((1, 2, t, LANES), lambda b: (b, u_blk0 // 2, 0, 0)),
                  full(a_re), full(a_im), full(pw), full(bd), full(cd), full(dvec), full(glu_w), full(glu_b),
                  st_spec],
        out_specs=[pl.BlockSpec((1, 2, t, LANES), lambda b: (b, 0, 0, 0)), st_spec],
        out_shape=[jax.ShapeDtypeStruct((bsz, 2, t, LANES), F32), jax.ShapeDtypeStruct(h0.shape, F32)],
        scratch_shapes=[pltpu.VMEM((S5_PAR, N_SEG * pitch, LANES), F32),
                        pltpu.VMEM((S5_PAR, N_SEG * pitch, LANES), F32),
                        pltpu.VMEM((2, S5_PAR, N_SEG * pitch, LANES), F32),
                        pltpu.VMEM((2, S5_PAR, N_SEG * pitch, LANES), F32),
                        pltpu.VMEM((t, S5_W), F32)],
        compiler_params=_cparams(("parallel",)),
        name="s5",
    )(p, a_re, a_im, pw, bd, cd, dvec, glu_w, glu_b, h0)


def _s5_params(lam_re, lam_im, log_step, b_re, b_im, c_re, c_im):
    step = jnp.exp(log_step)[..., None]
    mag = jnp.exp(lam_re * step)
    ar, ai = mag * jnp.cos(lam_im * step), mag * jnp.sin(lam_im * step)
    den = lam_re * lam_re + lam_im * lam_im
    zr = ((ar - 1) * lam_re + ai * lam_im) / den
    zi = (ai * lam_re - (ar - 1) * lam_im) / den
    czr = c_re * zr[:, :, None, :] - c_im * zi[:, :, None, :]
    czi = c_re * zi[:, :, None, :] + c_im * zr[:, :, None, :]
    eye = jnp.eye(S5_GROUPS, dtype=F32)

    def in_mat(m):
        return jnp.einsum("gpk,gh->gkhp", m, eye).reshape(S5_W, S5_NSTATE)

    def out_mat(m):
        return jnp.einsum("dgkp,gh->dgphk", m, eye).reshape(2, S5_NSTATE, S5_W)

    bd = jnp.stack([in_mat(b_re), in_mat(b_im)], axis=0).astype(BF16)
    cd = jnp.stack([out_mat(czr), -out_mat(czi)], axis=1).astype(BF16)
    shape = (2, S5_NB, 1, LANES)
    log_a = ((lam_re * step).reshape(shape), (lam_im * step).reshape(shape))
    return log_a, bd, cd


OD_U0 = 5 * HG_HEADS


def _odd_weights(w_in, lower_bound, hg_norm_g, lam_re, lam_im, log_step, b_re, b_im, c_re, c_im, s5_d,
                 glu_w, glu_b):
    log_a, bd, cd = _s5_params(lam_re, lam_im, log_step, b_re, b_im, c_re, c_im)
    return dict(w_in=w_in.astype(BF16), lb=lower_bound, norm_g=hg_norm_g, log_a=log_a, bd=bd, cd=cd,
                dvec=s5_d.reshape(1, S5_W), glu_w=glu_w.astype(BF16), glu_b=glu_b.reshape(1, S5_W))


def _odd_scans(p, states, w, need_out=True):
    hg_h, s5_h = states
    o, hg_h = _hgrn(p, hg_h, w["lb"], w["norm_g"], need_out)
    y, s5_h = _s5(p, s5_h, w["log_a"], w["bd"], w["cd"], w["dvec"], w["glu_w"], w["glu_b"], u_blk0=OD_U0)
    return o, y, (hg_h, s5_h)


def _odd_zero_states(bsz):
    return (jnp.zeros((bsz, 2, HG_HEADS, LANES, LANES), F32),
            jnp.zeros((bsz, 2, 2, S5_NB, 1, LANES), F32))


PROJ_TILE = 512
MIX_TILE = 1024
FFN_TILE = 1024
FFN_COLS = 256


def kernel(x, c, ctx, c_ctx, w_mod, b_mod, norm_mix_g, norm_ffn_g, final_norm_g,
           ev_w_in, ev_w_out, ssd_conv_w, ssd_conv_b, ssd_dt_bias, ssd_a_log, ssd_d, ssd_norm_g,
           lru_conv_w, lru_conv_b, lru_w_a, lru_b_a, lru_w_i, lru_b_i, lru_lam,
           od_w_in, od_w_out, hg_lb_logits, hg_norm_g,
           s5_lam_re, s5_lam_im, s5_log_step, s5_b_re, s5_b_im, s5_c_re, s5_c_im, s5_d,
           s5_glu_w, s5_glu_b,
           ffn_w_gate, ffn_w_up, ffn_conv_w, ffn_conv_b, ffn_w_down):
    bsz, _, d = x.shape
    depth = w_mod.shape[0]
    prob = jax.nn.softmax(hg_lb_logits.astype(F32), axis=0)
    lower_bounds = (jnp.cumsum(prob, axis=0) - prob[0]).astype(hg_lb_logits.dtype)

    pad = (-(bsz + 1)) % SUBLANES
    cond = jnp.concatenate([c, c_ctx[None], jnp.zeros((pad, d), c.dtype)], axis=0)
    mods = _modulation(cond, w_mod, b_mod).transpose(0, 2, 1, 3)

    for layer in range(depth):
        last = layer == depth - 1
        j = layer // 2
        mod_x = mods[layer, :bsz]
        mod_c = mods[layer, bsz:bsz + 1]
        if layer % 2 == 0:
            w = _even_weights(ev_w_in[j], ssd_conv_w[j], ssd_conv_b[j], ssd_dt_bias[j], ssd_a_log[j], ssd_d[j],
                              lru_conv_w[j], lru_conv_b[j], lru_w_a[j], lru_b_a[j], lru_w_i[j], lru_b_i[j],
                              lru_lam[j])
            scans, zero_states = _even_scans, _even_zero_states
            w_out, norm_gain, norm_first = ev_w_out[j].astype(BF16), ssd_norm_g[j], True
        else:
            w = _odd_weights(od_w_in[j], lower_bounds[layer], hg_norm_g[j], s5_lam_re[j], s5_lam_im[j],
                             s5_log_step[j], s5_b_re[j], s5_b_im[j], s5_c_re[j], s5_c_im[j], s5_d[j],
                             s5_glu_w[j], s5_glu_b[j])
            scans, zero_states = _odd_scans, _odd_zero_states
            w_out, norm_gain, norm_first = od_w_out[j].astype(BF16), hg_norm_g[j], False
        ffn_w = _ffn_weights(ffn_w_gate[layer], ffn_w_up[layer], ffn_conv_w[layer], ffn_conv_b[layer],
                             ffn_w_down[layer], FFN_COLS)

        p_c = _project(ctx, mod_c, norm_mix_g[layer], w["w_in"], PROJ_TILE)
        a_c, b_c, states = scans(p_c, zero_states(bsz), w, need_out=not last)
        p_x = _project(x, mod_x, norm_mix_g[layer], w["w_in"], PROJ_TILE)
        a_x, b_x, _ = scans(p_x, states, w)
        x = _mix_out(x, a_x, b_x, mod_x, norm_gain, w_out, norm_first, MIX_TILE)
        x = _conv_ffn(x, mod_x, norm_ffn_g[layer], final_norm_g, *ffn_w, grid_conv=True, final_norm=last,
                      tile=FFN_TILE, tf=FFN_COLS)
        if not last:
            ctx = _mix_out(ctx, a_c, b_c, mod_c, norm_gain, w_out, norm_first, MIX_TILE)
            ctx = _conv_ffn(ctx, mod_c, norm_ffn_g[layer], final_norm_g, *ffn_w, grid_conv=False,
                            final_norm=False, tile=FFN_TILE, tf=FFN_COLS)
    return x
```

```python
import functools
import math

import jax
import jax.numpy as jnp
import numpy as np
from jax import lax
from jax.experimental import pallas as pl
from jax.experimental.pallas import tpu as pltpu

LANES = 128
SUBLANES = 8
RMS_EPS = 1e-6
N_MOD = 6
GRID_W = 64
SSD_HEAD_DIM = 64
SSD_HEADS = 16
SSD_GROUPS = 2
SSD_HPG = SSD_HEADS // SSD_GROUPS
SSD_STATE = 128
SSD_CHUNK = 128
SSD_W = SSD_HEADS * SSD_HEAD_DIM
SSD_XBC = SSD_W + 2 * SSD_GROUPS * SSD_STATE
SSD_W_BLKS, SSD_XBC_BLKS = SSD_W // LANES, SSD_XBC // LANES
SSD_GBLK = SSD_W_BLKS // SSD_GROUPS
SSD_B0, SSD_C0 = SSD_W_BLKS, SSD_W_BLKS + SSD_GROUPS
LRU_BLOCKS = 8
LRU_C = 8.0
HG_HEADS = 6
S5_GROUPS = 16
S5_GROUP_CH = 16
S5_STATE = 64
VMEM_LIMIT = 56 * 1024 * 1024

BF16 = jnp.bfloat16
F32 = jnp.float32


def _cparams(sem):
    return pltpu.CompilerParams(dimension_semantics=sem, vmem_limit_bytes=VMEM_LIMIT)


def _dot(a, b):
    return jnp.dot(a.astype(BF16), b.astype(BF16), preferred_element_type=F32)


def _dot_select(sel, v):
    hi = v.astype(BF16)
    rest = v - hi.astype(F32)
    mid = rest.astype(BF16)
    lo = (rest - mid.astype(F32)).astype(BF16)
    return (jnp.dot(sel, hi, preferred_element_type=F32) + jnp.dot(sel, mid, preferred_element_type=F32)
            + jnp.dot(sel, lo, preferred_element_type=F32))


def _sigmoid(v):
    return 0.5 * jnp.tanh(0.5 * v) + 0.5


def _silu(v):
    h = 0.5 * v
    return h * jnp.tanh(h) + h


def _gelu_tanh(v):
    return 0.5 * v * (1.0 + jnp.tanh(math.sqrt(2.0 / math.pi) * (v + 0.044715 * (v * v * v))))


def _softplus(v):
    return jnp.maximum(v, 0.0) + jnp.log(1.0 + jnp.exp(-jnp.abs(v)))


def _rms(v, g):
    return v * lax.rsqrt(jnp.mean(v * v, axis=-1, keepdims=True) + RMS_EPS) * g


def _norm_mod(xv, g, shift, scale):
    return _rms(xv, g) * (1.0 + scale) + shift


def _mod_kernel(s_ref, w_ref, b_ref, o_ref):
    o_ref[0, 0] = _dot(_silu(s_ref[...]), w_ref[0]) + b_ref[0, 0]


def _modulation(s, w_mod, b_mod):
    depth, d, _ = w_mod.shape
    rows = s.shape[0]
    return pl.pallas_call(
        _mod_kernel,
        grid=(depth, N_MOD),
        in_specs=[pl.BlockSpec((rows, d), lambda l, j: (0, 0)),
                  pl.BlockSpec((1, d, d), lambda l, j: (l, 0, j)),
                  pl.BlockSpec((1, 1, 1, d), lambda l, j: (l, j, 0, 0))],
        out_specs=pl.BlockSpec((1, 1, rows, d), lambda l, j: (l, j, 0, 0)),
        out_shape=jax.ShapeDtypeStruct((depth, N_MOD, rows, d), F32),
        compiler_params=_cparams(("arbitrary", "arbitrary")),
        name="modulation",
    )(s, w_mod, b_mod.reshape(depth, N_MOD, 1, d))


def _proj_kernel(x_ref, mod_ref, g_ref, w_ref, o_ref, *, nblk):
    m = mod_ref[0]
    h = _norm_mod(x_ref[0], g_ref[...], m[0:1], m[1:2]).astype(BF16)
    group = 4
    for b0 in range(0, nblk, group):
        nb = min(group, nblk - b0)
        r = jnp.dot(h, w_ref[:, b0 * LANES:(b0 + nb) * LANES], preferred_element_type=F32)
        for k in range(nb):
            o_ref[0, b0 + k] = r[:, k * LANES:(k + 1) * LANES]


def _project(x, mod, gain, w, tile):
    bsz, t, d = x.shape
    nblk = w.shape[1] // LANES
    tile = min(tile, t)
    mod_map = (lambda b, i: (b, 0, 0)) if mod.shape[0] == bsz else (lambda b, i: (0, 0, 0))
    return pl.pallas_call(
        functools.partial(_proj_kernel, nblk=nblk),
        grid=(bsz, t // tile),
        in_specs=[pl.BlockSpec((1, tile, d), lambda b, i: (b, i, 0)),
                  pl.BlockSpec((1, N_MOD, d), mod_map),
                  pl.BlockSpec((1, d), lambda b, i: (0, 0)),
                  pl.BlockSpec((d, nblk * LANES), lambda b, i: (0, 0), pipeline_mode=pl.Buffered(1))],
        out_specs=pl.BlockSpec((1, nblk, tile, LANES), lambda b, i: (b, 0, i, 0)),
        out_shape=jax.ShapeDtypeStruct((bsz, nblk, t, LANES), F32),
        compiler_params=_cparams(("parallel", "parallel")),
        name="project",
    )(x, mod, gain.reshape(1, d), w)


def _mix_out_kernel(x_ref, ma_ref, mb_ref, mod_ref, ng_ref, w_ref, o_ref, *, norm_first):
    m = mod_ref[0]
    pa = [ma_ref[0, k] for k in range(ma_ref.shape[1])]
    pb = [mb_ref[0, k].astype(BF16) for k in range(mb_ref.shape[1])]
    if norm_first:
        pa = [_rms(jnp.concatenate(pa, axis=-1), ng_ref[...]).astype(BF16)]
    else:
        pa = [p.astype(BF16) for p in pa]
    v = jnp.concatenate(pa + pb, axis=-1)
    o_ref[0] = x_ref[0] + m[2:3] * jnp.dot(v, w_ref[...], preferred_element_type=F32)


def _mix_out(x, mix_a, mix_b, mod, norm_gain, w, norm_first, tile):
    bsz, t, d = x.shape
    tile = min(tile, t)
    mod_map = (lambda b, i: (b, 0, 0)) if mod.shape[0] == bsz else (lambda b, i: (0, 0, 0))
    ng = norm_gain.reshape(1, -1)
    mix_spec = lambda a: pl.BlockSpec((1, a.shape[1], tile, LANES), lambda b, i: (b, 0, i, 0))
    return pl.pallas_call(
        functools.partial(_mix_out_kernel, norm_first=norm_first),
        grid=(bsz, t // tile),
        in_specs=[pl.BlockSpec((1, tile, d), lambda b, i: (b, i, 0)),
                  mix_spec(mix_a), mix_spec(mix_b),
                  pl.BlockSpec((1, N_MOD, d), mod_map),
                  pl.BlockSpec(ng.shape, lambda b, i: (0, 0)),
                  pl.BlockSpec(w.shape, lambda b, i: (0, 0), pipeline_mode=pl.Buffered(1))],
        out_specs=pl.BlockSpec((1, tile, d), lambda b, i: (b, i, 0)),
        out_shape=jax.ShapeDtypeStruct((bsz, t, d), F32),
        compiler_params=_cparams(("parallel", "parallel")),
        name="mix_out",
    )(x, mix_a, mix_b, mod, ng, w)


def _ffn_kernel(x_ref, xp_ref, xn_ref, mod_ref, g_ref, fg_ref, wg_ref, wu_ref, cw_ref, cb_ref, wd_ref,
                o_ref, fx_ref, gt_ref, a0_ref, a1_ref, *, tile, halo, tf, grid_conv, period, final_norm):
    i = pl.program_id(1)
    nt = pl.num_programs(1)
    nf = wg_ref.shape[0]
    m = mod_ref[0]
    fx_ref[halo:halo + tile] = _norm_mod(x_ref[0], g_ref[...], m[3:4], m[4:5]).astype(BF16)
    if halo:
        keep_p = jnp.where(i > 0, 1.0, 0.0)
        keep_n = jnp.where(i < nt - 1, 1.0, 0.0)
        fx_ref[0:halo] = (keep_p * _norm_mod(xp_ref[0], g_ref[...], m[3:4], m[4:5])).astype(BF16)
        fx_ref[halo + tile:] = (keep_n * _norm_mod(xn_ref[0], g_ref[...], m[3:4], m[4:5])).astype(BF16)

    def gate(f, dst_ref):
        dst_ref[...] = jnp.dot(fx_ref[...], wg_ref[f], preferred_element_type=F32)

    def column(f, src_ref):
        a = src_ref[...]
        cw = cw_ref[f]
        rows = a.shape[0]
        pos = lax.broadcasted_iota(jnp.int32, a.shape, 0)
        if grid_conv:
            col = pos % GRID_W
            a_m1 = jnp.where(col == 0, 0.0, pltpu.roll(a, 1, axis=0))
            a_p1 = jnp.where(col == GRID_W - 1, 0.0, pltpu.roll(a, rows - 1, axis=0))
            conv = cb_ref[f]
            for dr in range(3):
                lo = dr * GRID_W
                conv = conv + (cw[3 * dr + 0:3 * dr + 1] * a_m1[lo:lo + tile]
                               + cw[3 * dr + 1:3 * dr + 2] * a[lo:lo + tile]
                               + cw[3 * dr + 2:3 * dr + 3] * a_p1[lo:lo + tile])
        else:
            col = pos % period
            a_m1 = jnp.where(col == 0, 0.0, pltpu.roll(a, 1, axis=0))
            a_p1 = jnp.where(col == period - 1, 0.0, pltpu.roll(a, rows - 1, axis=0))
            conv = cb_ref[f] + cw[3:4] * a_m1 + cw[4:5] * a + cw[5:6] * a_p1
        act = _silu(conv)
        up = jnp.dot(fx_ref[halo:halo + tile], wu_ref[f], preferred_element_type=F32)
        start = f * tf if isinstance(f, int) else pl.multiple_of(f * tf, tf)
        gt_ref[:, pl.ds(start, tf)] = (act * up).astype(BF16)

    def column_pair(p, carry):
        f = 2 * p
        gate(f + 1, a1_ref)
        column(f, a0_ref)
        gate(f + 2, a0_ref)
        column(f + 1, a1_ref)
        return carry

    gate(0, a0_ref)
    pairs = (nf - 1) // 2
    lax.fori_loop(0, pairs, column_pair, 0)
    if nf - 2 * pairs == 2:
        gate(nf - 1, a1_ref)
        column(nf - 2, a0_ref)
        column(nf - 1, a1_ref)
    else:
        column(nf - 1, a0_ref)
    y = x_ref[0] + m[5:6] * jnp.dot(gt_ref[...], wd_ref[...], preferred_element_type=F32)
    if final_norm:
        y = _rms(y, fg_ref[...])
    o_ref[0] = y


def _conv_ffn(x, mod, gain, final_gain, w_gate, w_up, conv_w, conv_b, w_down, grid_conv, final_norm,
              tile, tf):
    shape = x.shape
    period = x.shape[1]
    if not grid_conv and mod.shape[0] == 1 and tile % period == 0 and (x.shape[0] * period) % tile == 0:
        x = x.reshape(-1, tile, x.shape[2])
    bsz, t, d = x.shape
    nf = w_gate.shape[0]
    dff = nf * tf
    tile = min(tile, t)
    halo = GRID_W if grid_conv else 0
    nh = t // GRID_W
    per = tile // GRID_W
    mod_map = (lambda b, i: (b, 0, 0)) if mod.shape[0] == bsz else (lambda b, i: (0, 0, 0))
    resident = lambda a: pl.BlockSpec(a.shape, lambda b, i: (0,) * a.ndim, pipeline_mode=pl.Buffered(1))
    return pl.pallas_call(
        functools.partial(_ffn_kernel, tile=tile, halo=halo, tf=tf, grid_conv=grid_conv, period=period,
                          final_norm=final_norm),
        grid=(bsz, t // tile),
        in_specs=[pl.BlockSpec((1, tile, d), lambda b, i: (b, i, 0)),
                  pl.BlockSpec((1, GRID_W, d), lambda b, i: (b, jnp.maximum(i * per - 1, 0), 0)),
                  pl.BlockSpec((1, GRID_W, d), lambda b, i: (b, jnp.minimum((i + 1) * per, nh - 1), 0)),
                  pl.BlockSpec((1, N_MOD, d), mod_map),
                  pl.BlockSpec((1, d), lambda b, i: (0, 0)),
                  pl.BlockSpec((1, d), lambda b, i: (0, 0)),
                  resident(w_gate), resident(w_up), resident(conv_w), resident(conv_b), resident(w_down)],
        out_specs=pl.BlockSpec((1, tile, d), lambda b, i: (b, i, 0)),
        out_shape=jax.ShapeDtypeStruct((bsz, t, d), F32),
        scratch_shapes=[pltpu.VMEM((tile + 2 * halo, d), BF16), pltpu.VMEM((tile, dff), BF16),
                        pltpu.VMEM((tile + 2 * halo, tf), F32), pltpu.VMEM((tile + 2 * halo, tf), F32)],
        compiler_params=_cparams(("parallel", "parallel")),
        name="conv_ffn",
    )(x, x, x, mod, gain.reshape(1, d), final_gain.reshape(1, d), w_gate, w_up, conv_w, conv_b,
      w_down).reshape(shape)


def _ffn_weights(w_gate, w_up, conv_w, conv_b, w_down, tf):
    d, dff = w_gate.shape
    nf = dff // tf
    tiles = lambda w: w.reshape(w.shape[0], nf, tf).transpose(1, 0, 2)
    return (tiles(w_gate).astype(BF16), tiles(w_up).astype(BF16), tiles(conv_w.reshape(9, dff)),
            tiles(conv_b.reshape(1, dff)), w_down.astype(BF16))


def _conv4(v, w, b):
    t = v.shape[0]
    pos = lax.broadcasted_iota(jnp.int32, v.shape, 0)
    acc = b + w[1:2] * v
    acc = acc + w[0:1] * jnp.where(pos < 1, 0.0, pltpu.roll(v, 1, axis=0))
    acc = acc + w[2:3] * jnp.where(pos >= t - 1, 0.0, pltpu.roll(v, t - 1, axis=0))
    acc = acc + w[3:4] * jnp.where(pos >= t - 2, 0.0, pltpu.roll(v, t - 2, axis=0))
    return acc


def _ssd_kernel(xr_ref, br_ref, cr_ref, dt_ref, z_ref, cwx_ref, cwb_ref, cwc_ref, cbx_ref, cbb_ref, cbc_ref,
                dtb_ref, alog_ref, dvec_ref, h0_ref, y_ref, ht_ref, xs_ref, bs_ref, cs_ref, s_ref, *, t):
    nblk = SSD_GBLK
    nchunk = t // SSD_CHUNK
    for k in range(nblk):
        xs_ref[k] = _silu(_conv4(xr_ref[0, k], cwx_ref[k], cbx_ref[k]))
    bs_ref[...] = _silu(_conv4(br_ref[0, 0], cwb_ref[0], cbb_ref[0]))
    cs_ref[...] = _silu(_conv4(cr_ref[0, 0], cwc_ref[0], cbc_ref[0]))

    dtb = dtb_ref[0]
    a_neg = -jnp.exp(alog_ref[0])
    li = lax.broadcasted_iota(jnp.int32, (SSD_CHUNK, SSD_CHUNK), 0)
    si = lax.broadcasted_iota(jnp.int32, (SSD_CHUNK, SSD_CHUNK), 1)
    lane = lax.broadcasted_iota(jnp.int32, (1, LANES), 1)
    lo_half = lane < SSD_HEAD_DIM

    valid = [li >= si, li <= si]
    tri = [v.astype(F32).astype(BF16) for v in valid]
    for d in (0, 1):
        for k in range(nblk):
            s_ref[d, k] = h0_ref[0, d, 0, k]

    def pair(i, carry, finish):
        jobs = [(0, pl.ds(pl.multiple_of(i * SSD_CHUNK, SSD_CHUNK), SSD_CHUNK)),
                (1, pl.ds(pl.multiple_of((nchunk - 1 - i) * SSD_CHUNK, SSD_CHUNK), SSD_CHUNK))]
        pre = []
        for d, rows in jobs:
            bm = bs_ref[rows, :]
            cm = cs_ref[rows, :]
            dt = _softplus(dt_ref[0, 0, rows, :] + dtb)
            cum = _dot_select(tri[d], dt * a_neg)
            cb = lax.dot_general(cm.astype(BF16), bm.astype(BF16), (((1,), (1,)), ((), ())),
                                 preferred_element_type=F32)
            pre.append((dt, cum, cum.T, dt.T, cb, bm.T.astype(BF16), cm.astype(BF16)))
        for k in range(nblk):
            for (d, rows), (dt, cum, cum_t, dt_t, cb, bm_t, cm16) in zip(jobs, pre):
                end_row = SSD_CHUNK - 1 if d == 0 else 0
                xk = xs_ref[k, rows, :]
                ms = []
                for e in range(2):
                    idx = d * SSD_HPG + 2 * k + e
                    seg = jnp.where(valid[d], cum[:, idx:idx + 1] - cum_t[idx:idx + 1, :], -jnp.inf)
                    ms.append((cb * jnp.exp(seg) * dt_t[idx:idx + 1, :]).astype(BF16))
                x_lo = jnp.where(lo_half, xk, 0.0).astype(BF16)
                x_hi = jnp.where(lo_half, 0.0, xk).astype(BF16)
                yk = (jnp.dot(ms[0], x_lo, preferred_element_type=F32)
                      + jnp.dot(ms[1], x_hi, preferred_element_type=F32))
                i0 = d * SSD_HPG + 2 * k
                ecol = jnp.where(lo_half, cum[:, i0:i0 + 1], cum[:, i0 + 1:i0 + 2])
                dcol = jnp.where(lo_half, dt[:, i0:i0 + 1], dt[:, i0 + 1:i0 + 2])
                tot = jnp.where(lo_half, cum[end_row:end_row + 1, i0:i0 + 1],
                                cum[end_row:end_row + 1, i0 + 1:i0 + 2])
                sk = s_ref[d, k]
                yk = yk + jnp.dot(cm16, sk.astype(BF16), preferred_element_type=F32) * jnp.exp(ecol)
                xw = (xk * dcol * jnp.exp(tot - ecol)).astype(BF16)
                s_ref[d, k] = sk * jnp.exp(tot) + jnp.dot(bm_t, xw, preferred_element_type=F32)
                if finish:
                    ytot = y_ref[0, k, rows, :] + yk + dvec_ref[k] * xk
                    y_ref[0, k, rows, :] = ytot * _silu(z_ref[0, k, rows, :])
                else:
                    y_ref[0, k, rows, :] = yk
        return carry

    lax.fori_loop(0, nchunk // 2, functools.partial(pair, finish=False), 0)
    lax.fori_loop(nchunk // 2, nchunk, functools.partial(pair, finish=True), 0)
    for d in (0, 1):
        for k in range(nblk):
            ht_ref[0, d, 0, k] = s_ref[d, k]


def _ssd(p, h0, conv_w, conv_b, dtb, alog, dvec, *, z_blk0, xbc_blk0, dt_blk0):
    bsz, _, t, _ = p.shape
    assert (t // SSD_CHUNK) % 2 == 0
    nb = SSD_GBLK
    gx = lambda off: (lambda b, g: (b, off // nb + g, 0, 0))
    g1 = lambda off: (lambda b, g: (b, off + g, 0, 0))
    one = lambda off: pl.BlockSpec((1, 1, t, LANES), g1(off))
    st_spec = pl.BlockSpec((1, 2, 1, nb, SSD_STATE, LANES), lambda b, g: (b, 0, g, 0, 0, 0))
    return pl.pallas_call(
        functools.partial(_ssd_kernel, t=t),
        grid=(bsz, SSD_GROUPS),
        in_specs=[pl.BlockSpec((1, nb, t, LANES), gx(xbc_blk0)),
                  one(xbc_blk0 + SSD_B0), one(xbc_blk0 + SSD_C0), one(dt_blk0),
                  pl.BlockSpec((1, nb, t, LANES), gx(z_blk0)),
                  pl.BlockSpec((nb, 4, LANES), lambda b, g: (g, 0, 0)),
                  pl.BlockSpec((1, 4, LANES), lambda b, g: (SSD_B0 + g, 0, 0)),
                  pl.BlockSpec((1, 4, LANES), lambda b, g: (SSD_C0 + g, 0, 0)),
                  pl.BlockSpec((nb, 1, LANES), lambda b, g: (g, 0, 0)),
                  pl.BlockSpec((1, 1, LANES), lambda b, g: (SSD_B0 + g, 0, 0)),
                  pl.BlockSpec((1, 1, LANES), lambda b, g: (SSD_C0 + g, 0, 0)),
                  pl.BlockSpec((1, 1, LANES), lambda b, g: (g, 0, 0)),
                  pl.BlockSpec((1, 1, LANES), lambda b, g: (g, 0, 0)),
                  pl.BlockSpec((nb, 1, LANES), lambda b, g: (g, 0, 0)),
                  st_spec],
        out_specs=[pl.BlockSpec((1, nb, t, LANES), lambda b, g: (b, g, 0, 0)), st_spec],
        out_shape=[jax.ShapeDtypeStruct((bsz, SSD_W_BLKS, t, LANES), F32),
                   jax.ShapeDtypeStruct(h0.shape, F32)],
        scratch_shapes=[pltpu.VMEM((nb, t, LANES), F32), pltpu.VMEM((t, LANES), F32),
                        pltpu.VMEM((t, LANES), F32), pltpu.VMEM((2, nb, SSD_STATE, LANES), F32)],
        compiler_params=_cparams(("parallel", "parallel")),
        name="ssd",
    )(p, p, p, p, p, conv_w, conv_w, conv_w, conv_b, conv_b, conv_b, dtb, alog, dvec, h0)


LRU_NB = 2
N_SEG = 8
SCAN_UNROLL = 8


def _seg_rows(t):
    seg = t // N_SEG
    return seg, seg + SUBLANES


def _seg_scan(a_ref, b_ref, acc_ref, h_ref, lead, seg, pitch, reverse):
    def step(i, carry):
        out = []
        for (h, acc), ld, rev in zip(carry, lead, reverse):
            j = (seg - 1 - i) if rev else i
            idx = ld + (pl.ds(j, N_SEG, stride=pitch), slice(None))
            a = a_ref[idx]
            h = a * h + b_ref[idx]
            acc = acc * a
            acc_ref[idx] = acc
            h_ref[idx] = h
            out.append((h, acc))
        return tuple(out)

    init = (jnp.zeros((N_SEG, LANES), F32), jnp.ones((N_SEG, LANES), F32))
    return lax.fori_loop(0, seg, step, tuple(init for _ in lead), unroll=SCAN_UNROLL)


def _seg_inputs(h0, end, tot, reverse):
    rows = [None] * N_SEG
    hin = h0
    for s in (range(N_SEG - 1, -1, -1) if reverse else range(N_SEG)):
        rows[s] = hin
        hin = tot[s:s + 1] * hin + end[s:s + 1]
    return rows, hin


def _lru_kernel(u_ref, gy_ref, cw_ref, cb_ref, wa_ref, wi_ref, ba_ref, bi_ref, lam_ref, h0_ref,
                r_ref, ht_ref, uc_ref, a_ref, b_ref, acc_ref, hl_ref, *, t):
    seg, pitch = _seg_rows(t)
    for k in range(LRU_NB):
        uc_ref[k] = _conv4(u_ref[0, k], cw_ref[k], cb_ref[k])
    for d in (1, 0):
        for k in range(LRU_NB):
            c = (-0.5 * LRU_C * math.log2(math.e)) * _softplus(-lam_ref[d, k])
            for s in range(N_SEG):
                u = uc_ref[k, s * seg:(s + 1) * seg, :]
                u16 = u.astype(BF16)
                ta = jnp.tanh(jnp.dot(u16, wa_ref[d, k], preferred_element_type=F32) + ba_ref[d, k])
                ti = jnp.tanh(jnp.dot(u16, wi_ref[d, k], preferred_element_type=F32) + bi_ref[d, k])
                a = jnp.exp2(c * ta + c)
                a_ref[d, k, s * pitch:s * pitch + seg, :] = a
                half_u = 0.5 * u
                gated_u = half_u * ti + half_u
                y = (1.0 - a) * (1.0 + a)
                root = jnp.where(y > 0.0, y * lax.rsqrt(y), 0.0)
                b_ref[d, k, s * pitch:s * pitch + seg, :] = root * gated_u
    chains = [(d, k) for d in (1, 0) for k in range(LRU_NB)]
    scanned = _seg_scan(a_ref, b_ref, acc_ref, hl_ref, chains, seg, pitch, [d == 1 for d, _ in chains])
    for (d, k), (end, tot) in zip(chains, scanned):
        rows, hfin = _seg_inputs(h0_ref[0, d, k], end, tot, reverse=(d == 1))
        ht_ref[0, d, k] = hfin
        for s in range(N_SEG):
            src = slice(s * pitch, s * pitch + seg)
            dst = slice(s * seg, (s + 1) * seg)
            h = hl_ref[d, k, src, :] + acc_ref[d, k, src, :] * rows[s]
            if d == 1:
                r_ref[0, k, dst, :] = h
            else:
                r_ref[0, k, dst, :] = (r_ref[0, k, dst, :] + h) * _gelu_tanh(gy_ref[0, k, dst, :])


def _lru(p, h0, conv_w, conv_b, wa, wi, ba, bi, lam, *, gy_blk0, u_blk0):
    bsz, _, t, _ = p.shape
    nb = LRU_NB
    seg, pitch = _seg_rows(t)
    blk = lambda off: pl.BlockSpec((1, nb, t, LANES), lambda b, g: (b, off // nb + g, 0, 0))
    par = lambda shape: pl.BlockSpec(shape, lambda b, g: (0, g) + (0,) * (len(shape) - 2))
    st_spec = pl.BlockSpec((1, 2, nb, 1, LANES), lambda b, g: (b, 0, g, 0, 0))
    return pl.pallas_call(
        functools.partial(_lru_kernel, t=t),
        grid=(bsz, LRU_BLOCKS // nb),
        in_specs=[blk(u_blk0), blk(gy_blk0),
                  pl.BlockSpec((nb, 4, LANES), lambda b, g: (g, 0, 0)),
                  pl.BlockSpec((nb, 1, LANES), lambda b, g: (g, 0, 0)),
                  par((2, nb, LANES, LANES)), par((2, nb, LANES, LANES)),
                  par((2, nb, 1, LANES)), par((2, nb, 1, LANES)), par((2, nb, 1, LANES)),
                  st_spec],
        out_specs=[pl.BlockSpec((1, nb, t, LANES), lambda b, g: (b, g, 0, 0)), st_spec],
        out_shape=[jax.ShapeDtypeStruct((bsz, LRU_BLOCKS, t, LANES), F32),
                   jax.ShapeDtypeStruct(h0.shape, F32)],
        scratch_shapes=[pltpu.VMEM((nb, t, LANES), F32)] + [pltpu.VMEM((2, nb, N_SEG * pitch, LANES), F32)] * 4,
        compiler_params=_cparams(("parallel", "parallel")),
        name="rglru",
    )(p, p, conv_w, conv_b, wa, wi, ba, bi, lam, h0)


LRU_W = LRU_BLOCKS * LANES
EV_Z0 = 0
EV_XBC0 = EV_Z0 + SSD_W_BLKS
EV_DT0 = EV_XBC0 + SSD_XBC_BLKS
EV_GY0 = EV_DT0 + SSD_GROUPS
EV_U0 = EV_GY0 + LRU_BLOCKS


def _blocks(v, n):
    return v.reshape(n, 1, LANES)


def _per_group_heads(v):
    v = v.reshape(2, SSD_GROUPS, SSD_HPG).transpose(1, 0, 2).reshape(SSD_GROUPS, 2 * SSD_HPG)
    return jnp.pad(v, ((0, 0), (0, LANES - 2 * SSD_HPG))).reshape(SSD_GROUPS, 1, LANES)


def _even_weights(w_in, conv_w, conv_b, dt_bias, a_log, ssd_d, lru_conv_w, lru_conv_b, w_a, b_a, w_i, b_i, lam):
    d = w_in.shape[0]
    splits = np.cumsum([SSD_W, SSD_XBC, 2 * SSD_HEADS, LRU_W])
    z, xbc, dt, gy, u = jnp.split(w_in, [int(v) for v in splits], axis=1)
    dt = dt.reshape(d, 2, SSD_GROUPS, SSD_HPG).transpose(0, 2, 1, 3).reshape(d, SSD_GROUPS, 2 * SSD_HPG)
    dt = jnp.pad(dt, ((0, 0), (0, 0), (0, LANES - 2 * SSD_HPG))).reshape(d, SSD_GROUPS * LANES)
    return dict(
        w_in=jnp.concatenate([z, xbc, dt, gy, u], axis=1).astype(BF16),
        conv_w=conv_w.reshape(4, SSD_XBC_BLKS, LANES).transpose(1, 0, 2), conv_b=_blocks(conv_b, SSD_XBC_BLKS),
        dtb=_per_group_heads(dt_bias), alog=_per_group_heads(a_log),
        dvec=_blocks(jnp.repeat(ssd_d, SSD_HEAD_DIM), SSD_W_BLKS),
        lru_conv_w=lru_conv_w.reshape(4, LRU_BLOCKS, LANES).transpose(1, 0, 2),
        lru_conv_b=_blocks(lru_conv_b, LRU_BLOCKS),
        w_a=(0.5 * w_a).astype(BF16), w_i=(0.5 * w_i).astype(BF16),
        b_a=0.5 * b_a.reshape(2, LRU_BLOCKS, 1, LANES), b_i=0.5 * b_i.reshape(2, LRU_BLOCKS, 1, LANES),
        lam=lam.reshape(2, LRU_BLOCKS, 1, LANES))


def _even_scans(p, states, w, need_out=True):
    del need_out
    ssd_h, lru_h = states
    y, ssd_h = _ssd(p, ssd_h, w["conv_w"], w["conv_b"], w["dtb"], w["alog"], w["dvec"],
                    z_blk0=EV_Z0, xbc_blk0=EV_XBC0, dt_blk0=EV_DT0)
    r, lru_h = _lru(p, lru_h, w["lru_conv_w"], w["lru_conv_b"], w["w_a"], w["w_i"], w["b_a"], w["b_i"],
                    w["lam"], gy_blk0=EV_GY0, u_blk0=EV_U0)
    return y, r, (ssd_h, lru_h)


def _even_zero_states(bsz):
    return (jnp.zeros((bsz, 2, SSD_GROUPS, SSD_GBLK, SSD_STATE, LANES), F32),
            jnp.zeros((bsz, 2, LRU_BLOCKS, 1, LANES), F32))


HG_CHUNK = 128
HG_PAR = 3


def _group_boundary(cum, c, reverse):
    m = c // 2
    off = m if reverse else m - 1
    n = cum.shape[0]
    if c >= 2 * SUBLANES:
        r = cum.reshape(n // c, c, LANES)
        return jnp.broadcast_to(r[:, off:off + 1, :], r.shape).reshape(n, LANES)
    r = cum.reshape(n // SUBLANES, SUBLANES, LANES)
    sub = lax.broadcasted_iota(jnp.int32, r.shape, 1)
    p = None
    for g0 in range(0, SUBLANES, c):
        cand = jnp.broadcast_to(r[:, g0 + off:g0 + off + 1, :], r.shape)
        p = cand if p is None else jnp.where(sub >= g0, cand, p)
    return p.reshape(n, LANES)


def _hgrn_masks():
    l = np.arange(HG_CHUNK)[:, None]
    s = np.arange(HG_CHUNK)[None, :]
    fwd = []
    size = HG_CHUNK
    while size >= 2:
        half = size // 2
        fwd.append((l // size == s // size) & (l % size >= half) & (s % size < half))
        size = half
    fwd = np.stack(fwd).astype(np.float32)
    pairs = np.stack([fwd, fwd.transpose(0, 2, 1)])
    tri = np.stack([l >= s, l <= s]).astype(np.float32)
    sign = np.stack([np.where(pairs[d].any(axis=2, keepdims=True), 1.0, -1.0) for d in range(2)])
    sign = np.broadcast_to(sign, pairs.shape).astype(np.float32)
    return jnp.asarray(pairs), jnp.asarray(tri, dtype=BF16), jnp.asarray(sign)


def _hgrn_chunks(jobs, q_ref, ff_ref, fb_ref, v_ref, lb_ref, sf_ref, sb_ref, pairs_ref, tri_ref, sign_ref,
                 need_out):
    n = len(jobs)
    qq, kk, vv, cum = [], [], [], []
    for hd, reverse, rows in jobs:
        lb = lb_ref[hd]
        fx = (fb_ref if reverse else ff_ref)[0, hd, rows, :]
        e = jnp.exp(-jnp.abs(fx))
        big = 1.0 / (1.0 + e)
        small = e * big
        pos = fx >= 0.0
        log2_f = jnp.log2(lb + (1.0 - lb) * jnp.where(pos, big, small))
        kk.append((1.0 - lb) * jnp.where(pos, small, big))
        cum.append(_dot_select(tri_ref[1 if reverse else 0], log2_f))
    for hd, reverse, rows in jobs:
        qq.append(_silu(q_ref[0, hd, rows, :]) if need_out else None)
        vv.append(v_ref[0, hd, rows, :])
    att = [None] * n
    q16 = [v.astype(BF16) for v in qq] if need_out else None
    k16 = [v.astype(BF16) for v in kk]
    size = HG_CHUNK
    level = 0
    while need_out and size >= 2:
        for j, (hd, reverse, rows) in enumerate(jobs):
            dist = (cum[j] - _group_boundary(cum[j], size, reverse)) * sign_ref[1 if reverse else 0, level]
            fac = jnp.exp2(dist).astype(BF16)
            a_l = lax.dot_general(q16[j] * fac, k16[j] * fac, (((1,), (1,)), ((), ())),
                                  preferred_element_type=F32)
            a_l = a_l * pairs_ref[1 if reverse else 0, level]
            att[j] = a_l if att[j] is None else att[j] + a_l
        size //= 2
        level += 1
    outs = []
    for j, (hd, reverse, rows) in enumerate(jobs):
        s_ref = (sb_ref if reverse else sf_ref).at[hd]
        sv = s_ref[...]
        o = None
        if need_out:
            diag = jnp.sum(qq[j] * kk[j], axis=-1, keepdims=True)
            o = jnp.dot(att[j].astype(BF16), vv[j].astype(BF16), preferred_element_type=F32) + diag * vv[j]
            o = o + jnp.dot((qq[j] * jnp.exp2(cum[j])).astype(BF16), sv.astype(BF16),
                            preferred_element_type=F32)
        end_row = 0 if reverse else HG_CHUNK - 1
        cum_end = cum[j][end_row:end_row + 1, :]
        kw = (kk[j] * jnp.exp2(cum_end - cum[j])).T.astype(BF16)
        keep = jnp.broadcast_to(jnp.exp2(cum_end), (HG_CHUNK, LANES)).T
        s_ref[...] = sv * keep + jnp.dot(kw, vv[j].astype(BF16), preferred_element_type=F32)
        outs.append(o)
    return outs


def _hgrn_kernel(q_ref, ff_ref, fb_ref, v_ref, g_ref, lb_ref, ng_ref, pairs_ref, tri_ref, sign_ref, h0_ref,
                 *rest, t, need_out):
    if need_out:
        o_ref, ht_ref, sf_ref, sb_ref = rest
    else:
        ht_ref, sf_ref, sb_ref = rest
    nchunk = t // HG_CHUNK
    for hd in range(HG_PAR):
        sf_ref[hd] = h0_ref[0, 0, hd]
        sb_ref[hd] = h0_ref[0, 1, hd]

    def pair(i, carry, second):
        jobs = []
        for hd in range(HG_PAR):
            for reverse in (False, True):
                c = (nchunk - 1 - i) if reverse else i
                jobs.append((hd, reverse, pl.ds(pl.multiple_of(c * HG_CHUNK, HG_CHUNK), HG_CHUNK)))
        outs = _hgrn_chunks(jobs, q_ref, ff_ref, fb_ref, v_ref, lb_ref, sf_ref, sb_ref, pairs_ref, tri_ref,
                            sign_ref, need_out)
        if need_out:
            for (hd, reverse, rows), o in zip(jobs, outs):
                o_ref[0, hd, rows, :] = (o_ref[0, hd, rows, :] + o) if second else o
        return carry

    lax.fori_loop(0, nchunk // 2, functools.partial(pair, second=False), 0)
    lax.fori_loop(nchunk // 2, nchunk, functools.partial(pair, second=True), 0)
    if need_out:
        def finish(c, carry):
            rows = pl.ds(pl.multiple_of(c * HG_CHUNK, HG_CHUNK), HG_CHUNK)
            for hd in range(HG_PAR):
                o_ref[0, hd, rows, :] = _rms(o_ref[0, hd, rows, :], ng_ref[hd]) * _silu(g_ref[0, hd, rows, :])
            return carry

        lax.fori_loop(0, nchunk, finish, 0)
    for hd in range(HG_PAR):
        ht_ref[0, 0, hd] = sf_ref[hd]
        ht_ref[0, 1, hd] = sb_ref[hd]


def _hgrn(p, h0, lb, norm_g, need_out=True):
    bsz, _, t, _ = p.shape
    assert (t // HG_CHUNK) % 2 == 0
    hp = HG_PAR
    blk = lambda off: pl.BlockSpec((1, hp, t, LANES), lambda b, h: (b, off // hp + h, 0, 0))
    par = pl.BlockSpec((hp, 1, LANES), lambda b, h: (h, 0, 0))
    st_spec = pl.BlockSpec((1, 2, hp, LANES, LANES), lambda b, h: (b, 0, h, 0, 0))
    pairs, tri, sign = _hgrn_masks()
    o_spec = [pl.BlockSpec((1, hp, t, LANES), lambda b, h: (b, h, 0, 0))] if need_out else []
    o_shape = [jax.ShapeDtypeStruct((bsz, HG_HEADS, t, LANES), F32)] if need_out else []
    outs = pl.pallas_call(
        functools.partial(_hgrn_kernel, t=t, need_out=need_out),
        grid=(bsz, HG_HEADS // hp),
        in_specs=[blk(0), blk(HG_HEADS), blk(2 * HG_HEADS), blk(3 * HG_HEADS), blk(4 * HG_HEADS),
                  par, par,
                  pl.BlockSpec(pairs.shape, lambda b, h: (0, 0, 0, 0)),
                  pl.BlockSpec(tri.shape, lambda b, h: (0, 0, 0)),
                  pl.BlockSpec(sign.shape, lambda b, h: (0, 0, 0, 0)),
                  st_spec],
        out_specs=o_spec + [st_spec],
        out_shape=o_shape + [jax.ShapeDtypeStruct(h0.shape, F32)],
        scratch_shapes=[pltpu.VMEM((hp, LANES, LANES), F32), pltpu.VMEM((hp, LANES, LANES), F32)],
        compiler_params=_cparams(("parallel", "parallel")),
        name="hgrn2",
    )(p, p, p, p, p, lb.reshape(HG_HEADS, 1, LANES), norm_g.reshape(HG_HEADS, 1, LANES), pairs, tri, sign,
      h0)
    return tuple(outs) if need_out else (None, outs[0])


S5_W = S5_GROUPS * S5_GROUP_CH
S5_NSTATE = S5_GROUPS * S5_STATE
S5_NB = S5_NSTATE // LANES
S5_PAR = 2


def _s5_kernel(u_ref, are_ref, aim_ref, pw_ref, bd_ref, cd_ref, dvec_ref, gw_ref, gb_ref, h0_ref,
               y_ref, ht_ref, wr_ref, wi_ref, hr_ref, hi_ref, acc_ref, *, t):
    seg, pitch = _seg_rows(t)
    u = jnp.concatenate([u_ref[0, 0], u_ref[0, 1]], axis=-1)
    u16 = u.astype(BF16)
    acc_ref[...] = dvec_ref[...] * u
    chains = [(d, k) for d in (0, 1) for k in range(S5_PAR)]
    for j0 in range(0, S5_NB, S5_PAR):
        cols = slice(j0 * LANES, (j0 + S5_PAR) * LANES)
        group = max(1, LANES // seg)
        for s0 in range(0, N_SEG, group):
            us = u16[s0 * seg:(s0 + group) * seg]
            wr = jnp.dot(us, bd_ref[0, :, cols], preferred_element_type=F32)
            wi = jnp.dot(us, bd_ref[1, :, cols], preferred_element_type=F32)
            for s in range(s0, s0 + group):
                part = slice((s - s0) * seg, (s - s0 + 1) * seg)
                for k in range(S5_PAR):
                    wr_ref[k, s * pitch:s * pitch + seg, :] = wr[part, k * LANES:(k + 1) * LANES]
                    wi_ref[k, s * pitch:s * pitch + seg, :] = wi[part, k * LANES:(k + 1) * LANES]
        ar = [jnp.broadcast_to(are_ref[d, j0 + k], (N_SEG, LANES)) for d, k in chains]
        ai = [jnp.broadcast_to(aim_ref[d, j0 + k], (N_SEG, LANES)) for d, k in chains]

        def step(i, carry, ar=ar, ai=ai):
            out = []
            for c, (d, k) in enumerate(chains):
                gr, gi = carry[c]
                j = (seg - 1 - i) if d == 1 else i
                src = (k, pl.ds(j, N_SEG, stride=pitch), slice(None))
                dst = (d, k, pl.ds(j, N_SEG, stride=pitch), slice(None))
                ngr = ar[c] * gr - ai[c] * gi + wr_ref[src]
                ngi = ar[c] * gi + ai[c] * gr + wi_ref[src]
                hr_ref[dst] = ngr
                hi_ref[dst] = ngi
                out.append((ngr, ngi))
            return tuple(out)

        zero = jnp.zeros((N_SEG, LANES), F32)
        fin = lax.fori_loop(0, seg, step, tuple((zero, zero) for _ in chains), unroll=SCAN_UNROLL)
        for d in (0, 1):
            ins = []
            last = 0 if d == 1 else seg - 1
            for k in range(S5_PAR):
                end_r, end_i = fin[d * S5_PAR + k]
                tot_r = pw_ref[d, 0, j0 + k, last:last + 1, :]
                tot_i = pw_ref[d, 1, j0 + k, last:last + 1, :]
                hin_r, hin_i = h0_ref[0, d, 0, j0 + k], h0_ref[0, d, 1, j0 + k]
                rows = [None] * N_SEG
                for s in (range(N_SEG - 1, -1, -1) if d == 1 else range(N_SEG)):
                    rows[s] = (hin_r, hin_i)
                    hin_r, hin_i = (tot_r[0:1] * hin_r - tot_i[0:1] * hin_i + end_r[s:s + 1],
                                    tot_r[0:1] * hin_i + tot_i[0:1] * hin_r + end_i[s:s + 1])
                ht_ref[0, d, 0, j0 + k] = hin_r
                ht_ref[0, d, 1, j0 + k] = hin_i
                ins.append(rows)
            gr_rows, gi_rows = [], []
            for s in range(N_SEG):
                src = slice(s * pitch, s * pitch + seg)
                gr_parts, gi_parts = [], []
                for k in range(S5_PAR):
                    in_r, in_i = ins[k][s]
                    pr, pi = pw_ref[d, 0, j0 + k], pw_ref[d, 1, j0 + k]
                    gr_parts.append((hr_ref[d, k, src, :] + pr * in_r - pi * in_i).astype(BF16))
                    gi_parts.append((hi_ref[d, k, src, :] + pr * in_i + pi * in_r).astype(BF16))
                gr_rows.append(jnp.concatenate(gr_parts, axis=-1))
                gi_rows.append(jnp.concatenate(gi_parts, axis=-1))
                if len(gr_rows) == group:
                    rows = slice((s + 1 - group) * seg, (s + 1) * seg)
                    acc_ref[rows, :] += (
                        jnp.dot(jnp.concatenate(gr_rows, axis=0), cd_ref[d, 0, cols, :],
                                preferred_element_type=F32)
                        + jnp.dot(jnp.concatenate(gi_rows, axis=0), cd_ref[d, 1, cols, :],
                                  preferred_element_type=F32))
                    gr_rows, gi_rows = [], []
    y = _gelu_tanh(acc_ref[...])
    y = y * _sigmoid(jnp.dot(y.astype(BF16), gw_ref[...], preferred_element_type=F32) + gb_ref[...])
    y_ref[0, 0] = y[:, :LANES].astype(BF16)
    y_ref[0, 1] = y[:, LANES:].astype(BF16)


def _s5(p, h0, log_a, bd, cd, dvec, glu_w, glu_b, *, u_blk0):
    bsz, _, t, _ = p.shape
    seg, pitch = _seg_rows(t)
    la_re, la_im = log_a
    pos = jnp.arange(seg, dtype=F32)
    n = jnp.stack([pos + 1.0, seg - pos]).reshape(2, 1, seg, 1)
    mag = jnp.exp(n * la_re)
    pw = jnp.stack([mag * jnp.cos(n * la_im), mag * jnp.sin(n * la_im)], axis=1)
    a_re, a_im = pw[0, 0, :, 0:1, :], pw[0, 1, :, 0:1, :]
    a_re = jnp.stack([a_re, pw[1, 0, :, seg - 1:seg, :]])
    a_im = jnp.stack([a_im, pw[1, 1, :, seg - 1:seg, :]])
    full = lambda a: pl.BlockSpec(a.shape, lambda b: (0,) * a.ndim)
    st_spec = pl.BlockSpec((1, 2, 2, S5_NB, 1, LANES), lambda b: (b, 0, 0, 0, 0, 0))
    return pl.pallas_call(
        functools.partial(_s5_kernel, t=t),
        grid=(bsz,),
        in_specs=[pl.BlockSpec((1, 2, t, LANES), lambda b: (b, u_blk0 // 2, 0, 0)),
                  full(a_re), full(a_im), full(pw), full(bd), full(cd), full(dvec), full(glu_w), full(glu_b),
                  st_spec],
        out_specs=[pl.BlockSpec((1, 2, t, LANES), lambda b: (b, 0, 0, 0)), st_spec],
        out_shape=[jax.ShapeDtypeStruct((bsz, 2, t, LANES), BF16), jax.ShapeDtypeStruct(h0.shape, F32)],
        scratch_shapes=[pltpu.VMEM((S5_PAR, N_SEG * pitch, LANES), F32),
                        pltpu.VMEM((S5_PAR, N_SEG * pitch, LANES), F32),
                        pltpu.VMEM((2, S5_PAR, N_SEG * pitch, LANES), F32),
                        pltpu.VMEM((2, S5_PAR, N_SEG * pitch, LANES), F32),
                        pltpu.VMEM((t, S5_W), F32)],
        compiler_params=_cparams(("parallel",)),
        name="s5",
    )(p, a_re, a_im, pw, bd, cd, dvec, glu_w, glu_b, h0)


def _s5_params(lam_re, lam_im, log_step, b_re, b_im, c_re, c_im):
    step = jnp.exp(log_step)[..., None]
    mag = jnp.exp(lam_re * step)
    ar, ai = mag * jnp.cos(lam_im * step), mag * jnp.sin(lam_im * step)
    den = lam_re * lam_re + lam_im * lam_im
    zr = ((ar - 1) * lam_re + ai * lam_im) / den
    zi = (ai * lam_re - (ar - 1) * lam_im) / den
    czr = c_re * zr[:, :, None, :] - c_im * zi[:, :, None, :]
    czi = c_re * zi[:, :, None, :] + c_im * zr[:, :, None, :]
    eye = jnp.eye(S5_GROUPS, dtype=F32)

    def in_mat(m):
        return jnp.einsum("gpk,gh->gkhp", m, eye).reshape(S5_W, S5_NSTATE)

    def out_mat(m):
        return jnp.einsum("dgkp,gh->dgphk", m, eye).reshape(2, S5_NSTATE, S5_W)

    bd = jnp.stack([in_mat(b_re), in_mat(b_im)], axis=0).astype(BF16)
    cd = jnp.stack([out_mat(czr), -out_mat(czi)], axis=1).astype(BF16)
    shape = (2, S5_NB, 1, LANES)
    log_a = ((lam_re * step).reshape(shape), (lam_im * step).reshape(shape))
    return log_a, bd, cd


OD_U0 = 5 * HG_HEADS


def _odd_weights(w_in, lower_bound, hg_norm_g, lam_re, lam_im, log_step, b_re, b_im, c_re, c_im, s5_d,
                 glu_w, glu_b):
    log_a, bd, cd = _s5_params(lam_re, lam_im, log_step, b_re, b_im, c_re, c_im)
    return dict(w_in=w_in.astype(BF16), lb=lower_bound, norm_g=hg_norm_g, log_a=log_a, bd=bd, cd=cd,
                dvec=s5_d.reshape(1, S5_W), glu_w=glu_w.astype(BF16), glu_b=glu_b.reshape(1, S5_W))


def _odd_scans(p, states, w, need_out=True):
    hg_h, s5_h = states
    o, hg_h = _hgrn(p, hg_h, w["lb"], w["norm_g"], need_out)
    y, s5_h = _s5(p, s5_h, w["log_a"], w["bd"], w["cd"], w["dvec"], w["glu_w"], w["glu_b"], u_blk0=OD_U0)
    return o, y, (hg_h, s5_h)


def _odd_zero_states(bsz):
    return (jnp.zeros((bsz, 2, HG_HEADS, LANES, LANES), F32),
            jnp.zeros((bsz, 2, 2, S5_NB, 1, LANES), F32))


PROJ_TILE = 512
MIX_TILE = 1024
FFN_TILE = 1024
FFN_COLS = 256


def kernel(x, c, ctx, c_ctx, w_mod, b_mod, norm_mix_g, norm_ffn_g, final_norm_g,
           ev_w_in, ev_w_out, ssd_conv_w, ssd_conv_b, ssd_dt_bias, ssd_a_log, ssd_d, ssd_norm_g,
           lru_conv_w, lru_conv_b, lru_w_a, lru_b_a, lru_w_i, lru_b_i, lru_lam,
           od_w_in, od_w_out, hg_lb_logits, hg_norm_g,
           s5_lam_re, s5_lam_im, s5_log_step, s5_b_re, s5_b_im, s5_c_re, s5_c_im, s5_d,
           s5_glu_w, s5_glu_b,
           ffn_w_gate, ffn_w_up, ffn_conv_w, ffn_conv_b, ffn_w_down):
    bsz, _, d = x.shape
    depth = w_mod.shape[0]
    prob = jax.nn.softmax(hg_lb_logits.astype(F32), axis=0)
    lower_bounds = (jnp.cumsum(prob, axis=0) - prob[0]).astype(hg_lb_logits.dtype)

    pad = (-(bsz + 1)) % SUBLANES
    cond = jnp.concatenate([c, c_ctx[None], jnp.zeros((pad, d), c.dtype)], axis=0)
    mods = _modulation(cond, w_mod, b_mod).transpose(0, 2, 1, 3)

    for layer in range(depth):
        last = layer == depth - 1
        j = layer // 2
        mod_x = mods[layer, :bsz]
        mod_c = mods[layer, bsz:bsz + 1]
        if layer % 2 == 0:
            w = _even_weights(ev_w_in[j], ssd_conv_w[j], ssd_conv_b[j], ssd_dt_bias[j], ssd_a_log[j], ssd_d[j],
                              lru_conv_w[j], lru_conv_b[j], lru_w_a[j], lru_b_a[j], lru_w_i[j], lru_b_i[j],
                              lru_lam[j])
            scans, zero_states = _even_scans, _even_zero_states
            w_out, norm_gain, norm_first = ev_w_out[j].astype(BF16), ssd_norm_g[j], True
        else:
            w = _odd_weights(od_w_in[j], lower_bounds[layer], hg_norm_g[j], s5_lam_re[j], s5_lam_im[j],
                             s5_log_step[j], s5_b_re[j], s5_b_im[j], s5_c_re[j], s5_c_im[j], s5_d[j],
                             s5_glu_w[j], s5_glu_b[j])
            scans, zero_states = _odd_scans, _odd_zero_states
            w_out, norm_gain, norm_first = od_w_out[j].astype(BF16), hg_norm_g[j], False
        ffn_w = _ffn_weights(ffn_w_gate[layer], ffn_w_up[layer], ffn_conv_w[layer], ffn_conv_b[layer],
                             ffn_w_down[layer], FFN_COLS)

        p_c = _project(ctx, mod_c, norm_mix_g[layer], w["w_in"], PROJ_TILE)
        a_c, b_c, states = scans(p_c, zero_states(bsz), w, need_out=not last)
        p_x = _project(x, mod_x, norm_mix_g[layer], w["w_in"], PROJ_TILE)
        a_x, b_x, _ = scans(p_x, states, w)
        x = _mix_out(x, a_x, b_x, mod_x, norm_gain, w_out, norm_first, MIX_TILE)
        x = _conv_ffn(x, mod_x, norm_ffn_g[layer], final_norm_g, *ffn_w, grid_conv=True, final_norm=last,
                      tile=FFN_TILE, tf=FFN_COLS)
        if not last:
            ctx = _mix_out(ctx, a_c, b_c, mod_c, norm_gain, w_out, norm_first, MIX_TILE)
            ctx = _conv_ffn(ctx, mod_c, norm_ffn_g[layer], final_norm_g, *ffn_w, grid_conv=False,
                            final_norm=False, tile=FFN_TILE, tf=FFN_COLS)
    return x
```
